```python
import jax, jax.numpy as jnp
from jax import lax
import numpy as np

D_MODEL = 1024
BATCH = 2
SEQ = 16384
DEPTH = 2

CONV_DIM = 512
CONV_WIDTH = 3
NSA_HEADS = 8
NSA_GROUPS = 2
HEADS_PER_GROUP = NSA_HEADS // NSA_GROUPS
HEAD_DIM = 64
ROT_DIM = HEAD_DIM // 4
ROPE_THETA = 500000.0
CMP_BLOCK = 32
CMP_STRIDE = 16
CMP_HIDDEN = 256
SLC_BLOCK = 64
N_SELECT = 16
WINDOW = 512
MEM_LEN = 256
MEM_HEADS = 4
MEM_HEAD_DIM = 128
D_FF = 2816
Q_BLOCK = 128
EPS = 1e-6
NEG = -1e30
BIG = 1e30
SPLIT_SIZES = (CONV_DIM, CONV_DIM, CONV_DIM, NSA_HEADS * HEAD_DIM, 6 * NSA_GROUPS * HEAD_DIM, 3 * NSA_HEADS, MEM_HEADS * MEM_HEAD_DIM, 3 * D_MODEL)
MIX_IN = 3 * CONV_DIM + NSA_HEADS * HEAD_DIM + 6 * NSA_GROUPS * HEAD_DIM + 3 * NSA_HEADS + MEM_HEADS * MEM_HEAD_DIM + 3 * D_MODEL

kernel_name = 'hybrid_conv_nsa_memory_macaron_block'


def rms_norm(x, g):
    xf = x.astype(jnp.float32)
    y = xf * lax.rsqrt(jnp.mean(xf * xf, axis=-1, keepdims=True) + EPS)
    return (y * g.astype(jnp.float32)).astype(x.dtype)


def swiglu(h, w_in, w_out):
    a, b = jnp.split(h @ w_in, 2, axis=-1)
    return (jax.nn.silu(a) * b) @ w_out


def rope_tables(positions):
    inv_freq = ROPE_THETA ** (-jnp.arange(0, ROT_DIM, 2, dtype=jnp.float32) / ROT_DIM)
    ang = positions.astype(jnp.float32)[..., None] * inv_freq
    return jnp.cos(ang)[:, :, None, :], jnp.sin(ang)[:, :, None, :]


def partial_rope(t, cos, sin):
    half = ROT_DIM // 2
    t1, t2, rest = t[..., :half], t[..., half:ROT_DIM], t[..., ROT_DIM:]
    return jnp.concatenate([t1 * cos - t2 * sin, t2 * cos + t1 * sin, rest], axis=-1)


def masked_softmax(s, mask):
    s = jnp.where(mask, s.astype(jnp.float32), NEG)
    m = jnp.max(s, axis=-1, keepdims=True)
    p = jnp.where(mask, jnp.exp(s - m), 0.0)
    return p / jnp.maximum(jnp.sum(p, axis=-1, keepdims=True), 1e-30)


def short_conv(v, w):
    return lax.conv_general_dilated(v, w[:, None, :].astype(v.dtype), window_strides=(1,),
                                    padding=[(CONV_WIDTH - 1, 0)],
                                    dimension_numbers=('NWC', 'WIO', 'NWC'),
                                    feature_group_count=v.shape[-1])


def compress(t, pos_emb, w1, b1, w2):
    b, s, g, d = t.shape
    chunks = t.reshape(b, s // CMP_STRIDE, CMP_STRIDE, g, d)
    blocks = jnp.concatenate([chunks[:, :-1], chunks[:, 1:]], axis=2)
    blocks = blocks + pos_emb[None, None, :, None, :]
    flat = jnp.moveaxis(blocks, 3, 2).reshape(b, -1, g, CMP_BLOCK * d)
    return jax.nn.gelu(flat @ w1 + b1) @ w2


def cmp_to_slc_matrix(n_cmp, n_slc):
    i = jnp.arange(n_cmp)[:, None] * CMP_STRIDE
    j = jnp.arange(n_slc)[None, :] * SLC_BLOCK
    ov = jnp.minimum(i + CMP_BLOCK, j + SLC_BLOCK) - jnp.maximum(i, j)
    return jnp.maximum(ov, 0).astype(jnp.float32) / CMP_BLOCK


def nsa_attention(q, q_rot, k_cmp, v_cmp, k_slc, v_slc, k_win, v_win, gates):
    b, s, h, d = q.shape
    n_cmp = k_cmp.shape[1]
    n_slc = s // SLC_BLOCK
    top = min(N_SELECT, n_slc)
    scale = HEAD_DIM ** -0.5
    q_plain_g = q.reshape(b, s, NSA_GROUPS, HEADS_PER_GROUP, d)
    q_rot_g = q_rot.reshape(b, s, NSA_GROUPS, HEADS_PER_GROUP, d)
    kc = jnp.transpose(k_cmp, (0, 2, 1, 3))
    vc = jnp.transpose(v_cmp, (0, 2, 1, 3))
    cmp_end = jnp.arange(n_cmp) * CMP_STRIDE + CMP_BLOCK - 1
    overlap = cmp_to_slc_matrix(n_cmp, n_slc)
    ks_blocks = k_slc.reshape(b, n_slc, SLC_BLOCK, NSA_GROUPS, d).transpose(0, 3, 1, 2, 4)
    vs_blocks = v_slc.reshape(b, n_slc, SLC_BLOCK, NSA_GROUPS, d).transpose(0, 3, 1, 2, 4)
    pad = ((0, 0), (WINDOW, 0), (0, 0), (0, 0))
    kw_pad = jnp.pad(k_win, pad)
    vw_pad = jnp.pad(v_win, pad)
    bi = jnp.arange(b)[:, None, None, None]
    gi = jnp.arange(NSA_GROUPS)[None, :, None, None]
    blk = jnp.arange(n_slc)

    def block(start):
        t = start + jnp.arange(Q_BLOCK)
        qp = lax.dynamic_slice_in_dim(q_plain_g, start, Q_BLOCK, axis=1)
        qr = lax.dynamic_slice_in_dim(q_rot_g, start, Q_BLOCK, axis=1)
        s_c = jnp.einsum('bqghd,bgnd->bghqn', qp, kc) * scale
        p_c = masked_softmax(s_c, cmp_end[None, :] <= t[:, None])
        o_c = jnp.einsum('bghqn,bgnd->bqghd', p_c.astype(vc.dtype), vc)
        imp = jnp.einsum('bghqn,nj->bgqj', p_c, overlap)
        cur = (t // SLC_BLOCK)[:, None]
        forced = (blk == 0) | (blk == cur) | (blk == cur - 1)
        imp = jnp.where(blk > cur, NEG, jnp.where(forced, BIG, imp))
        _, idx = lax.top_k(imp, top)
        ks = ks_blocks[bi, gi, idx].reshape(b, NSA_GROUPS, Q_BLOCK, top * SLC_BLOCK, d)
        vs = vs_blocks[bi, gi, idx].reshape(b, NSA_GROUPS, Q_BLOCK, top * SLC_BLOCK, d)
        kpos = (idx[..., None] * SLC_BLOCK + jnp.arange(SLC_BLOCK)).reshape(b, NSA_GROUPS, Q_BLOCK, top * SLC_BLOCK)
        m_s = kpos <= t[None, None, :, None]
        s_s = jnp.einsum('bqghd,bgqld->bghql', qr, ks) * scale
        p_s = masked_softmax(s_s, m_s[:, :, None])
        o_s = jnp.einsum('bghql,bgqld->bqghd', p_s.astype(vs.dtype), vs)
        kw = lax.dynamic_slice_in_dim(kw_pad, start, WINDOW + Q_BLOCK, axis=1)
        vw = lax.dynamic_slice_in_dim(vw_pad, start, WINDOW + Q_BLOCK, axis=1)
        wpos = start - WINDOW + jnp.arange(WINDOW + Q_BLOCK)
        m_w = (wpos[None, :] >= 0) & (wpos[None, :] <= t[:, None]) & (wpos[None, :] > t[:, None] - WINDOW)
        s_w = jnp.einsum('bqghd,bkgd->bghqk', qr, kw) * scale
        p_w = masked_softmax(s_w, m_w)
        o_w = jnp.einsum('bghqk,bkgd->bqghd', p_w.astype(vw.dtype), vw)
        g = lax.dynamic_slice_in_dim(gates, start, Q_BLOCK, axis=1)
        return g[..., 0:1] * o_c + g[..., 1:2] * o_s + g[..., 2:3] * o_w

    out = lax.map(block, jnp.arange(s // Q_BLOCK) * Q_BLOCK)
    return jnp.moveaxis(out, 0, 1).reshape(b, s, h * d)


def token_mixer(h, mem, cos, sin, mem_g, w_in, conv_w, cmp_pos_k, cmp_pos_v,
                ck_w1, ck_b1, ck_w2, cv_w1, cv_b1, cv_w2,
                w_mem_kv, w_br_conv, w_br_nsa, w_br_mem, w_out):
    b, s, _ = h.shape
    points, acc = [], 0
    for n in SPLIT_SIZES[:-1]:
        acc += n
        points.append(acc)
    u, bg, cg, q, kv, nsa_g, q_mem, merge_g = jnp.split(h @ w_in, points, axis=-1)
    cos = cos.astype(h.dtype)
    sin = sin.astype(h.dtype)
    y_conv = bg * short_conv(cg * u, conv_w)
    q = q.reshape(b, s, NSA_HEADS, HEAD_DIM)
    kv = kv.reshape(b, s, 6, NSA_GROUPS, HEAD_DIM)
    k_cmp = compress(kv[:, :, 0], cmp_pos_k, ck_w1, ck_b1, ck_w2)
    v_cmp = compress(kv[:, :, 1], cmp_pos_v, cv_w1, cv_b1, cv_w2)
    k_slc = partial_rope(kv[:, :, 2], cos, sin)
    k_win = partial_rope(kv[:, :, 4], cos, sin)
    q_rot = partial_rope(q, cos, sin)
    gates = jax.nn.sigmoid(nsa_g).reshape(b, s, NSA_GROUPS, HEADS_PER_GROUP, 3)
    y_nsa = nsa_attention(q, q_rot, k_cmp, v_cmp, k_slc, kv[:, :, 3], k_win, kv[:, :, 5], gates)
    m_len = mem.shape[1]
    mkv = (rms_norm(mem, mem_g) @ w_mem_kv).reshape(b, m_len, 2, MEM_HEADS, MEM_HEAD_DIM)
    q_m = q_mem.reshape(b, s, MEM_HEADS, MEM_HEAD_DIM)
    s_m = jnp.einsum('bshd,bmhd->bhsm', q_m, mkv[:, :, 0]).astype(jnp.float32) * (MEM_HEAD_DIM ** -0.5)
    p_m = jax.nn.softmax(s_m, axis=-1)
    y_mem = jnp.einsum('bhsm,bmhd->bshd', p_m.astype(mkv.dtype), mkv[:, :, 1]).reshape(b, s, MEM_HEADS * MEM_HEAD_DIM)
    g_conv, g_nsa, g_mem = jnp.split(jax.nn.sigmoid(merge_g), 3, axis=-1)
    merged = g_conv * (y_conv @ w_br_conv) + g_nsa * (y_nsa @ w_br_nsa) + g_mem * (y_mem @ w_br_mem)
    return merged @ w_out


def setup_inputs(seed: int = 0) -> dict:
    key = jax.random.key(seed)
    keys = iter(jax.random.split(key, 40))
    L = DEPTH

    def dense(shape, fan_in):
        return jax.random.normal(next(keys), shape, jnp.float32) * fan_in ** -0.5

    def gain(shape):
        return 1.0 + 0.05 * jax.random.normal(next(keys), shape, jnp.float32)

    def small(shape, scale):
        return scale * jax.random.normal(next(keys), shape, jnp.float32)

    x = jax.random.normal(next(keys), (BATCH, SEQ, D_MODEL), jnp.float32)
    mem = jax.random.normal(next(keys), (BATCH, MEM_LEN, D_MODEL), jnp.float32)
    offset = jax.random.randint(next(keys), (BATCH, 1), 0, 4096, dtype=jnp.int32)
    positions = (jnp.arange(SEQ, dtype=jnp.int32)[None, :] + offset).astype(jnp.int32)
    return {
        'x': x,
        'mem': mem,
        'positions': positions,
        'ffn1_norm_pre': gain((L, D_MODEL)),
        'ffn1_norm_post': gain((L, D_MODEL)),
        'ffn1_w_in': dense((L, D_MODEL, 2 * D_FF), D_MODEL),
        'ffn1_w_out': dense((L, D_FF, D_MODEL), D_FF),
        'mix_norm_pre': gain((L, D_MODEL)),
        'mix_norm_post': gain((L, D_MODEL)),
        'mem_norm': gain((L, D_MODEL)),
        'w_mix_in': dense((L, D_MODEL, MIX_IN), D_MODEL),
        'conv_w': dense((L, CONV_WIDTH, CONV_DIM), CONV_WIDTH),
        'cmp_pos_k': small((L, CMP_BLOCK, HEAD_DIM), 0.1),
        'cmp_pos_v': small((L, CMP_BLOCK, HEAD_DIM), 0.1),
        'cmp_k_w1': dense((L, CMP_BLOCK * HEAD_DIM, CMP_HIDDEN), CMP_BLOCK * HEAD_DIM),
        'cmp_k_b1': small((L, CMP_HIDDEN), 0.01),
        'cmp_k_w2': dense((L, CMP_HIDDEN, HEAD_DIM), CMP_HIDDEN),
        'cmp_v_w1': dense((L, CMP_BLOCK * HEAD_DIM, CMP_HIDDEN), CMP_BLOCK * HEAD_DIM),
        'cmp_v_b1': small((L, CMP_HIDDEN), 0.01),
        'cmp_v_w2': dense((L, CMP_HIDDEN, HEAD_DIM), CMP_HIDDEN),
        'w_mem_kv': dense((L, D_MODEL, 2 * MEM_HEADS * MEM_HEAD_DIM), D_MODEL),
        'w_branch_conv': dense((L, CONV_DIM, D_MODEL), CONV_DIM),
        'w_branch_nsa': dense((L, NSA_HEADS * HEAD_DIM, D_MODEL), NSA_HEADS * HEAD_DIM),
        'w_branch_mem': dense((L, MEM_HEADS * MEM_HEAD_DIM, D_MODEL), MEM_HEADS * MEM_HEAD_DIM),
        'w_mix_out': dense((L, D_MODEL, D_MODEL), D_MODEL),
        'ffn2_norm_pre': gain((L, D_MODEL)),
        'ffn2_norm_post': gain((L, D_MODEL)),
        'ffn2_w_in': dense((L, D_MODEL, 2 * D_FF), D_MODEL),
        'ffn2_w_out': dense((L, D_FF, D_MODEL), D_FF),
    }


def reference(x, mem, positions, ffn1_norm_pre, ffn1_norm_post, ffn1_w_in, ffn1_w_out,
              mix_norm_pre, mix_norm_post, mem_norm, w_mix_in, conv_w,
              cmp_pos_k, cmp_pos_v, cmp_k_w1, cmp_k_b1, cmp_k_w2, cmp_v_w1, cmp_v_b1, cmp_v_w2,
              w_mem_kv, w_branch_conv, w_branch_nsa, w_branch_mem, w_mix_out,
              ffn2_norm_pre, ffn2_norm_post, ffn2_w_in, ffn2_w_out):
    cos, sin = rope_tables(positions)
    for l in range(DEPTH):
        h = rms_norm(x, ffn1_norm_pre[l])
        x = x + 0.5 * rms_norm(swiglu(h, ffn1_w_in[l], ffn1_w_out[l]), ffn1_norm_post[l])
        h = rms_norm(x, mix_norm_pre[l])
        y = token_mixer(h, mem, cos, sin, mem_norm[l], w_mix_in[l], conv_w[l],
                        cmp_pos_k[l], cmp_pos_v[l], cmp_k_w1[l], cmp_k_b1[l], cmp_k_w2[l],
                        cmp_v_w1[l], cmp_v_b1[l], cmp_v_w2[l], w_mem_kv[l],
                        w_branch_conv[l], w_branch_nsa[l], w_branch_mem[l], w_mix_out[l])
        x = x + rms_norm(y, mix_norm_post[l])
        h = rms_norm(x, ffn2_norm_pre[l])
        x = x + 0.5 * rms_norm(swiglu(h, ffn2_w_in[l], ffn2_w_out[l]), ffn2_norm_post[l])
    return x
```

```python
import functools

import jax
import jax.numpy as jnp
import numpy as np
from jax import lax
from jax.experimental import pallas as pl
from jax.experimental.pallas import tpu as pltpu

D_MODEL = 1024
DEPTH = 2
CONV_DIM = 512
CONV_WIDTH = 3
NSA_HEADS = 8
NSA_GROUPS = 2
HEADS_PER_GROUP = NSA_HEADS // NSA_GROUPS
HEAD_DIM = 64
ROT_DIM = HEAD_DIM // 4
ROPE_THETA = 500000.0
CMP_BLOCK = 32
CMP_STRIDE = 16
CMP_HIDDEN = 256
SLC_BLOCK = 64
N_SELECT = 16
WINDOW = 512
MEM_HEADS = 4
MEM_HEAD_DIM = 128
D_FF = 2816
EPS = 1e-6
NEG = -1e30
BIG = 1e30
REMOVED = -3e38

LANES = 128
F32 = jnp.float32
BF16 = jnp.bfloat16

FFN_TM = 512
MIX_TM = 512
MERGE_TM = 512
HALO = 8
ATT_T = 256
BLK_PER_T = ATT_T // SLC_BLOCK
WIN_TILES = WINDOW // ATT_T

C_U = 0
C_BG = C_U + CONV_DIM
C_CG = C_BG + CONV_DIM
C_Q = C_CG + CONV_DIM
C_KSLC = C_Q + NSA_HEADS * LANES
C_KWIN = C_KSLC + NSA_GROUPS * LANES
C_VSLC = C_KWIN + NSA_GROUPS * LANES
C_VWIN = C_VSLC + NSA_GROUPS * LANES
C_KCR = C_VWIN + NSA_GROUPS * LANES
C_VCR = C_KCR + LANES
C_GATE = C_VCR + LANES
C_QMEM = C_GATE + NSA_GROUPS * LANES
C_MERGE = C_QMEM + MEM_HEADS * MEM_HEAD_DIM
C_TOTAL = C_MERGE + 3 * D_MODEL

VMEM_LIMIT = 56 * 1024 * 1024


def _params(n_axes):
    return pltpu.CompilerParams(dimension_semantics=("arbitrary",) * n_axes,
                                vmem_limit_bytes=VMEM_LIMIT)


def _resident(shape, index_map):
    return pl.BlockSpec(shape, index_map, pipeline_mode=pl.Buffered(1))


def _rms(x, g):
    return x * lax.rsqrt(jnp.mean(x * x, axis=-1, keepdims=True) + EPS) * g


def _dot(a, b):
    return jnp.dot(a, b, preferred_element_type=F32)


def _ffn_body(x_ref, gpre_ref, gpost_ref, wa_ref, wb_ref, wo_ref, o_ref):
    x = x_ref[...]
    h = _rms(x, gpre_ref[...]).astype(BF16)
    a = _dot(h, wa_ref[...])
    b = _dot(h, wb_ref[...])
    g = (a * jax.nn.sigmoid(a) * b).astype(BF16)
    y = _dot(g, wo_ref[...])
    o_ref[...] = x + 0.5 * _rms(y, gpost_ref[...])


def _ffn(x, gpre, gpost, wa, wb, wo):
    t = x.shape[0]
    row = lambda i: (i, 0)
    fix = lambda i: (0, 0)
    return pl.pallas_call(
        _ffn_body,
        grid=(t // FFN_TM,),
        in_specs=[pl.BlockSpec((FFN_TM, D_MODEL), row),
                  _resident((1, D_MODEL), fix), _resident((1, D_MODEL), fix),
                  _resident((D_MODEL, D_FF), fix), _resident((D_MODEL, D_FF), fix),
                  _resident((D_FF, D_MODEL), fix)],
        out_specs=pl.BlockSpec((FFN_TM, D_MODEL), row),
        out_shape=jax.ShapeDtypeStruct((t, D_MODEL), F32),
        compiler_params=_params(1),
        name="ffn",
    )(x, gpre, gpost, wa, wb, wo)


def _memkv_body(mem_ref, g_ref, w_ref, mk_ref, mv_ref):
    h = _rms(mem_ref[0], g_ref[...]).astype(BF16)
    kv = _dot(h, w_ref[...])
    half = MEM_HEADS * MEM_HEAD_DIM
    mk_ref[0] = kv[:, :half].astype(BF16)
    mv_ref[0] = kv[:, half:].astype(BF16)


def _memkv(mem, g, w):
    b, m, _ = mem.shape
    half = MEM_HEADS * MEM_HEAD_DIM
    out = jax.ShapeDtypeStruct((b, m, half), BF16)
    return pl.pallas_call(
        _memkv_body,
        grid=(b,),
        in_specs=[pl.BlockSpec((1, m, D_MODEL), lambda i: (i, 0, 0)),
                  pl.BlockSpec((1, D_MODEL), lambda i: (0, 0)),
                  pl.BlockSpec((D_MODEL, 2 * half), lambda i: (0, 0))],
        out_specs=[pl.BlockSpec((1, m, half), lambda i: (i, 0, 0))] * 2,
        out_shape=[out, out],
        compiler_params=_params(1),
        name="memkv",
    )(mem, g, w)


def _mix_body(tiles_per_batch,
              x_ref, halo_ref, pos_ref, gpre_ref, tab_ref, w_ref, convw_ref, mk_ref, mv_ref,
              wbc_ref, wbm_ref,
              part_ref, gnsa_ref, qpt_ref, qrt_ref, kslc_ref, kwin_ref, vslct_ref, vwint_ref,
              kcr_ref, vcr_ref, gatest_ref,
              cu_scr):
    tm = MIX_TM
    i = pl.program_id(0)
    gpre = gpre_ref[...]
    h = _rms(x_ref[...], gpre).astype(BF16)
    hh = _rms(halo_ref[...], gpre).astype(BF16)

    def proj(hv, c0, width):
        return _dot(hv, w_ref[:, c0:c0 + width])

    cu = proj(h, C_CG, CONV_DIM) * proj(h, C_U, CONV_DIM)
    cu_h = proj(hh, C_CG, CONV_DIM) * proj(hh, C_U, CONV_DIM)
    first = (i % tiles_per_batch) == 0
    cu_scr[0:HALO, :] = jnp.where(first, 0.0, cu_h)
    cu_scr[HALO:HALO + tm, :] = cu
    cw = convw_ref[...]
    conv = (cu * cw[2:3, :]
            + cu_scr[HALO - 1:HALO - 1 + tm, :] * cw[1:2, :]
            + cu_scr[HALO - 2:HALO - 2 + tm, :] * cw[0:1, :])
    y_conv = (proj(h, C_BG, CONV_DIM) * conv).astype(BF16)

    ang = pos_ref[...].astype(F32) * tab_ref[0:1, :]
    cos_t = jnp.cos(ang)
    sin_t = jnp.sin(ang)
    sin_a = sin_t * tab_ref[1:2, :]
    sin_b = sin_t * tab_ref[2:3, :]

    def rope(t):
        half = ROT_DIM // 2
        return (t * cos_t + pltpu.roll(t, LANES - half, 1) * sin_a
                + pltpu.roll(t, half, 1) * sin_b)

    scale = HEAD_DIM ** -0.5
    for hd in range(NSA_HEADS):
        q = proj(h, C_Q + hd * LANES, LANES) * scale
        sl = slice(hd * LANES, (hd + 1) * LANES)
        for j in range(tm // ATT_T):
            rows = slice(j * ATT_T, (j + 1) * ATT_T)
            qpt_ref[0, sl, rows] = q[rows].T.astype(BF16)
            qrt_ref[0, sl, rows] = rope(q)[rows].T.astype(BF16)
    one_hot = tab_ref[3:4, :]
    for g in range(NSA_GROUPS):
        sl = slice(g * LANES, (g + 1) * LANES)
        kslc_ref[:, sl] = rope(proj(h, C_KSLC + g * LANES, LANES)).astype(BF16)
        kwin_ref[:, sl] = rope(proj(h, C_KWIN + g * LANES, LANES)).astype(BF16)
        vs = proj(h, C_VSLC + g * LANES, LANES) + one_hot
        vw = proj(h, C_VWIN + g * LANES, LANES) + one_hot
        gt = jax.nn.sigmoid(proj(h, C_GATE + g * LANES, LANES))
        for j in range(tm // ATT_T):
            rows = slice(j * ATT_T, (j + 1) * ATT_T)
            vslct_ref[0, j, sl, :] = vs[rows].T.astype(BF16)
            vwint_ref[0, j, sl, :] = vw[rows].T.astype(BF16)
            gatest_ref[0, sl, rows] = gt[rows].T
    kcr_ref[...] = proj(h, C_KCR, LANES)
    vcr_ref[...] = proj(h, C_VCR, LANES)

    mem_scale = MEM_HEAD_DIM ** -0.5
    outs = []
    for hd in range(MEM_HEADS):
        sl = slice(hd * MEM_HEAD_DIM, (hd + 1) * MEM_HEAD_DIM)
        qm = (proj(h, C_QMEM + hd * MEM_HEAD_DIM, MEM_HEAD_DIM) * mem_scale).astype(BF16)
        s = lax.dot_general(qm, mk_ref[0, :, sl], (((1,), (1,)), ((), ())),
                            preferred_element_type=F32)
        p = jnp.exp(s - jnp.max(s, axis=-1, keepdims=True))
        l = jnp.sum(p, axis=-1, keepdims=True)
        outs.append((_dot(p.astype(BF16), mv_ref[0, :, sl]) / l).astype(BF16))
    y_mem = jnp.concatenate(outs, axis=1)

    g_conv = jax.nn.sigmoid(proj(h, C_MERGE, D_MODEL))
    g_nsa = jax.nn.sigmoid(proj(h, C_MERGE + D_MODEL, D_MODEL))
    g_mem = jax.nn.sigmoid(proj(h, C_MERGE + 2 * D_MODEL, D_MODEL))
    part_ref[...] = g_conv * _dot(y_conv, wbc_ref[...]) + g_mem * _dot(y_mem, wbm_ref[...])
    gnsa_ref[...] = g_nsa.astype(BF16)


def _mixin(x, pos, gpre, tab, w_main, convw, mk, mv, wbc, wbm, batch, seq):
    t = batch * seq
    tm = MIX_TM
    tpb = seq // tm
    nt = tm // ATT_T
    row = lambda i: (i, 0)
    fix = lambda i: (0, 0)
    bt = lambda i: (i // tpb, 0, i % tpb)
    mem_len = mk.shape[1]
    half = MEM_HEADS * MEM_HEAD_DIM
    out_shape = [
        jax.ShapeDtypeStruct((t, D_MODEL), F32),
        jax.ShapeDtypeStruct((t, D_MODEL), BF16),
        jax.ShapeDtypeStruct((batch, NSA_HEADS * LANES, seq), BF16),
        jax.ShapeDtypeStruct((batch, NSA_HEADS * LANES, seq), BF16),
        jax.ShapeDtypeStruct((t, NSA_GROUPS * LANES), BF16),
        jax.ShapeDtypeStruct((t, NSA_GROUPS * LANES), BF16),
        jax.ShapeDtypeStruct((batch, seq // ATT_T, NSA_GROUPS * LANES, ATT_T), BF16),
        jax.ShapeDtypeStruct((batch, seq // ATT_T, NSA_GROUPS * LANES, ATT_T), BF16),
        jax.ShapeDtypeStruct((t, LANES), F32),
        jax.ShapeDtypeStruct((t, LANES), F32),
        jax.ShapeDtypeStruct((batch, NSA_GROUPS * LANES, seq), F32),
    ]
    vt_spec = pl.BlockSpec((1, nt, NSA_GROUPS * LANES, ATT_T), lambda i: (i // tpb, i % tpb, 0, 0))
    out_specs = [
        pl.BlockSpec((tm, D_MODEL), row),
        pl.BlockSpec((tm, D_MODEL), row),
        pl.BlockSpec((1, NSA_HEADS * LANES, tm), bt),
        pl.BlockSpec((1, NSA_HEADS * LANES, tm), bt),
        pl.BlockSpec((tm, NSA_GROUPS * LANES), row),
        pl.BlockSpec((tm, NSA_GROUPS * LANES), row),
        vt_spec, vt_spec,
        pl.BlockSpec((tm, LANES), row),
        pl.BlockSpec((tm, LANES), row),
        pl.BlockSpec((1, NSA_GROUPS * LANES, tm), bt),
    ]
    in_specs = [
        pl.BlockSpec((tm, D_MODEL), row),
        pl.BlockSpec((HALO, D_MODEL), lambda i: (jnp.maximum(i * (tm // HALO) - 1, 0), 0)),
        pl.BlockSpec((tm, 1), row),
        _resident((1, D_MODEL), fix),
        _resident((8, LANES), fix),
        _resident((D_MODEL, C_TOTAL), fix),
        _resident((8, CONV_DIM), fix),
        pl.BlockSpec((1, mem_len, half), lambda i: (i // tpb, 0, 0)),
        pl.BlockSpec((1, mem_len, half), lambda i: (i // tpb, 0, 0)),
        _resident((CONV_DIM, D_MODEL), fix),
        _resident((half, D_MODEL), fix),
    ]
    return pl.pallas_call(
        functools.partial(_mix_body, tpb),
        grid=(t // tm,),
        in_specs=in_specs,
        out_specs=out_specs,
        out_shape=out_shape,
        scratch_shapes=[pltpu.VMEM((HALO + tm, CONV_DIM), F32)],
        compiler_params=_params(1),
        name="mixin",
    )(x, x, pos, gpre, tab, w_main, convw, mk, mv, wbc, wbm)


def _cmp_body(transpose_out, r_ref, pe_ref, w1a_ref, w1b_ref, b1_ref, w2_ref, one_ref, o_ref):
    r = r_ref[0]
    n = r.shape[0]
    a = _dot((r + pe_ref[0:1, :]).astype(BF16), w1a_ref[...])
    b = _dot((r + pe_ref[1:2, :]).astype(BF16), w1b_ref[...])
    hid = a + pltpu.roll(b, n - 1, 0) + b1_ref[...]
    o = _dot(jax.nn.gelu(hid).astype(BF16), w2_ref[...])
    if transpose_out:
        o = o + one_ref[...]
        o_ref[0] = jnp.concatenate(
            [o[:, g * LANES:(g + 1) * LANES].T for g in range(NSA_GROUPS)], axis=0).astype(BF16)
    else:
        o_ref[0] = o.astype(BF16)


def _compress(r, pe, w1a, w1b, b1, w2, one, transpose_out):
    b, n, width = r.shape
    fix = lambda i: (0, 0)
    oshape = (b, NSA_GROUPS * LANES, n) if transpose_out else (b, n, NSA_GROUPS * LANES)
    return pl.pallas_call(
        functools.partial(_cmp_body, transpose_out),
        grid=(b,),
        in_specs=[pl.BlockSpec((1, n, width), lambda i: (i, 0, 0)),
                  pl.BlockSpec((8, width), fix),
                  pl.BlockSpec(w1a.shape, fix), pl.BlockSpec(w1b.shape, fix),
                  pl.BlockSpec(b1.shape, fix), pl.BlockSpec(w2.shape, fix),
                  pl.BlockSpec(one.shape, fix)],
        out_specs=pl.BlockSpec((1,) + oshape[1:], lambda i: (i, 0, 0)),
        out_shape=jax.ShapeDtypeStruct(oshape, BF16),
        compiler_params=_params(1),
        name="compress_v" if transpose_out else "compress_k",
    )(r, pe, w1a, w1b, b1, w2, one)


def _nsa_body(qpt_ref, qrt_ref, gt_ref, kc_ref, vct_ref, ovt_ref,
              kslc_ref, vslct_ref, kwin_ref, vwint_ref,
              y_ref,
              acc_ref, m_ref, selb_ref):
    tq = ATT_T
    hg = HEADS_PER_GROUP
    i = pl.program_id(2)
    start = i * tq
    t_row = start + lax.broadcasted_iota(jnp.int32, (1, tq), 1)

    def lane_cat(ref):
        return jnp.concatenate([ref[0, h * LANES:(h + 1) * LANES, :] for h in range(hg)], axis=1)

    qp4 = lane_cat(qpt_ref)
    qr4 = lane_cat(qrt_ref)

    n_cmp = kc_ref.shape[1]
    n_idx = lax.broadcasted_iota(jnp.int32, (n_cmp, tq), 0)
    mask_c = (n_idx * CMP_STRIDE + (CMP_BLOCK - 1)) <= t_row
    sc = _dot(kc_ref[0], qp4)
    psum = jnp.zeros((n_cmp, tq), F32)
    y_parts = []
    for h in range(hg):
        s = jnp.where(mask_c, sc[:, h * tq:(h + 1) * tq], NEG)
        m = jnp.max(s, axis=0, keepdims=True)
        e = jnp.where(mask_c, jnp.exp(s - m), 0.0)
        l = jnp.sum(e, axis=0, keepdims=True)
        p = e * (1.0 / jnp.maximum(l, 1e-30))
        psum = psum + p
        oc = _dot(vct_ref[0], p.astype(BF16))
        y_parts.append(oc[0:HEAD_DIM, :] * gt_ref[0, 3 * h:3 * h + 1, :])

    p_hi = psum.astype(BF16)
    p_lo = (psum - p_hi.astype(F32)).astype(BF16)
    imp = _dot(ovt_ref[...], p_hi) + _dot(ovt_ref[...], p_lo)

    n_slc = imp.shape[0]
    blk = lax.broadcasted_iota(jnp.int32, (n_slc, tq), 0)
    cur = t_row // SLC_BLOCK
    forced = (blk == 0) | (blk == cur) | (blk == cur - 1)
    v = jnp.where(blk > cur, NEG, jnp.where(forced, BIG, imp))
    for _ in range(N_SELECT):
        mx = jnp.max(v, axis=0, keepdims=True)
        first = jnp.min(jnp.where(v == mx, blk, n_slc), axis=0, keepdims=True)
        v = jnp.where(blk == first, REMOVED, v)
    selb_ref[...] = jnp.where((v == REMOVED) & (blk <= cur), 0.0, NEG)

    def reset():
        acc_ref[...] = jnp.zeros_like(acc_ref)
        m_ref[...] = jnp.full_like(m_ref, NEG)

    def attend(k, vt, bias):
        st = _dot(k, qr4)
        ps, alphas = [], []
        for h in range(hg):
            hs = slice(h * tq, (h + 1) * tq)
            s = st[:, hs] + bias
            m_old = m_ref[0:1, hs]
            m_new = jnp.maximum(m_old, jnp.max(s, axis=0, keepdims=True))
            alphas.append(jnp.exp(m_old - m_new))
            ps.append(jnp.exp(s - m_new).astype(BF16))
            m_ref[0:1, hs] = m_new
        alpha = jnp.concatenate(alphas, axis=1)
        acc_ref[...] = acc_ref[...] * alpha + _dot(vt, jnp.concatenate(ps, axis=1))

    def finish(branch):
        for h in range(hg):
            hs = slice(h * tq, (h + 1) * tq)
            o = acc_ref[0:HEAD_DIM, hs] * (1.0 / acc_ref[HEAD_DIM:HEAD_DIM + 1, hs])
            y_parts[h] = y_parts[h] + o * gt_ref[0, 3 * h + branch:3 * h + branch + 1, :]

    k_iota = lax.broadcasted_iota(jnp.int32, (tq, 1), 0)

    def sel_bias(kt):
        rows = [jnp.broadcast_to(selb_ref[pl.ds(kt * BLK_PER_T + j, 1), :], (SLC_BLOCK, tq))
                for j in range(BLK_PER_T)]
        return jnp.concatenate(rows, axis=0)

    reset()

    def slc_step(kt, carry):
        ks = pl.multiple_of(kt * tq, tq)
        attend(kslc_ref[0, pl.ds(ks, tq), :], vslct_ref[0, kt], sel_bias(kt))
        return carry

    lax.fori_loop(0, i, slc_step, 0)
    causal = (start + k_iota) <= t_row
    attend(kslc_ref[0, pl.ds(pl.multiple_of(start, tq), tq), :], vslct_ref[0, i],
           jnp.where(causal, sel_bias(i), NEG))
    finish(1)

    reset()
    for back in range(WIN_TILES, -1, -1):
        @pl.when(i >= back)
        def _():
            kt = i - back
            kpos = kt * tq + k_iota
            ok = (kpos <= t_row) & (kpos > t_row - WINDOW)
            ks = pl.multiple_of(kt * tq, tq)
            attend(kwin_ref[0, pl.ds(ks, tq), :], vwint_ref[0, kt], jnp.where(ok, 0.0, NEG))
    finish(2)

    y_t = jnp.concatenate(y_parts, axis=0)
    y_ref[0] = y_t.T.astype(BF16)


def _nsa(qpt, qrt, gt, kc, vct, ovt, kslc, vslct, kwin, vwint):
    b, _, seq = qpt.shape
    tq = ATT_T
    hg = HEADS_PER_GROUP
    n_cmp = kc.shape[1]
    nkt = seq // tq
    qspec = pl.BlockSpec((1, hg * LANES, tq), lambda bi, g, i: (bi, g, i))
    in_specs = [
        qspec, qspec,
        pl.BlockSpec((1, LANES, tq), lambda bi, g, i: (bi, g, i)),
        pl.BlockSpec((1, n_cmp, LANES), lambda bi, g, i: (bi, 0, g)),
        pl.BlockSpec((1, LANES, n_cmp), lambda bi, g, i: (bi, g, 0)),
        pl.BlockSpec(ovt.shape, lambda bi, g, i: (0, 0)),
        _resident((1, seq, LANES), lambda bi, g, i: (bi, 0, g)),
        _resident((1, nkt, LANES, tq), lambda bi, g, i: (bi, 0, g, 0)),
        _resident((1, seq, LANES), lambda bi, g, i: (bi, 0, g)),
        _resident((1, nkt, LANES, tq), lambda bi, g, i: (bi, 0, g, 0)),
    ]
    return pl.pallas_call(
        _nsa_body,
        grid=(b, NSA_GROUPS, seq // tq),
        in_specs=in_specs,
        out_specs=pl.BlockSpec((1, tq, hg * HEAD_DIM), lambda bi, g, i: (bi, i, g)),
        out_shape=jax.ShapeDtypeStruct((b, seq, NSA_HEADS * HEAD_DIM), BF16),
        scratch_shapes=[pltpu.VMEM((LANES, hg * tq), F32),
                        pltpu.VMEM((8, hg * tq), F32),
                        pltpu.VMEM((seq // SLC_BLOCK, tq), F32)],
        compiler_params=_params(3),
        name="nsa",
    )(qpt, qrt, gt, kc, vct, ovt, kslc, vslct, kwin, vwint)


def _merge_body(x_ref, part_ref, gnsa_ref, ynsa_ref, wbn_ref, wo_ref, gpost_ref, o_ref):
    merged = part_ref[...] + gnsa_ref[...].astype(F32) * _dot(ynsa_ref[...], wbn_ref[...])
    y = _dot(merged.astype(BF16), wo_ref[...])
    o_ref[...] = x_ref[...] + _rms(y, gpost_ref[...])


def _merge(x, part, gnsa, ynsa, wbn, wo, gpost):
    t = x.shape[0]
    tm = MERGE_TM
    row = lambda i: (i, 0)
    fix = lambda i: (0, 0)
    return pl.pallas_call(
        _merge_body,
        grid=(t // tm,),
        in_specs=[pl.BlockSpec((tm, D_MODEL), row), pl.BlockSpec((tm, D_MODEL), row),
                  pl.BlockSpec((tm, D_MODEL), row), pl.BlockSpec((tm, NSA_HEADS * HEAD_DIM), row),
                  _resident(wbn.shape, fix), _resident(wo.shape, fix), _resident((1, D_MODEL), fix)],
        out_specs=pl.BlockSpec((tm, D_MODEL), row),
        out_shape=jax.ShapeDtypeStruct((t, D_MODEL), F32),
        compiler_params=_params(1),
        name="merge",
    )(x, part, gnsa, ynsa, wbn, wo, gpost)


def _pad_lanes(w, groups, width):
    d = w.shape[0]
    w = w.reshape(d, groups, width)
    return jnp.pad(w, ((0, 0), (0, 0), (0, LANES - width))).reshape(d, groups * LANES)


def _mix_weight(w):
    o = 0
    u = w[:, o:o + CONV_DIM]; o += CONV_DIM
    bg = w[:, o:o + CONV_DIM]; o += CONV_DIM
    cg = w[:, o:o + CONV_DIM]; o += CONV_DIM
    q = w[:, o:o + NSA_HEADS * HEAD_DIM]; o += NSA_HEADS * HEAD_DIM
    kv = w[:, o:o + 6 * NSA_GROUPS * HEAD_DIM].reshape(D_MODEL, 6, NSA_GROUPS * HEAD_DIM)
    o += 6 * NSA_GROUPS * HEAD_DIM
    gate = w[:, o:o + 3 * NSA_HEADS]; o += 3 * NSA_HEADS
    qmem = w[:, o:o + MEM_HEADS * MEM_HEAD_DIM]; o += MEM_HEADS * MEM_HEAD_DIM
    merge = w[:, o:]
    slab = lambda c: _pad_lanes(kv[:, c], NSA_GROUPS, HEAD_DIM)
    cols = [u, bg, cg, _pad_lanes(q, NSA_HEADS, HEAD_DIM),
            slab(2), slab(4), slab(3), slab(5), kv[:, 0], kv[:, 1],
            _pad_lanes(gate, NSA_GROUPS, 3 * HEADS_PER_GROUP), qmem, merge]
    out = jnp.concatenate(cols, axis=1).astype(BF16)
    assert out.shape == (D_MODEL, C_TOTAL)
    return out


def _cmp_weights(pos_emb, w1, b1, w2):
    eye = jnp.eye(NSA_GROUPS, dtype=F32)
    w1r = w1.reshape(CMP_BLOCK, HEAD_DIM, CMP_HIDDEN)
    width = CMP_STRIDE * NSA_GROUPS * HEAD_DIM

    def spread(a):
        return jnp.einsum('pdj,gh->pgdhj', a, eye).reshape(width, NSA_GROUPS * CMP_HIDDEN).astype(BF16)

    def pe_row(p):
        return jnp.broadcast_to(p[:, None, :], (CMP_STRIDE, NSA_GROUPS, HEAD_DIM)).reshape(1, width)

    pe = jnp.concatenate([pe_row(pos_emb[:CMP_STRIDE]), pe_row(pos_emb[CMP_STRIDE:]),
                          jnp.zeros((6, width), F32)], axis=0)
    w2p = jnp.pad(w2, ((0, 0), (0, LANES - HEAD_DIM)))
    w2b = jnp.einsum('jc,gh->gjhc', w2p, eye).reshape(NSA_GROUPS * CMP_HIDDEN, NSA_GROUPS * LANES)
    return (pe, spread(w1r[:CMP_STRIDE]), spread(w1r[CMP_STRIDE:]),
            jnp.tile(b1, NSA_GROUPS)[None, :], w2b.astype(BF16))


def _rope_table():
    inv_freq = ROPE_THETA ** (-jnp.arange(0, ROT_DIM, 2, dtype=jnp.float32) / ROT_DIM)
    half = ROT_DIM // 2
    lane = np.arange(LANES)
    tab = jnp.zeros((8, LANES), F32)
    tab = tab.at[0, :ROT_DIM].set(jnp.tile(inv_freq, 2))
    tab = tab.at[1].set(jnp.asarray(np.where(lane < half, -1.0, 0.0), F32))
    tab = tab.at[2].set(jnp.asarray(np.where((lane >= half) & (lane < ROT_DIM), 1.0, 0.0), F32))
    tab = tab.at[3].set(jnp.asarray(np.where(lane == HEAD_DIM, 1.0, 0.0), F32))
    return tab


def _overlap_t(n_cmp, n_slc):
    i = jnp.arange(n_cmp)[:, None] * CMP_STRIDE
    j = jnp.arange(n_slc)[None, :] * SLC_BLOCK
    ov = jnp.minimum(i + CMP_BLOCK, j + SLC_BLOCK) - jnp.maximum(i, j)
    return (jnp.maximum(ov, 0).astype(F32) / CMP_BLOCK).T.astype(BF16)


def kernel(x, mem, positions, ffn1_norm_pre, ffn1_norm_post, ffn1_w_in, ffn1_w_out, mix_norm_pre, mix_norm_post, mem_norm, w_mix_in, conv_w, cmp_pos_k, cmp_pos_v, cmp_k_w1, cmp_k_b1, cmp_k_w2, cmp_v_w1, cmp_v_b1, cmp_v_w2, w_mem_kv, w_branch_conv, w_branch_nsa, w_branch_mem, w_mix_out, ffn2_norm_pre, ffn2_norm_post, ffn2_w_in, ffn2_w_out):
    batch, seq, _ = x.shape
    t = batch * seq
    n_chunks = seq // CMP_STRIDE
    xf = x.reshape(t, D_MODEL)
    pos = positions.reshape(t, 1)
    tab = _rope_table()
    ovt = _overlap_t(n_chunks, seq // SLC_BLOCK)
    one = jnp.tile(tab[3:4, :], (1, NSA_GROUPS))
    row = lambda g: g[None, :]

    for l in range(DEPTH):
        xf = _ffn(xf, row(ffn1_norm_pre[l]), row(ffn1_norm_post[l]),
                  ffn1_w_in[l][:, :D_FF].astype(BF16), ffn1_w_in[l][:, D_FF:].astype(BF16),
                  ffn1_w_out[l].astype(BF16))

        mk, mv = _memkv(mem, row(mem_norm[l]), w_mem_kv[l].astype(BF16))
        convw = jnp.pad(conv_w[l], ((0, 8 - CONV_WIDTH), (0, 0)))
        (part, gnsa, qpt, qrt, kslc, kwin, vslct, vwint, kcr, vcr, gt) = _mixin(
            xf, pos, row(mix_norm_pre[l]), tab, _mix_weight(w_mix_in[l]), convw, mk, mv,
            w_branch_conv[l].astype(BF16), w_branch_mem[l].astype(BF16), batch, seq)

        width = CMP_STRIDE * NSA_GROUPS * HEAD_DIM
        kc = _compress(kcr.reshape(batch, n_chunks, width),
                       *_cmp_weights(cmp_pos_k[l], cmp_k_w1[l], cmp_k_b1[l], cmp_k_w2[l]), one, False)
        vct = _compress(vcr.reshape(batch, n_chunks, width),
                        *_cmp_weights(cmp_pos_v[l], cmp_v_w1[l], cmp_v_b1[l], cmp_v_w2[l]), one, True)

        ynsa = _nsa(qpt, qrt, gt, kc, vct, ovt,
                    kslc.reshape(batch, seq, NSA_GROUPS * LANES), vslct,
                    kwin.reshape(batch, seq, NSA_GROUPS * LANES), vwint)

        xf = _merge(xf, part, gnsa, ynsa.reshape(t, NSA_HEADS * HEAD_DIM),
                    w_branch_nsa[l].astype(BF16), w_mix_out[l].astype(BF16), row(mix_norm_post[l]))

        xf = _ffn(xf, row(ffn2_norm_pre[l]), row(ffn2_norm_post[l]),
                  ffn2_w_in[l][:, :D_FF].astype(BF16), ffn2_w_in[l][:, D_FF:].astype(BF16),
                  ffn2_w_out[l].astype(BF16))
    return xf.reshape(batch, seq, D_MODEL)
```

```python
import functools

import jax
import jax.numpy as jnp
import numpy as np
from jax import lax
from jax.experimental import pallas as pl
from jax.experimental.pallas import tpu as pltpu

D_MODEL = 1024
DEPTH = 2
CONV_DIM = 512
CONV_WIDTH = 3
NSA_HEADS = 8
NSA_GROUPS = 2
HEADS_PER_GROUP = NSA_HEADS // NSA_GROUPS
HEAD_DIM = 64
ROT_DIM = HEAD_DIM // 4
ROPE_THETA = 500000.0
CMP_BLOCK = 32
CMP_STRIDE = 16
CMP_HIDDEN = 256
SLC_BLOCK = 64
N_SELECT = 16
WINDOW = 512
MEM_HEADS = 4
MEM_HEAD_DIM = 128
D_FF = 2816
EPS = 1e-6
NEG = -1e30
REMOVED = -3e38

LANES = 128
F32 = jnp.float32
BF16 = jnp.bfloat16

FFN_TM = 512
MIX_TM = 512
MERGE_TM = 512
HALO = 8
ATT_T = 256
WIN_TILES = WINDOW // ATT_T
KT_GROUP = 4
BIAS_ROWS = KT_GROUP * ATT_T // SLC_BLOCK
VT_ROWS = HEAD_DIM + BIAS_ROWS
LOG2_E = 1.4426950408889634
CMP_CHUNK = 256
E_PAD = 8
N_FORCED = 3

C_U = 0
C_BG = C_U + CONV_DIM
C_CG = C_BG + CONV_DIM
C_Q = C_CG + CONV_DIM
C_KSLC = C_Q + NSA_HEADS * LANES
C_KWIN = C_KSLC + NSA_GROUPS * LANES
C_VSLC = C_KWIN + NSA_GROUPS * LANES
C_VWIN = C_VSLC + NSA_GROUPS * LANES
C_KCR = C_VWIN + NSA_GROUPS * LANES
C_VCR = C_KCR + LANES
C_GATE = C_VCR + LANES
C_QMEM = C_GATE + NSA_GROUPS * LANES
C_MERGE = C_QMEM + MEM_HEADS * MEM_HEAD_DIM
C_TOTAL = C_MERGE + 3 * D_MODEL

VMEM_LIMIT = 56 * 1024 * 1024


def _params(n_axes):
    return pltpu.CompilerParams(dimension_semantics=("arbitrary",) * n_axes,
                                vmem_limit_bytes=VMEM_LIMIT)


def _resident(shape, index_map):
    return pl.BlockSpec(shape, index_map, pipeline_mode=pl.Buffered(1))


def _rms(x, g):
    return x * lax.rsqrt(jnp.mean(x * x, axis=-1, keepdims=True) + EPS) * g


def _dot(a, b):
    return jnp.dot(a, b, preferred_element_type=F32)


def _ffn_body(x_ref, gpre_ref, gpost_ref, wa_ref, wb_ref, wo_ref, o_ref):
    x = x_ref[...]
    h = _rms(x, gpre_ref[...]).astype(BF16)
    a = _dot(h, wa_ref[...])
    b = _dot(h, wb_ref[...])
    g = (a * jax.nn.sigmoid(a) * b).astype(BF16)
    y = _dot(g, wo_ref[...])
    o_ref[...] = x + 0.5 * _rms(y, gpost_ref[...])


def _ffn(x, gpre, gpost, wa, wb, wo):
    t = x.shape[0]
    row = lambda i: (i, 0)
    fix = lambda i: (0, 0)
    return pl.pallas_call(
        _ffn_body,
        grid=(t // FFN_TM,),
        in_specs=[pl.BlockSpec((FFN_TM, D_MODEL), row),
                  _resident((1, D_MODEL), fix), _resident((1, D_MODEL), fix),
                  _resident((D_MODEL, D_FF), fix), _resident((D_MODEL, D_FF), fix),
                  _resident((D_FF, D_MODEL), fix)],
        out_specs=pl.BlockSpec((FFN_TM, D_MODEL), row),
        out_shape=jax.ShapeDtypeStruct((t, D_MODEL), F32),
        compiler_params=_params(1),
        name="ffn",
    )(x, gpre, gpost, wa, wb, wo)


def _memkv_body(mem_ref, g_ref, w_ref, mk_ref, mv_ref):
    h = _rms(mem_ref[0], g_ref[...]).astype(BF16)
    kv = _dot(h, w_ref[...])
    half = MEM_HEADS * MEM_HEAD_DIM
    mk_ref[0] = kv[:, :half].astype(BF16)
    mv_ref[0] = kv[:, half:].astype(BF16)


def _memkv(mem, g, w):
    b, m, _ = mem.shape
    half = MEM_HEADS * MEM_HEAD_DIM
    out = jax.ShapeDtypeStruct((b, m, half), BF16)
    return pl.pallas_call(
        _memkv_body,
        grid=(b,),
        in_specs=[pl.BlockSpec((1, m, D_MODEL), lambda i: (i, 0, 0)),
                  pl.BlockSpec((1, D_MODEL), lambda i: (0, 0)),
                  pl.BlockSpec((D_MODEL, 2 * half), lambda i: (0, 0))],
        out_specs=[pl.BlockSpec((1, m, half), lambda i: (i, 0, 0))] * 2,
        out_shape=[out, out],
        compiler_params=_params(1),
        name="memkv",
    )(mem, g, w)


def _mix_body(tiles_per_batch,
              x_ref, halo_ref, pos_ref, gpre_ref, tab_ref, w_ref, convw_ref, mk_ref, mv_ref,
              wbc_ref, wbm_ref,
              part_ref, gnsa_ref, qpt_ref, qrt_ref, kslc_ref, kwin_ref, vslct_ref, vwint_ref,
              kcr_ref, vcr_ref, gatest_ref,
              cu_scr):
    tm = MIX_TM
    i = pl.program_id(0)
    gpre = gpre_ref[...]
    h = _rms(x_ref[...], gpre).astype(BF16)
    hh = _rms(halo_ref[...], gpre).astype(BF16)

    def proj(hv, c0, width):
        return _dot(hv, w_ref[:, c0:c0 + width])

    cu = proj(h, C_CG, CONV_DIM) * proj(h, C_U, CONV_DIM)
    cu_h = proj(hh, C_CG, CONV_DIM) * proj(hh, C_U, CONV_DIM)
    first = (i % tiles_per_batch) == 0
    cu_scr[0:HALO, :] = jnp.where(first, 0.0, cu_h)
    cu_scr[HALO:HALO + tm, :] = cu
    cw = convw_ref[...]
    conv = (cu * cw[2:3, :]
            + cu_scr[HALO - 1:HALO - 1 + tm, :] * cw[1:2, :]
            + cu_scr[HALO - 2:HALO - 2 + tm, :] * cw[0:1, :])
    y_conv = (proj(h, C_BG, CONV_DIM) * conv).astype(BF16)

    ang = pos_ref[...].astype(F32) * tab_ref[0:1, :]
    cos_t = jnp.cos(ang)
    sin_t = jnp.sin(ang)
    sin_a = sin_t * tab_ref[1:2, :]
    sin_b = sin_t * tab_ref[2:3, :]

    def rope(t):
        half = ROT_DIM // 2
        return (t * cos_t + pltpu.roll(t, LANES - half, 1) * sin_a
                + pltpu.roll(t, half, 1) * sin_b)

    scale = HEAD_DIM ** -0.5 * LOG2_E
    for hd in range(NSA_HEADS):
        q = proj(h, C_Q + hd * LANES, LANES) * scale
        sl = slice(hd * LANES, (hd + 1) * LANES)
        for j in range(tm // ATT_T):
            rows = slice(j * ATT_T, (j + 1) * ATT_T)
            qpt_ref[0, sl, rows] = q[rows].T.astype(BF16)
            qrt_ref[0, sl, rows] = rope(q)[rows].T.astype(BF16)
    one_hot = tab_ref[3:4, :]
    seq_row = (i % tiles_per_batch) * tm + lax.broadcasted_iota(jnp.int32, (tm, LANES), 0)
    lane = lax.broadcasted_iota(jnp.int32, (tm, LANES), 1)
    blk_hot = jnp.where(lane - HEAD_DIM == (seq_row // SLC_BLOCK) % BIAS_ROWS, 1.0, 0.0)
    for g in range(NSA_GROUPS):
        sl = slice(g * LANES, (g + 1) * LANES)
        kslc_ref[:, sl] = (rope(proj(h, C_KSLC + g * LANES, LANES)) + blk_hot).astype(BF16)
        kwin_ref[:, sl] = rope(proj(h, C_KWIN + g * LANES, LANES)).astype(BF16)
        vs = proj(h, C_VSLC + g * LANES, LANES) + one_hot
        vw = proj(h, C_VWIN + g * LANES, LANES) + one_hot
        gt = jax.nn.sigmoid(proj(h, C_GATE + g * LANES, LANES))
        for j in range(tm // ATT_T):
            rows = slice(j * ATT_T, (j + 1) * ATT_T)
            vslct_ref[0, j, g] = vs[rows].T[0:VT_ROWS].astype(BF16)
            vwint_ref[0, j, g] = vw[rows].T[0:VT_ROWS].astype(BF16)
            gatest_ref[0, sl, rows] = gt[rows].T
    kcr_ref[...] = proj(h, C_KCR, LANES)
    vcr_ref[...] = proj(h, C_VCR, LANES)

    mem_scale = MEM_HEAD_DIM ** -0.5
    outs = []
    for hd in range(MEM_HEADS):
        sl = slice(hd * MEM_HEAD_DIM, (hd + 1) * MEM_HEAD_DIM)
        qm = (proj(h, C_QMEM + hd * MEM_HEAD_DIM, MEM_HEAD_DIM) * mem_scale).astype(BF16)
        s = lax.dot_general(qm, mk_ref[0, :, sl], (((1,), (1,)), ((), ())),
                            preferred_element_type=F32)
        p = jnp.exp(s - jnp.max(s, axis=-1, keepdims=True))
        l = jnp.sum(p, axis=-1, keepdims=True)
        outs.append((_dot(p.astype(BF16), mv_ref[0, :, sl]) / l).astype(BF16))
    y_mem = jnp.concatenate(outs, axis=1)

    g_conv = jax.nn.sigmoid(proj(h, C_MERGE, D_MODEL))
    g_nsa = jax.nn.sigmoid(proj(h, C_MERGE + D_MODEL, D_MODEL))
    g_mem = jax.nn.sigmoid(proj(h, C_MERGE + 2 * D_MODEL, D_MODEL))
    part_ref[...] = g_conv * _dot(y_conv, wbc_ref[...]) + g_mem * _dot(y_mem, wbm_ref[...])
    gnsa_ref[...] = g_nsa.astype(BF16)


def _mixin(x, pos, gpre, tab, w_main, convw, mk, mv, wbc, wbm, batch, seq):
    t = batch * seq
    tm = MIX_TM
    tpb = seq // tm
    nt = tm // ATT_T
    row = lambda i: (i, 0)
    fix = lambda i: (0, 0)
    bt = lambda i: (i // tpb, 0, i % tpb)
    mem_len = mk.shape[1]
    half = MEM_HEADS * MEM_HEAD_DIM
    out_shape = [
        jax.ShapeDtypeStruct((t, D_MODEL), F32),
        jax.ShapeDtypeStruct((t, D_MODEL), BF16),
        jax.ShapeDtypeStruct((batch, NSA_HEADS * LANES, seq), BF16),
        jax.ShapeDtypeStruct((batch, NSA_HEADS * LANES, seq), BF16),
        jax.ShapeDtypeStruct((t, NSA_GROUPS * LANES), BF16),
        jax.ShapeDtypeStruct((t, NSA_GROUPS * LANES), BF16),
        jax.ShapeDtypeStruct((batch, seq // ATT_T, NSA_GROUPS, VT_ROWS, ATT_T), BF16),
        jax.ShapeDtypeStruct((batch, seq // ATT_T, NSA_GROUPS, VT_ROWS, ATT_T), BF16),
        jax.ShapeDtypeStruct((t, LANES), F32),
        jax.ShapeDtypeStruct((t, LANES), F32),
        jax.ShapeDtypeStruct((batch, NSA_GROUPS * LANES, seq), F32),
    ]
    vt_spec = pl.BlockSpec((1, nt, NSA_GROUPS, VT_ROWS, ATT_T), lambda i: (i // tpb, i % tpb, 0, 0, 0))
    out_specs = [
        pl.BlockSpec((tm, D_MODEL), row),
        pl.BlockSpec((tm, D_MODEL), row),
        pl.BlockSpec((1, NSA_HEADS * LANES, tm), bt),
        pl.BlockSpec((1, NSA_HEADS * LANES, tm), bt),
        pl.BlockSpec((tm, NSA_GROUPS * LANES), row),
        pl.BlockSpec((tm, NSA_GROUPS * LANES), row),
        vt_spec, vt_spec,
        pl.BlockSpec((tm, LANES), row),
        pl.BlockSpec((tm, LANES), row),
        pl.BlockSpec((1, NSA_GROUPS * LANES, tm), bt),
    ]
    in_specs = [
        pl.BlockSpec((tm, D_MODEL), row),
        pl.BlockSpec((HALO, D_MODEL), lambda i: (jnp.maximum(i * (tm // HALO) - 1, 0), 0)),
        pl.BlockSpec((tm, 1), row),
        _resident((1, D_MODEL), fix),
        _resident((8, LANES), fix),
        _resident((D_MODEL, C_TOTAL), fix),
        _resident((8, CONV_DIM), fix),
        pl.BlockSpec((1, mem_len, half), lambda i: (i // tpb, 0, 0)),
        pl.BlockSpec((1, mem_len, half), lambda i: (i // tpb, 0, 0)),
        _resident((CONV_DIM, D_MODEL), fix),
        _resident((half, D_MODEL), fix),
    ]
    return pl.pallas_call(
        functools.partial(_mix_body, tpb),
        grid=(t // tm,),
        in_specs=in_specs,
        out_specs=out_specs,
        out_shape=out_shape,
        scratch_shapes=[pltpu.VMEM((HALO + tm, CONV_DIM), F32)],
        compiler_params=_params(1),
        name="mixin",
    )(x, x, pos, gpre, tab, w_main, convw, mk, mv, wbc, wbm)


def _cmp_body(transpose_out, r_ref, pe_ref, w1a_ref, w1b_ref, b1_ref, w2_ref, one_ref, o_ref):
    r = r_ref[0]
    n = r.shape[0]
    a = _dot((r + pe_ref[0:1, :]).astype(BF16), w1a_ref[...])
    b = _dot((r + pe_ref[1:2, :]).astype(BF16), w1b_ref[...])
    hid = a + pltpu.roll(b, n - 1, 0) + b1_ref[...]
    o = _dot(jax.nn.gelu(hid).astype(BF16), w2_ref[...])
    if transpose_out:
        o = o + one_ref[...]
        o_ref[0] = jnp.concatenate(
            [o[:, g * LANES:(g + 1) * LANES].T for g in range(NSA_GROUPS)], axis=0).astype(BF16)
    else:
        o_ref[0] = o.astype(BF16)


def _compress(r, pe, w1a, w1b, b1, w2, one, transpose_out):
    b, n, width = r.shape
    fix = lambda i: (0, 0)
    oshape = (b, NSA_GROUPS * LANES, n) if transpose_out else (b, n, NSA_GROUPS * LANES)
    return pl.pallas_call(
        functools.partial(_cmp_body, transpose_out),
        grid=(b,),
        in_specs=[pl.BlockSpec((1, n, width), lambda i: (i, 0, 0)),
                  pl.BlockSpec((8, width), fix),
                  pl.BlockSpec(w1a.shape, fix), pl.BlockSpec(w1b.shape, fix),
                  pl.BlockSpec(b1.shape, fix), pl.BlockSpec(w2.shape, fix),
                  pl.BlockSpec(one.shape, fix)],
        out_specs=pl.BlockSpec((1,) + oshape[1:], lambda i: (i, 0, 0)),
        out_shape=jax.ShapeDtypeStruct(oshape, BF16),
        compiler_params=_params(1),
        name="compress_v" if transpose_out else "compress_k",
    )(r, pe, w1a, w1b, b1, w2, one)


def _nsa_body(qpt_ref, qrt_ref, gt_ref, kc_ref, vct_ref,
              kslc_ref, vslct_ref, kwin_ref, vwint_ref,
              y_ref,
              acc_ref, m_ref, selb_ref, sa_ref, sb_ref, e_ref, sc_ref, yacc_ref):
    tq = ATT_T
    hg = HEADS_PER_GROUP
    i = pl.program_id(2)
    start = i * tq
    t_row = start + lax.broadcasted_iota(jnp.int32, (1, tq), 1)

    def lane_cat(ref):
        return jnp.concatenate([ref[0, h * LANES:(h + 1) * LANES, :] for h in range(hg)], axis=1)

    qp4 = lane_cat(qpt_ref)
    qr4 = lane_cat(qrt_ref)

    any_c = t_row >= CMP_BLOCK - 1
    per_slc = SLC_BLOCK // CMP_STRIDE
    n_half = tq // LANES
    cur = t_row // SLC_BLOCK
    e_ref[:, 0:E_PAD, :] = jnp.zeros((n_half, E_PAD, LANES), F32)

    def cmp_and_select(n_chunk):
        rows = n_chunk * CMP_CHUNK
        nblk = rows // per_slc
        n_idx = lax.broadcasted_iota(jnp.int32, (rows, tq), 0)
        bias_c = jnp.where((n_idx * CMP_STRIDE + (CMP_BLOCK - 1)) <= t_row, 0.0, NEG)
        m4 = jnp.full((1, hg * tq), NEG, F32)
        for c in range(n_chunk):
            rs = slice(c * CMP_CHUNK, (c + 1) * CMP_CHUNK)
            st = _dot(kc_ref[0, rs, :], qp4)
            sc_ref[rs, :] = st
            m4 = jnp.maximum(m4, jnp.max(st + jnp.concatenate([bias_c[rs]] * hg, axis=1),
                                         axis=0, keepdims=True))
        imp = jnp.zeros((nblk, tq), F32)
        for h in range(hg):
            hs = slice(h * tq, (h + 1) * tq)
            e = jnp.exp2(sc_ref[0:rows, hs] + bias_c - m4[:, hs])
            for c in range(n_half):
                e_ref[c, E_PAD:E_PAD + rows, :] = e[:, c * LANES:(c + 1) * LANES]
            oc = _dot(vct_ref[0, :, 0:rows], e.astype(BF16))
            rl = jnp.where(any_c, 1.0 / jnp.maximum(oc[HEAD_DIM:HEAD_DIM + 1, :], 1e-30), 0.0)
            yacc_ref[h * HEAD_DIM:(h + 1) * HEAD_DIM, :] = oc[0:HEAD_DIM, :] * (rl * gt_ref[0, 3 * h:3 * h + 1, :])
            tap = lambda d: jnp.concatenate(
                [e_ref[c, pl.ds(E_PAD + d, nblk, stride=per_slc), :] for c in range(n_half)], axis=1)
            imp_h = 0.5 * (tap(-1) + tap(per_slc - 1))
            for d in range(per_slc - 1):
                imp_h = imp_h + tap(d)
            imp = imp + imp_h * rl

        blk = lax.broadcasted_iota(jnp.int32, (nblk, tq), 0)
        blk_f = blk.astype(F32)
        forced = (blk == 0) | (blk == cur) | (blk == cur - 1)
        v = jnp.where(blk > cur, NEG, jnp.where(forced, REMOVED, imp))
        for _ in range(N_SELECT - N_FORCED):
            mx = jnp.max(v, axis=0, keepdims=True)
            first = jnp.min(jnp.where(v == mx, blk_f, float(nblk)), axis=0, keepdims=True)
            v = jnp.where(blk_f == first, REMOVED, v)
        selb_ref[0:nblk, :] = jnp.where((v == REMOVED) & (blk <= cur), 0.0, NEG)

    n_chunks_total = kc_ref.shape[1] // CMP_CHUNK
    visible = (start + tq - CMP_BLOCK) // (CMP_STRIDE * CMP_CHUNK) + 1
    for n_chunk in range(1, n_chunks_total + 1):
        pl.when(visible == n_chunk)(functools.partial(cmp_and_select, n_chunk))

    def reset():
        acc_ref[...] = jnp.zeros_like(acc_ref)
        m_ref[...] = jnp.full_like(m_ref, NEG)

    def soft_pv(st_ref, vt, bias=None):
        ps, alphas = [], []
        for h in range(hg):
            hs = slice(h * tq, (h + 1) * tq)
            load = (lambda: st_ref[:, hs]) if bias is None else (lambda: st_ref[:, hs] + bias)
            m_old = m_ref[0:1, hs]
            m_new = jnp.maximum(m_old, jnp.max(load(), axis=0, keepdims=True))
            alphas.append(jnp.exp2(m_old - m_new))
            ps.append(jnp.exp2(load() - m_new).astype(BF16))
            m_ref[0:1, hs] = m_new
        alpha = jnp.concatenate(alphas, axis=1)
        acc_ref[...] = acc_ref[...] * alpha + _dot(vt, jnp.concatenate(ps, axis=1))

    def finish(branch):
        for h in range(hg):
            hs = slice(h * tq, (h + 1) * tq)
            o = acc_ref[0:HEAD_DIM, hs] * (1.0 / acc_ref[HEAD_DIM:HEAD_DIM + 1, hs])
            yacc_ref[h * HEAD_DIM:(h + 1) * HEAD_DIM, :] += o * gt_ref[0, 3 * h + branch:3 * h + branch + 1, :]

    k_iota = lax.broadcasted_iota(jnp.int32, (tq, 1), 0)

    def q_biased(grp):
        band = selb_ref[pl.ds(pl.multiple_of(grp * BIAS_ROWS, BIAS_ROWS), BIAS_ROWS), :].astype(BF16)
        return jnp.concatenate([qr4[0:HEAD_DIM], jnp.concatenate([band] * hg, axis=1),
                                qr4[HEAD_DIM + BIAS_ROWS:]], axis=0)

    def key_rows(ref, kt):
        return ref[0, pl.ds(pl.multiple_of(kt * tq, tq), tq), :]

    reset()
    sa_ref[...] = _dot(key_rows(kslc_ref, 0), q_biased(0))

    def slc_group(grp, carry):
        kt0 = grp * KT_GROUP
        q4 = q_biased(grp)
        for j in range(KT_GROUP):
            cur_ref, nxt_ref = (sa_ref, sb_ref) if j % 2 == 0 else (sb_ref, sa_ref)
            q_next = q4 if j + 1 < KT_GROUP else q_biased(grp + 1)
            nxt_ref[...] = _dot(key_rows(kslc_ref, kt0 + j + 1), q_next)
            soft_pv(cur_ref, vslct_ref[0, kt0 + j])
        return carry

    def slc_single(kt, carry):
        sb_ref[...] = _dot(key_rows(kslc_ref, kt + 1), q_biased(n_grp))
        soft_pv(sa_ref, vslct_ref[0, kt])
        sa_ref[...] = sb_ref[...]
        return carry

    n_grp = i // KT_GROUP
    lax.fori_loop(0, n_grp, slc_group, 0)
    lax.fori_loop(n_grp * KT_GROUP, i, slc_single, 0)

    def win_tile(back):
        kt = jnp.maximum(i - back, 0)
        kpos = kt * tq + k_iota
        ok = (kpos <= t_row) & (kpos > t_row - WINDOW) & (i >= back)
        return kt, jnp.where(ok, 0.0, NEG)

    kts, biases = zip(*[win_tile(back) for back in range(WIN_TILES, -1, -1)])
    sb_ref[...] = _dot(key_rows(kwin_ref, kts[0]), qr4)
    soft_pv(sa_ref, vslct_ref[0, i], jnp.where((start + k_iota) <= t_row, 0.0, NEG))
    finish(1)
    reset()
    bufs = (sb_ref, sa_ref)
    for w in range(WIN_TILES + 1):
        if w + 1 <= WIN_TILES:
            bufs[(w + 1) % 2][...] = _dot(key_rows(kwin_ref, kts[w + 1]), qr4)
        soft_pv(bufs[w % 2], vwint_ref[0, kts[w]], biases[w])
    finish(2)

    y_ref[0] = yacc_ref[...].T.astype(BF16)


def _nsa(qpt, qrt, gt, kc, vct, kslc, vslct, kwin, vwint):
    b, _, seq = qpt.shape
    tq = ATT_T
    hg = HEADS_PER_GROUP
    n_cmp = kc.shape[1]
    nkt = seq // tq
    qspec = pl.BlockSpec((1, hg * LANES, tq), lambda bi, g, i: (bi, g, i))
    in_specs = [
        qspec, qspec,
        pl.BlockSpec((1, LANES, tq), lambda bi, g, i: (bi, g, i)),
        pl.BlockSpec((1, n_cmp, LANES), lambda bi, g, i: (bi, 0, g)),
        pl.BlockSpec((1, LANES, n_cmp), lambda bi, g, i: (bi, g, 0)),
        _resident((1, seq, LANES), lambda bi, g, i: (bi, 0, g)),
        _resident((1, nkt, None, VT_ROWS, tq), lambda bi, g, i: (bi, 0, g, 0, 0)),
        _resident((1, seq, LANES), lambda bi, g, i: (bi, 0, g)),
        _resident((1, nkt, None, VT_ROWS, tq), lambda bi, g, i: (bi, 0, g, 0, 0)),
    ]
    return pl.pallas_call(
        _nsa_body,
        grid=(b, NSA_GROUPS, seq // tq),
        in_specs=in_specs,
        out_specs=pl.BlockSpec((1, tq, hg * HEAD_DIM), lambda bi, g, i: (bi, i, g)),
        out_shape=jax.ShapeDtypeStruct((b, seq, NSA_HEADS * HEAD_DIM), BF16),
        scratch_shapes=[pltpu.VMEM((VT_ROWS, hg * tq), F32),
                        pltpu.VMEM((8, hg * tq), F32),
                        pltpu.VMEM((seq // SLC_BLOCK, tq), F32),
                        pltpu.VMEM((tq, hg * tq), F32),
                        pltpu.VMEM((tq, hg * tq), F32),
                        pltpu.VMEM((tq // LANES, E_PAD + n_cmp, LANES), F32),
                        pltpu.VMEM((n_cmp, hg * tq), F32),
                        pltpu.VMEM((hg * HEAD_DIM, tq), F32)],
        compiler_params=_params(3),
        name="nsa",
    )(qpt, qrt, gt, kc, vct, kslc, vslct, kwin, vwint)


def _merge_body(x_ref, part_ref, gnsa_ref, ynsa_ref, wbn_ref, wo_ref, gpost_ref, o_ref):
    merged = part_ref[...] + gnsa_ref[...].astype(F32) * _dot(ynsa_ref[...], wbn_ref[...])
    y = _dot(merged.astype(BF16), wo_ref[...])
    o_ref[...] = x_ref[...] + _rms(y, gpost_ref[...])


def _merge(x, part, gnsa, ynsa, wbn, wo, gpost):
    t = x.shape[0]
    tm = MERGE_TM
    row = lambda i: (i, 0)
    fix = lambda i: (0, 0)
    return pl.pallas_call(
        _merge_body,
        grid=(t // tm,),
        in_specs=[pl.BlockSpec((tm, D_MODEL), row), pl.BlockSpec((tm, D_MODEL), row),
                  pl.BlockSpec((tm, D_MODEL), row), pl.BlockSpec((tm, NSA_HEADS * HEAD_DIM), row),
                  _resident(wbn.shape, fix), _resident(wo.shape, fix), _resident((1, D_MODEL), fix)],
        out_specs=pl.BlockSpec((tm, D_MODEL), row),
        out_shape=jax.ShapeDtypeStruct((t, D_MODEL), F32),
        compiler_params=_params(1),
        name="merge",
    )(x, part, gnsa, ynsa, wbn, wo, gpost)


def _pad_lanes(w, groups, width):
    d = w.shape[0]
    w = w.reshape(d, groups, width)
    return jnp.pad(w, ((0, 0), (0, 0), (0, LANES - width))).reshape(d, groups * LANES)


def _mix_weight(w):
    o = 0
    u = w[:, o:o + CONV_DIM]; o += CONV_DIM
    bg = w[:, o:o + CONV_DIM]; o += CONV_DIM
    cg = w[:, o:o + CONV_DIM]; o += CONV_DIM
    q = w[:, o:o + NSA_HEADS * HEAD_DIM]; o += NSA_HEADS * HEAD_DIM
    kv = w[:, o:o + 6 * NSA_GROUPS * HEAD_DIM].reshape(D_MODEL, 6, NSA_GROUPS * HEAD_DIM)
    o += 6 * NSA_GROUPS * HEAD_DIM
    gate = w[:, o:o + 3 * NSA_HEADS]; o += 3 * NSA_HEADS
    qmem = w[:, o:o + MEM_HEADS * MEM_HEAD_DIM]; o += MEM_HEADS * MEM_HEAD_DIM
    merge = w[:, o:]
    slab = lambda c: _pad_lanes(kv[:, c], NSA_GROUPS, HEAD_DIM)
    cols = [u, bg, cg, _pad_lanes(q, NSA_HEADS, HEAD_DIM),
            slab(2), slab(4), slab(3), slab(5), kv[:, 0], kv[:, 1],
            _pad_lanes(gate, NSA_GROUPS, 3 * HEADS_PER_GROUP), qmem, merge]
    out = jnp.concatenate(cols, axis=1).astype(BF16)
    assert out.shape == (D_MODEL, C_TOTAL)
    return out


def _cmp_weights(pos_emb, w1, b1, w2):
    eye = jnp.eye(NSA_GROUPS, dtype=F32)
    w1r = w1.reshape(CMP_BLOCK, HEAD_DIM, CMP_HIDDEN)
    width = CMP_STRIDE * NSA_GROUPS * HEAD_DIM

    def spread(a):
        return jnp.einsum('pdj,gh->pgdhj', a, eye).reshape(width, NSA_GROUPS * CMP_HIDDEN).astype(BF16)

    def pe_row(p):
        return jnp.broadcast_to(p[:, None, :], (CMP_STRIDE, NSA_GROUPS, HEAD_DIM)).reshape(1, width)

    pe = jnp.concatenate([pe_row(pos_emb[:CMP_STRIDE]), pe_row(pos_emb[CMP_STRIDE:]),
                          jnp.zeros((6, width), F32)], axis=0)
    w2p = jnp.pad(w2, ((0, 0), (0, LANES - HEAD_DIM)))
    w2b = jnp.einsum('jc,gh->gjhc', w2p, eye).reshape(NSA_GROUPS * CMP_HIDDEN, NSA_GROUPS * LANES)
    return (pe, spread(w1r[:CMP_STRIDE]), spread(w1r[CMP_STRIDE:]),
            jnp.tile(b1, NSA_GROUPS)[None, :], w2b.astype(BF16))


def _rope_table():
    inv_freq = ROPE_THETA ** (-jnp.arange(0, ROT_DIM, 2, dtype=jnp.float32) / ROT_DIM)
    half = ROT_DIM // 2
    lane = np.arange(LANES)
    tab = jnp.zeros((8, LANES), F32)
    tab = tab.at[0, :ROT_DIM].set(jnp.tile(inv_freq, 2))
    tab = tab.at[1].set(jnp.asarray(np.where(lane < half, -1.0, 0.0), F32))
    tab = tab.at[2].set(jnp.asarray(np.where((lane >= half) & (lane < ROT_DIM), 1.0, 0.0), F32))
    tab = tab.at[3].set(jnp.asarray(np.where(lane == HEAD_DIM, 1.0, 0.0), F32))
    return tab


def kernel(x, mem, positions, ffn1_norm_pre, ffn1_norm_post, ffn1_w_in, ffn1_w_out, mix_norm_pre, mix_norm_post, mem_norm, w_mix_in, conv_w, cmp_pos_k, cmp_pos_v, cmp_k_w1, cmp_k_b1, cmp_k_w2, cmp_v_w1, cmp_v_b1, cmp_v_w2, w_mem_kv, w_branch_conv, w_branch_nsa, w_branch_mem, w_mix_out, ffn2_norm_pre, ffn2_norm_post, ffn2_w_in, ffn2_w_out):
    batch, seq, _ = x.shape
    t = batch * seq
    n_chunks = seq // CMP_STRIDE
    xf = x.reshape(t, D_MODEL)
    pos = positions.reshape(t, 1)
    tab = _rope_table()
    one = jnp.tile(tab[3:4, :], (1, NSA_GROUPS))
    row = lambda g: g[None, :]

    for l in range(DEPTH):
        xf = _ffn(xf, row(ffn1_norm_pre[l]), row(ffn1_norm_post[l]),
                  ffn1_w_in[l][:, :D_FF].astype(BF16), ffn1_w_in[l][:, D_FF:].astype(BF16),
                  ffn1_w_out[l].astype(BF16))

        mk, mv = _memkv(mem, row(mem_norm[l]), w_mem_kv[l].astype(BF16))
        convw = jnp.pad(conv_w[l], ((0, 8 - CONV_WIDTH), (0, 0)))
        (part, gnsa, qpt, qrt, kslc, kwin, vslct, vwint, kcr, vcr, gt) = _mixin(
            xf, pos, row(mix_norm_pre[l]), tab, _mix_weight(w_mix_in[l]), convw, mk, mv,
            w_branch_conv[l].astype(BF16), w_branch_mem[l].astype(BF16), batch, seq)

        width = CMP_STRIDE * NSA_GROUPS * HEAD_DIM
        kc = _compress(kcr.reshape(batch, n_chunks, width),
                       *_cmp_weights(cmp_pos_k[l], cmp_k_w1[l], cmp_k_b1[l], cmp_k_w2[l]), one, False)
        vct = _compress(vcr.reshape(batch, n_chunks, width),
                        *_cmp_weights(cmp_pos_v[l], cmp_v_w1[l], cmp_v_b1[l], cmp_v_w2[l]), one, True)

        ynsa = _nsa(qpt, qrt, gt, kc, vct,
                    kslc.reshape(batch, seq, NSA_GROUPS * LANES), vslct,
                    kwin.reshape(batch, seq, NSA_GROUPS * LANES), vwint)

        xf = _merge(xf, part, gnsa, ynsa.reshape(t, NSA_HEADS * HEAD_DIM),
                    w_branch_nsa[l].astype(BF16), w_mix_out[l].astype(BF16), row(mix_norm_post[l]))

        xf = _ffn(xf, row(ffn2_norm_pre[l]), row(ffn2_norm_post[l]),
                  ffn2_w_in[l][:, :D_FF].astype(BF16), ffn2_w_in[l][:, D_FF:].astype(BF16),
                  ffn2_w_out[l].astype(BF16))
    return xf.reshape(batch, seq, D_MODEL)
```

```python
import functools

import jax
import jax.numpy as jnp
import numpy as np
from jax import lax
from jax.experimental import pallas as pl
from jax.experimental.pallas import tpu as pltpu

D_MODEL = 1024
DEPTH = 2
CONV_DIM = 512
CONV_WIDTH = 3
NSA_HEADS = 8
NSA_GROUPS = 2
HEADS_PER_GROUP = NSA_HEADS // NSA_GROUPS
HEAD_DIM = 64
ROT_DIM = HEAD_DIM // 4
ROPE_THETA = 500000.0
CMP_BLOCK = 32
CMP_STRIDE = 16
CMP_HIDDEN = 256
SLC_BLOCK = 64
N_SELECT = 16
WINDOW = 512
MEM_HEADS = 4
MEM_HEAD_DIM = 128
D_FF = 2816
EPS = 1e-6
NEG = -1e30
REMOVED = -3e38

LANES = 128
F32 = jnp.float32
BF16 = jnp.bfloat16

FFN_TM = 512
MIX_TM = 512
MERGE_TM = 512
HALO = 8
ATT_T = 256
WIN_TILES = WINDOW // ATT_T
KT_GROUP = 4
BIAS_ROWS = KT_GROUP * ATT_T // SLC_BLOCK
VT_ROWS = HEAD_DIM + BIAS_ROWS
LOG2_E = 1.4426950408889634
SAFE_EXP2 = 64.0
CMP_CHUNK = 256
E_PAD = 8
N_FORCED = 3

C_U = 0
C_BG = C_U + CONV_DIM
C_CG = C_BG + CONV_DIM
C_Q = C_CG + CONV_DIM
C_KSLC = C_Q + NSA_HEADS * LANES
C_KWIN = C_KSLC + NSA_GROUPS * LANES
C_VSLC = C_KWIN + NSA_GROUPS * LANES
C_VWIN = C_VSLC + NSA_GROUPS * LANES
C_KCR = C_VWIN + NSA_GROUPS * LANES
C_VCR = C_KCR + LANES
C_GATE = C_VCR + LANES
C_QMEM = C_GATE + NSA_GROUPS * LANES
C_MERGE = C_QMEM + MEM_HEADS * MEM_HEAD_DIM
C_TOTAL = C_MERGE + 3 * D_MODEL

VMEM_LIMIT = 56 * 1024 * 1024


def _params(n_axes):
    return pltpu.CompilerParams(dimension_semantics=("arbitrary",) * n_axes,
                                vmem_limit_bytes=VMEM_LIMIT)


def _resident(shape, index_map):
    return pl.BlockSpec(shape, index_map, pipeline_mode=pl.Buffered(1))


def _rms(x, g):
    return x * lax.rsqrt(jnp.mean(x * x, axis=-1, keepdims=True) + EPS) * g


def _dot(a, b):
    return jnp.dot(a, b, preferred_element_type=F32)


def _ffn_body(x_ref, gpre_ref, gpost_ref, wa_ref, wb_ref, wo_ref, o_ref):
    x = x_ref[...]
    h = _rms(x, gpre_ref[...]).astype(BF16)
    a = _dot(h, wa_ref[...])
    b = _dot(h, wb_ref[...])
    g = (a * jax.nn.sigmoid(a) * b).astype(BF16)
    y = _dot(g, wo_ref[...])
    o_ref[...] = x + 0.5 * _rms(y, gpost_ref[...])


def _ffn(x, gpre, gpost, wa, wb, wo):
    t = x.shape[0]
    row = lambda i: (i, 0)
    fix = lambda i: (0, 0)
    return pl.pallas_call(
        _ffn_body,
        grid=(t // FFN_TM,),
        in_specs=[pl.BlockSpec((FFN_TM, D_MODEL), row),
                  _resident((1, D_MODEL), fix), _resident((1, D_MODEL), fix),
                  _resident((D_MODEL, D_FF), fix), _resident((D_MODEL, D_FF), fix),
                  _resident((D_FF, D_MODEL), fix)],
        out_specs=pl.BlockSpec((FFN_TM, D_MODEL), row),
        out_shape=jax.ShapeDtypeStruct((t, D_MODEL), F32),
        compiler_params=_params(1),
        name="ffn",
    )(x, gpre, gpost, wa, wb, wo)


def _memkv_body(mem_ref, g_ref, w_ref, mk_ref, mv_ref):
    h = _rms(mem_ref[0], g_ref[...]).astype(BF16)
    kv = _dot(h, w_ref[...])
    half = MEM_HEADS * MEM_HEAD_DIM
    mk_ref[0] = kv[:, :half].astype(BF16)
    mv_ref[0] = kv[:, half:].astype(BF16)


def _memkv(mem, g, w):
    b, m, _ = mem.shape
    half = MEM_HEADS * MEM_HEAD_DIM
    out = jax.ShapeDtypeStruct((b, m, half), BF16)
    return pl.pallas_call(
        _memkv_body,
        grid=(b,),
        in_specs=[pl.BlockSpec((1, m, D_MODEL), lambda i: (i, 0, 0)),
                  pl.BlockSpec((1, D_MODEL), lambda i: (0, 0)),
                  pl.BlockSpec((D_MODEL, 2 * half), lambda i: (0, 0))],
        out_specs=[pl.BlockSpec((1, m, half), lambda i: (i, 0, 0))] * 2,
        out_shape=[out, out],
        compiler_params=_params(1),
        name="memkv",
    )(mem, g, w)


def _mix_body(tiles_per_batch,
              x_ref, halo_ref, pos_ref, gpre_ref, tab_ref, w_ref, convw_ref, mk_ref, mv_ref,
              wbc_ref, wbm_ref,
              part_ref, gnsa_ref, qpt_ref, qrt_ref, kslc_ref, kwin_ref, vslct_ref, vwint_ref,
              kcr_ref, vcr_ref, gatest_ref,
              cu_scr):
    tm = MIX_TM
    i = pl.program_id(0)
    gpre = gpre_ref[...]
    h = _rms(x_ref[...], gpre).astype(BF16)
    hh = _rms(halo_ref[...], gpre).astype(BF16)

    def proj(hv, c0, width):
        return _dot(hv, w_ref[:, c0:c0 + width])

    cu = proj(h, C_CG, CONV_DIM) * proj(h, C_U, CONV_DIM)
    cu_h = proj(hh, C_CG, CONV_DIM) * proj(hh, C_U, CONV_DIM)
    first = (i % tiles_per_batch) == 0
    cu_scr[0:HALO, :] = jnp.where(first, 0.0, cu_h)
    cu_scr[HALO:HALO + tm, :] = cu
    cw = convw_ref[...]
    conv = (cu * cw[2:3, :]
            + cu_scr[HALO - 1:HALO - 1 + tm, :] * cw[1:2, :]
            + cu_scr[HALO - 2:HALO - 2 + tm, :] * cw[0:1, :])
    y_conv = (proj(h, C_BG, CONV_DIM) * conv).astype(BF16)

    ang = pos_ref[...].astype(F32) * tab_ref[0:1, :]
    cos_t = jnp.cos(ang)
    sin_t = jnp.sin(ang)
    sin_a = sin_t * tab_ref[1:2, :]
    sin_b = sin_t * tab_ref[2:3, :]

    def rope(t):
        half = ROT_DIM // 2
        return (t * cos_t + pltpu.roll(t, LANES - half, 1) * sin_a
                + pltpu.roll(t, half, 1) * sin_b)

    wide_starts = (C_Q, C_KSLC, C_QMEM, C_MERGE)
    wide = [proj(h, a, b - a) for a, b in zip(wide_starts[:-1], wide_starts[1:])]

    def nsa_cols(c0):
        k = max(j for j, a in enumerate(wide_starts[:-1]) if a <= c0)
        return wide[k][:, c0 - wide_starts[k]:c0 - wide_starts[k] + LANES]

    scale = HEAD_DIM ** -0.5 * LOG2_E
    for hd in range(NSA_HEADS):
        q = nsa_cols(C_Q + hd * LANES) * scale
        sl = slice(hd * LANES, (hd + 1) * LANES)
        for j in range(tm // ATT_T):
            rows = slice(j * ATT_T, (j + 1) * ATT_T)
            qpt_ref[0, sl, rows] = q[rows].T.astype(BF16)
            qrt_ref[0, sl, rows] = rope(q)[rows].T.astype(BF16)
    one_hot = tab_ref[3:4, :]
    seq_row = (i % tiles_per_batch) * tm + lax.broadcasted_iota(jnp.int32, (tm, LANES), 0)
    lane = lax.broadcasted_iota(jnp.int32, (tm, LANES), 1)
    blk_hot = jnp.where(lane - HEAD_DIM == (seq_row // SLC_BLOCK) % BIAS_ROWS, 1.0, 0.0)
    for g in range(NSA_GROUPS):
        sl = slice(g * LANES, (g + 1) * LANES)
        kslc_ref[:, sl] = (rope(nsa_cols(C_KSLC + g * LANES)) + blk_hot).astype(BF16)
        kwin_ref[:, sl] = rope(nsa_cols(C_KWIN + g * LANES)).astype(BF16)
        vs = nsa_cols(C_VSLC + g * LANES) + one_hot
        vw = nsa_cols(C_VWIN + g * LANES) + one_hot
        gt = jax.nn.sigmoid(nsa_cols(C_GATE + g * LANES))
        for j in range(tm // ATT_T):
            rows = slice(j * ATT_T, (j + 1) * ATT_T)
            vslct_ref[0, j, g] = vs[rows].T[0:VT_ROWS].astype(BF16)
            vwint_ref[0, j, g] = vw[rows].T[0:VT_ROWS].astype(BF16)
            gatest_ref[0, sl, rows] = gt[rows].T
    kcr_ref[...] = nsa_cols(C_KCR)
    vcr_ref[...] = nsa_cols(C_VCR)

    mem_scale = MEM_HEAD_DIM ** -0.5
    outs = []
    for hd in range(MEM_HEADS):
        sl = slice(hd * MEM_HEAD_DIM, (hd + 1) * MEM_HEAD_DIM)
        qm = (nsa_cols(C_QMEM + hd * MEM_HEAD_DIM) * mem_scale).astype(BF16)
        s = lax.dot_general(qm, mk_ref[0, :, sl], (((1,), (1,)), ((), ())),
                            preferred_element_type=F32)
        p = jnp.exp(s - jnp.max(s, axis=-1, keepdims=True))
        l = jnp.sum(p, axis=-1, keepdims=True)
        outs.append((_dot(p.astype(BF16), mv_ref[0, :, sl]) / l).astype(BF16))
    y_mem = jnp.concatenate(outs, axis=1)

    g_conv = jax.nn.sigmoid(proj(h, C_MERGE, D_MODEL))
    g_nsa = jax.nn.sigmoid(proj(h, C_MERGE + D_MODEL, D_MODEL))
    g_mem = jax.nn.sigmoid(proj(h, C_MERGE + 2 * D_MODEL, D_MODEL))
    part_ref[...] = g_conv * _dot(y_conv, wbc_ref[...]) + g_mem * _dot(y_mem, wbm_ref[...])
    gnsa_ref[...] = g_nsa.astype(BF16)


def _mixin(x, pos, gpre, tab, w_main, convw, mk, mv, wbc, wbm, batch, seq):
    t = batch * seq
    tm = MIX_TM
    tpb = seq // tm
    nt = tm // ATT_T
    row = lambda i: (i, 0)
    fix = lambda i: (0, 0)
    bt = lambda i: (i // tpb, 0, i % tpb)
    mem_len = mk.shape[1]
    half = MEM_HEADS * MEM_HEAD_DIM
    out_shape = [
        jax.ShapeDtypeStruct((t, D_MODEL), F32),
        jax.ShapeDtypeStruct((t, D_MODEL), BF16),
        jax.ShapeDtypeStruct((batch, NSA_HEADS * LANES, seq), BF16),
        jax.ShapeDtypeStruct((batch, NSA_HEADS * LANES, seq), BF16),
        jax.ShapeDtypeStruct((t, NSA_GROUPS * LANES), BF16),
        jax.ShapeDtypeStruct((t, NSA_GROUPS * LANES), BF16),
        jax.ShapeDtypeStruct((batch, seq // ATT_T, NSA_GROUPS, VT_ROWS, ATT_T), BF16),
        jax.ShapeDtypeStruct((batch, seq // ATT_T, NSA_GROUPS, VT_ROWS, ATT_T), BF16),
        jax.ShapeDtypeStruct((t, LANES), F32),
        jax.ShapeDtypeStruct((t, LANES), F32),
        jax.ShapeDtypeStruct((batch, NSA_GROUPS * LANES, seq), F32),
    ]
    vt_spec = pl.BlockSpec((1, nt, NSA_GROUPS, VT_ROWS, ATT_T), lambda i: (i // tpb, i % tpb, 0, 0, 0))
    out_specs = [
        pl.BlockSpec((tm, D_MODEL), row),
        pl.BlockSpec((tm, D_MODEL), row),
        pl.BlockSpec((1, NSA_HEADS * LANES, tm), bt),
        pl.BlockSpec((1, NSA_HEADS * LANES, tm), bt),
        pl.BlockSpec((tm, NSA_GROUPS * LANES), row),
        pl.BlockSpec((tm, NSA_GROUPS * LANES), row),
        vt_spec, vt_spec,
        pl.BlockSpec((tm, LANES), row),
        pl.BlockSpec((tm, LANES), row),
        pl.BlockSpec((1, NSA_GROUPS * LANES, tm), bt),
    ]
    in_specs = [
        pl.BlockSpec((tm, D_MODEL), row),
        pl.BlockSpec((HALO, D_MODEL), lambda i: (jnp.maximum(i * (tm // HALO) - 1, 0), 0)),
        pl.BlockSpec((tm, 1), row),
        _resident((1, D_MODEL), fix),
        _resident((8, LANES), fix),
        _resident((D_MODEL, C_TOTAL), fix),
        _resident((8, CONV_DIM), fix),
        pl.BlockSpec((1, mem_len, half), lambda i: (i // tpb, 0, 0)),
        pl.BlockSpec((1, mem_len, half), lambda i: (i // tpb, 0, 0)),
        _resident((CONV_DIM, D_MODEL), fix),
        _resident((half, D_MODEL), fix),
    ]
    return pl.pallas_call(
        functools.partial(_mix_body, tpb),
        grid=(t // tm,),
        in_specs=in_specs,
        out_specs=out_specs,
        out_shape=out_shape,
        scratch_shapes=[pltpu.VMEM((HALO + tm, CONV_DIM), F32)],
        compiler_params=_params(1),
        name="mixin",
    )(x, x, pos, gpre, tab, w_main, convw, mk, mv, wbc, wbm)


def _cmp_body(transpose_out, r_ref, pe_ref, w1a_ref, w1b_ref, b1_ref, w2_ref, one_ref, o_ref):
    r = r_ref[0]
    n = r.shape[0]
    a = _dot((r + pe_ref[0:1, :]).astype(BF16), w1a_ref[...])
    b = _dot((r + pe_ref[1:2, :]).astype(BF16), w1b_ref[...])
    hid = a + pltpu.roll(b, n - 1, 0) + b1_ref[...]
    o = _dot(jax.nn.gelu(hid).astype(BF16), w2_ref[...])
    if transpose_out:
        o = o + one_ref[...]
        o_ref[0] = jnp.concatenate(
            [o[:, g * LANES:(g + 1) * LANES].T for g in range(NSA_GROUPS)], axis=0).astype(BF16)
    else:
        o_ref[0] = o.astype(BF16)


def _compress(r, pe, w1a, w1b, b1, w2, one, transpose_out):
    b, n, width = r.shape
    fix = lambda i: (0, 0)
    oshape = (b, NSA_GROUPS * LANES, n) if transpose_out else (b, n, NSA_GROUPS * LANES)
    return pl.pallas_call(
        functools.partial(_cmp_body, transpose_out),
        grid=(b,),
        in_specs=[pl.BlockSpec((1, n, width), lambda i: (i, 0, 0)),
                  pl.BlockSpec((8, width), fix),
                  pl.BlockSpec(w1a.shape, fix), pl.BlockSpec(w1b.shape, fix),
                  pl.BlockSpec(b1.shape, fix), pl.BlockSpec(w2.shape, fix),
                  pl.BlockSpec(one.shape, fix)],
        out_specs=pl.BlockSpec((1,) + oshape[1:], lambda i: (i, 0, 0)),
        out_shape=jax.ShapeDtypeStruct(oshape, BF16),
        compiler_params=_params(1),
        name="compress_v" if transpose_out else "compress_k",
    )(r, pe, w1a, w1b, b1, w2, one)


def _nsa_body(qpt_ref, qrt_ref, gt_ref, kc_ref, vct_ref,
              kslc_ref, vslct_ref, kwin_ref, vwint_ref,
              y_ref,
              acc_ref, m_ref, selb_ref, sa_ref, sb_ref, e_ref, sc_ref, yacc_ref):
    tq = ATT_T
    hg = HEADS_PER_GROUP
    i = pl.program_id(2)
    start = i * tq
    t_row = start + lax.broadcasted_iota(jnp.int32, (1, tq), 1)

    def lane_cat(ref):
        return jnp.concatenate([ref[0, h * LANES:(h + 1) * LANES, :] for h in range(hg)], axis=1)

    qp4 = lane_cat(qpt_ref)
    qr4 = lane_cat(qrt_ref)

    any_c = t_row >= CMP_BLOCK - 1
    per_slc = SLC_BLOCK // CMP_STRIDE
    n_half = tq // LANES
    cur = t_row // SLC_BLOCK
    e_ref[:, 0:E_PAD, :] = jnp.zeros((n_half, E_PAD, LANES), F32)

    def cmp_and_select(n_chunk):
        rows = n_chunk * CMP_CHUNK
        nblk = rows // per_slc
        n_idx = lax.broadcasted_iota(jnp.int32, (rows, tq), 0)
        bias_c = jnp.where((n_idx * CMP_STRIDE + (CMP_BLOCK - 1)) <= t_row, 0.0, NEG)
        m4 = jnp.full((1, hg * tq), NEG, F32)
        for c in range(n_chunk):
            rs = slice(c * CMP_CHUNK, (c + 1) * CMP_CHUNK)
            st = _dot(kc_ref[0, rs, :], qp4)
            sc_ref[rs, :] = st
            m4 = jnp.maximum(m4, jnp.max(st + jnp.concatenate([bias_c[rs]] * hg, axis=1),
                                         axis=0, keepdims=True))
        imp = jnp.zeros((nblk, tq), F32)
        for h in range(hg):
            hs = slice(h * tq, (h + 1) * tq)
            e = jnp.exp2(sc_ref[0:rows, hs] + bias_c - m4[:, hs])
            for c in range(n_half):
                e_ref[c, E_PAD:E_PAD + rows, :] = e[:, c * LANES:(c + 1) * LANES]
            oc = _dot(vct_ref[0, :, 0:rows], e.astype(BF16))
            rl = jnp.where(any_c, 1.0 / jnp.maximum(oc[HEAD_DIM:HEAD_DIM + 1, :], 1e-30), 0.0)
            yacc_ref[h * HEAD_DIM:(h + 1) * HEAD_DIM, :] = oc[0:HEAD_DIM, :] * (rl * gt_ref[0, 3 * h:3 * h + 1, :])
            tap = lambda d: jnp.concatenate(
                [e_ref[c, pl.ds(E_PAD + d, nblk, stride=per_slc), :] for c in range(n_half)], axis=1)
            imp_h = 0.5 * (tap(-1) + tap(per_slc - 1))
            for d in range(per_slc - 1):
                imp_h = imp_h + tap(d)
            imp = imp + imp_h * rl

        blk = lax.broadcasted_iota(jnp.int32, (nblk, tq), 0)
        blk_f = blk.astype(F32)
        forced = (blk == 0) | (blk == cur) | (blk == cur - 1)
        v = jnp.where(blk > cur, NEG, jnp.where(forced, REMOVED, imp))
        for _ in range(N_SELECT - N_FORCED):
            mx = jnp.max(v, axis=0, keepdims=True)
            first = jnp.min(jnp.where(v == mx, blk_f, float(nblk)), axis=0, keepdims=True)
            v = jnp.where(blk_f == first, REMOVED, v)
        selb_ref[0:nblk, :] = jnp.where((v == REMOVED) & (blk <= cur), 0.0, NEG)

    n_chunks_total = kc_ref.shape[1] // CMP_CHUNK
    visible = (start + tq - CMP_BLOCK) // (CMP_STRIDE * CMP_CHUNK) + 1
    for n_chunk in range(1, n_chunks_total + 1):
        pl.when(visible == n_chunk)(functools.partial(cmp_and_select, n_chunk))

    def reset():
        acc_ref[...] = jnp.zeros_like(acc_ref)
        m_ref[...] = jnp.full_like(m_ref, NEG)

    def soft_pv(st_ref, vt, bias=None):
        ps, alphas = [], []
        for h in range(hg):
            hs = slice(h * tq, (h + 1) * tq)
            load = (lambda: st_ref[:, hs]) if bias is None else (lambda: st_ref[:, hs] + bias)
            m_old = m_ref[0:1, hs]
            m_new = jnp.maximum(m_old, jnp.max(load(), axis=0, keepdims=True))
            alphas.append(jnp.exp2(m_old - m_new))
            ps.append(jnp.exp2(load() - m_new).astype(BF16))
            m_ref[0:1, hs] = m_new
        alpha = jnp.concatenate(alphas, axis=1)
        acc_ref[...] = acc_ref[...] * alpha + _dot(vt, jnp.concatenate(ps, axis=1))

    def finish(branch):
        for h in range(hg):
            hs = slice(h * tq, (h + 1) * tq)
            o = acc_ref[0:HEAD_DIM, hs] * (1.0 / acc_ref[HEAD_DIM:HEAD_DIM + 1, hs])
            yacc_ref[h * HEAD_DIM:(h + 1) * HEAD_DIM, :] += o * gt_ref[0, 3 * h + branch:3 * h + branch + 1, :]

    k_iota = lax.broadcasted_iota(jnp.int32, (tq, 1), 0)

    def q_biased(grp):
        band = selb_ref[pl.ds(pl.multiple_of(grp * BIAS_ROWS, BIAS_ROWS), BIAS_ROWS), :].astype(BF16)
        return jnp.concatenate([qr4[0:HEAD_DIM], jnp.concatenate([band] * hg, axis=1),
                                qr4[HEAD_DIM + BIAS_ROWS:]], axis=0)

    def key_rows(ref, kt):
        return ref[0, pl.ds(pl.multiple_of(kt * tq, tq), tq), :]

    reset()
    sa_ref[...] = _dot(key_rows(kslc_ref, 0), q_biased(0))

    @pl.when(i > 0)
    def _():
        m_ref[0:1, :] = jnp.max(sa_ref[...], axis=0, keepdims=True)

    def slc_group(grp, carry):
        kt0 = grp * KT_GROUP
        q4 = q_biased(grp)
        m_used = m_ref[0:1, :]
        gmax = [m_used[:, h * tq:(h + 1) * tq] for h in range(hg)]
        part = [jnp.zeros((VT_ROWS, tq), F32) for _ in range(hg)]
        for j in range(KT_GROUP):
            cur_ref, nxt_ref = (sa_ref, sb_ref) if j % 2 == 0 else (sb_ref, sa_ref)
            q_next = q4 if j + 1 < KT_GROUP else q_biased(grp + 1)
            nxt_ref[...] = _dot(key_rows(kslc_ref, kt0 + j + 1), q_next)
            vt = vslct_ref[0, kt0 + j]
            for h in range(hg):
                hs = slice(h * tq, (h + 1) * tq)
                s = cur_ref[:, hs]
                gmax[h] = jnp.maximum(gmax[h], jnp.max(s, axis=0, keepdims=True))
                part[h] = part[h] + _dot(vt, jnp.exp2(s - m_used[:, hs]).astype(BF16))
        safe = jnp.max(jnp.concatenate(gmax, axis=1) - m_used) <= SAFE_EXP2

        @pl.when(safe)
        def _():
            acc_ref[...] += jnp.concatenate(part, axis=1)

        @pl.when(jnp.logical_not(safe))
        def _():
            for j in range(KT_GROUP):
                sb_ref[...] = _dot(key_rows(kslc_ref, kt0 + j), q4)
                soft_pv(sb_ref, vslct_ref[0, kt0 + j])
        return carry

    def slc_single(kt, carry):
        sb_ref[...] = _dot(key_rows(kslc_ref, kt + 1), q_biased(n_grp))
        soft_pv(sa_ref, vslct_ref[0, kt])
        sa_ref[...] = sb_ref[...]
        return carry

    n_grp = i // KT_GROUP
    lax.fori_loop(0, n_grp, slc_group, 0)
    lax.fori_loop(n_grp * KT_GROUP, i, slc_single, 0)

    def win_tile(back):
        kt = jnp.maximum(i - back, 0)
        kpos = kt * tq + k_iota
        ok = (kpos <= t_row) & (kpos > t_row - WINDOW) & (i >= back)
        return kt, jnp.where(ok, 0.0, NEG)

    kts, biases = zip(*[win_tile(back) for back in range(WIN_TILES, -1, -1)])
    sb_ref[...] = _dot(key_rows(kwin_ref, kts[0]), qr4)
    soft_pv(sa_ref, vslct_ref[0, i], jnp.where((start + k_iota) <= t_row, 0.0, NEG))
    finish(1)
    reset()
    bufs = (sb_ref, sa_ref)
    for w in range(WIN_TILES + 1):
        if w + 1 <= WIN_TILES:
            bufs[(w + 1) % 2][...] = _dot(key_rows(kwin_ref, kts[w + 1]), qr4)
        soft_pv(bufs[w % 2], vwint_ref[0, kts[w]], biases[w])
    finish(2)

    y_ref[0] = yacc_ref[...].T.astype(BF16)


def _nsa(qpt, qrt, gt, kc, vct, kslc, vslct, kwin, vwint):
    b, _, seq = qpt.shape
    tq = ATT_T
    hg = HEADS_PER_GROUP
    n_cmp = kc.shape[1]
    nkt = seq // tq
    qspec = pl.BlockSpec((1, hg * LANES, tq), lambda bi, g, i: (bi, g, i))
    in_specs = [
        qspec, qspec,
        pl.BlockSpec((1, LANES, tq), lambda bi, g, i: (bi, g, i)),
        pl.BlockSpec((1, n_cmp, LANES), lambda bi, g, i: (bi, 0, g)),
        pl.BlockSpec((1, LANES, n_cmp), lambda bi, g, i: (bi, g, 0)),
        _resident((1, seq, LANES), lambda bi, g, i: (bi, 0, g)),
        _resident((1, nkt, None, VT_ROWS, tq), lambda bi, g, i: (bi, 0, g, 0, 0)),
        _resident((1, seq, LANES), lambda bi, g, i: (bi, 0, g)),
        _resident((1, nkt, None, VT_ROWS, tq), lambda bi, g, i: (bi, 0, g, 0, 0)),
    ]
    return pl.pallas_call(
        _nsa_body,
        grid=(b, NSA_GROUPS, seq // tq),
        in_specs=in_specs,
        out_specs=pl.BlockSpec((1, tq, hg * HEAD_DIM), lambda bi, g, i: (bi, i, g)),
        out_shape=jax.ShapeDtypeStruct((b, seq, NSA_HEADS * HEAD_DIM), BF16),
        scratch_shapes=[pltpu.VMEM((VT_ROWS, hg * tq), F32),
                        pltpu.VMEM((8, hg * tq), F32),
                        pltpu.VMEM((seq // SLC_BLOCK, tq), F32),
                        pltpu.VMEM((tq, hg * tq), F32),
                        pltpu.VMEM((tq, hg * tq), F32),
                        pltpu.VMEM((tq // LANES, E_PAD + n_cmp, LANES), F32),
                        pltpu.VMEM((n_cmp, hg * tq), F32),
                        pltpu.VMEM((hg * HEAD_DIM, tq), F32)],
        compiler_params=_params(3),
        name="nsa",
    )(qpt, qrt, gt, kc, vct, kslc, vslct, kwin, vwint)


def _merge_body(x_ref, part_ref, gnsa_ref, ynsa_ref, wbn_ref, wo_ref, gpost_ref, o_ref):
    merged = part_ref[...] + gnsa_ref[...].astype(F32) * _dot(ynsa_ref[...], wbn_ref[...])
    y = _dot(merged.astype(BF16), wo_ref[...])
    o_ref[...] = x_ref[...] + _rms(y, gpost_ref[...])


def _merge(x, part, gnsa, ynsa, wbn, wo, gpost):
    t = x.shape[0]
    tm = MERGE_TM
    row = lambda i: (i, 0)
    fix = lambda i: (0, 0)
    return pl.pallas_call(
        _merge_body,
        grid=(t // tm,),
        in_specs=[pl.BlockSpec((tm, D_MODEL), row), pl.BlockSpec((tm, D_MODEL), row),
                  pl.BlockSpec((tm, D_MODEL), row), pl.BlockSpec((tm, NSA_HEADS * HEAD_DIM), row),
                  _resident(wbn.shape, fix), _resident(wo.shape, fix), _resident((1, D_MODEL), fix)],
        out_specs=pl.BlockSpec((tm, D_MODEL), row),
        out_shape=jax.ShapeDtypeStruct((t, D_MODEL), F32),
        compiler_params=_params(1),
        name="merge",
    )(x, part, gnsa, ynsa, wbn, wo, gpost)


def _pad_lanes(w, groups, width):
    d = w.shape[0]
    w = w.reshape(d, groups, width)
    return jnp.pad(w, ((0, 0), (0, 0), (0, LANES - width))).reshape(d, groups * LANES)


def _mix_weight(w):
    o = 0
    u = w[:, o:o + CONV_DIM]; o += CONV_DIM
    bg = w[:, o:o + CONV_DIM]; o += CONV_DIM
    cg = w[:, o:o + CONV_DIM]; o += CONV_DIM
    q = w[:, o:o + NSA_HEADS * HEAD_DIM]; o += NSA_HEADS * HEAD_DIM
    kv = w[:, o:o + 6 * NSA_GROUPS * HEAD_DIM].reshape(D_MODEL, 6, NSA_GROUPS * HEAD_DIM)
    o += 6 * NSA_GROUPS * HEAD_DIM
    gate = w[:, o:o + 3 * NSA_HEADS]; o += 3 * NSA_HEADS
    qmem = w[:, o:o + MEM_HEADS * MEM_HEAD_DIM]; o += MEM_HEADS * MEM_HEAD_DIM
    merge = w[:, o:]
    slab = lambda c: _pad_lanes(kv[:, c], NSA_GROUPS, HEAD_DIM)
    cols = [u, bg, cg, _pad_lanes(q, NSA_HEADS, HEAD_DIM),
            slab(2), slab(4), slab(3), slab(5), kv[:, 0], kv[:, 1],
            _pad_lanes(gate, NSA_GROUPS, 3 * HEADS_PER_GROUP), qmem, merge]
    out = jnp.concatenate(cols, axis=1).astype(BF16)
    assert out.shape == (D_MODEL, C_TOTAL)
    return out


def _cmp_weights(pos_emb, w1, b1, w2):
    eye = jnp.eye(NSA_GROUPS, dtype=F32)
    w1r = w1.reshape(CMP_BLOCK, HEAD_DIM, CMP_HIDDEN)
    width = CMP_STRIDE * NSA_GROUPS * HEAD_DIM

    def spread(a):
        return jnp.einsum('pdj,gh->pgdhj', a, eye).reshape(width, NSA_GROUPS * CMP_HIDDEN).astype(BF16)

    def pe_row(p):
        return jnp.broadcast_to(p[:, None, :], (CMP_STRIDE, NSA_GROUPS, HEAD_DIM)).reshape(1, width)

    pe = jnp.concatenate([pe_row(pos_emb[:CMP_STRIDE]), pe_row(pos_emb[CMP_STRIDE:]),
                          jnp.zeros((6, width), F32)], axis=0)
    w2p = jnp.pad(w2, ((0, 0), (0, LANES - HEAD_DIM)))
    w2b = jnp.einsum('jc,gh->gjhc', w2p, eye).reshape(NSA_GROUPS * CMP_HIDDEN, NSA_GROUPS * LANES)
    return (pe, spread(w1r[:CMP_STRIDE]), spread(w1r[CMP_STRIDE:]),
            jnp.tile(b1, NSA_GROUPS)[None, :], w2b.astype(BF16))


def _rope_table():
    inv_freq = ROPE_THETA ** (-jnp.arange(0, ROT_DIM, 2, dtype=jnp.float32) / ROT_DIM)
    half = ROT_DIM // 2
    lane = np.arange(LANES)
    tab = jnp.zeros((8, LANES), F32)
    tab = tab.at[0, :ROT_DIM].set(jnp.tile(inv_freq, 2))
    tab = tab.at[1].set(jnp.asarray(np.where(lane < half, -1.0, 0.0), F32))
    tab = tab.at[2].set(jnp.asarray(np.where((lane >= half) & (lane < ROT_DIM), 1.0, 0.0), F32))
    tab = tab.at[3].set(jnp.asarray(np.where(lane == HEAD_DIM, 1.0, 0.0), F32))
    return tab


def kernel(x, mem, positions, ffn1_norm_pre, ffn1_norm_post, ffn1_w_in, ffn1_w_out, mix_norm_pre, mix_norm_post, mem_norm, w_mix_in, conv_w, cmp_pos_k, cmp_pos_v, cmp_k_w1, cmp_k_b1, cmp_k_w2, cmp_v_w1, cmp_v_b1, cmp_v_w2, w_mem_kv, w_branch_conv, w_branch_nsa, w_branch_mem, w_mix_out, ffn2_norm_pre, ffn2_norm_post, ffn2_w_in, ffn2_w_out):
    batch, seq, _ = x.shape
    t = batch * seq
    n_chunks = seq // CMP_STRIDE
    xf = x.reshape(t, D_MODEL)
    pos = positions.reshape(t, 1)
    tab = _rope_table()
    one = jnp.tile(tab[3:4, :], (1, NSA_GROUPS))
    row = lambda g: g[None, :]

    for l in range(DEPTH):
        xf = _ffn(xf, row(ffn1_norm_pre[l]), row(ffn1_norm_post[l]),
                  ffn1_w_in[l][:, :D_FF].astype(BF16), ffn1_w_in[l][:, D_FF:].astype(BF16),
                  ffn1_w_out[l].astype(BF16))

        mk, mv = _memkv(mem, row(mem_norm[l]), w_mem_kv[l].astype(BF16))
        convw = jnp.pad(conv_w[l], ((0, 8 - CONV_WIDTH), (0, 0)))
        (part, gnsa, qpt, qrt, kslc, kwin, vslct, vwint, kcr, vcr, gt) = _mixin(
            xf, pos, row(mix_norm_pre[l]), tab, _mix_weight(w_mix_in[l]), convw, mk, mv,
            w_branch_conv[l].astype(BF16), w_branch_mem[l].astype(BF16), batch, seq)

        width = CMP_STRIDE * NSA_GROUPS * HEAD_DIM
        kc = _compress(kcr.reshape(batch, n_chunks, width),
                       *_cmp_weights(cmp_pos_k[l], cmp_k_w1[l], cmp_k_b1[l], cmp_k_w2[l]), one, False)
        vct = _compress(vcr.reshape(batch, n_chunks, width),
                        *_cmp_weights(cmp_pos_v[l], cmp_v_w1[l], cmp_v_b1[l], cmp_v_w2[l]), one, True)

        ynsa = _nsa(qpt, qrt, gt, kc, vct,
                    kslc.reshape(batch, seq, NSA_GROUPS * LANES), vslct,
                    kwin.reshape(batch, seq, NSA_GROUPS * LANES), vwint)

        xf = _merge(xf, part, gnsa, ynsa.reshape(t, NSA_HEADS * HEAD_DIM),
                    w_branch_nsa[l].astype(BF16), w_mix_out[l].astype(BF16), row(mix_norm_post[l]))

        xf = _ffn(xf, row(ffn2_norm_pre[l]), row(ffn2_norm_post[l]),
                  ffn2_w_in[l][:, :D_FF].astype(BF16), ffn2_w_in[l][:, D_FF:].astype(BF16),
                  ffn2_w_out[l].astype(BF16))
    return xf.reshape(batch, seq, D_MODEL)
```

```python
import functools

import jax
import jax.numpy as jnp
import numpy as np
from jax import lax
from jax.experimental import pallas as pl
from jax.experimental.pallas import tpu as pltpu

D_MODEL = 1024
DEPTH = 2
CONV_DIM = 512
CONV_WIDTH = 3
NSA_HEADS = 8
NSA_GROUPS = 2
HEADS_PER_GROUP = NSA_HEADS // NSA_GROUPS
HEAD_DIM = 64
ROT_DIM = HEAD_DIM // 4
ROPE_THETA = 500000.0
CMP_BLOCK = 32
CMP_STRIDE = 16
CMP_HIDDEN = 256
SLC_BLOCK = 64
N_SELECT = 16
WINDOW = 512
MEM_HEADS = 4
MEM_HEAD_DIM = 128
D_FF = 2816
EPS = 1e-6
NEG = -1e30
REMOVED = -3e38

LANES = 128
F32 = jnp.float32
BF16 = jnp.bfloat16

FFN_TM = 512
MIX_TM = 512
MERGE_TM = 512
HALO = 8
ATT_T = 256
WIN_TILES = WINDOW // ATT_T
KT_GROUP = 4
BIAS_ROWS = KT_GROUP * ATT_T // SLC_BLOCK
VT_ROWS = HEAD_DIM + BIAS_ROWS
LOG2_E = 1.4426950408889634
CMP_CHUNK = 256
E_PAD = 8
N_FORCED = 3

C_U = 0
C_BG = C_U + CONV_DIM
C_CG = C_BG + CONV_DIM
C_Q = C_CG + CONV_DIM
C_KSLC = C_Q + NSA_HEADS * LANES
C_KWIN = C_KSLC + NSA_GROUPS * LANES
C_VSLC = C_KWIN + NSA_GROUPS * LANES
C_VWIN = C_VSLC + NSA_GROUPS * LANES
C_KCR = C_VWIN + NSA_GROUPS * LANES
C_VCR = C_KCR + LANES
C_GATE = C_VCR + LANES
C_QMEM = C_GATE + NSA_GROUPS * LANES
C_MERGE = C_QMEM + MEM_HEADS * MEM_HEAD_DIM
C_TOTAL = C_MERGE + 3 * D_MODEL

VMEM_LIMIT = 56 * 1024 * 1024


def _params(n_axes):
    return pltpu.CompilerParams(dimension_semantics=("arbitrary",) * n_axes,
                                vmem_limit_bytes=VMEM_LIMIT)


def _resident(shape, index_map):
    return pl.BlockSpec(shape, index_map, pipeline_mode=pl.Buffered(1))


def _rms(x, g):
    return x * lax.rsqrt(jnp.mean(x * x, axis=-1, keepdims=True) + EPS) * g


def _dot(a, b):
    return jnp.dot(a, b, preferred_element_type=F32)


def _ffn_body(x_ref, gpre_ref, gpost_ref, wa_ref, wb_ref, wo_ref, o_ref):
    x = x_ref[...]
    h = _rms(x, gpre_ref[...]).astype(BF16)
    a = _dot(h, wa_ref[...])
    b = _dot(h, wb_ref[...])
    g = (a * jax.nn.sigmoid(a) * b).astype(BF16)
    y = _dot(g, wo_ref[...])
    o_ref[...] = x + 0.5 * _rms(y, gpost_ref[...])


def _ffn(x, gpre, gpost, wa, wb, wo):
    t = x.shape[0]
    row = lambda i: (i, 0)
    fix = lambda i: (0, 0)
    return pl.pallas_call(
        _ffn_body,
        grid=(t // FFN_TM,),
        in_specs=[pl.BlockSpec((FFN_TM, D_MODEL), row),
                  _resident((1, D_MODEL), fix), _resident((1, D_MODEL), fix),
                  _resident((D_MODEL, D_FF), fix), _resident((D_MODEL, D_FF), fix),
                  _resident((D_FF, D_MODEL), fix)],
        out_specs=pl.BlockSpec((FFN_TM, D_MODEL), row),
        out_shape=jax.ShapeDtypeStruct((t, D_MODEL), F32),
        compiler_params=_params(1),
        name="ffn",
    )(x, gpre, gpost, wa, wb, wo)


def _memkv_body(mem_ref, g_ref, w_ref, mk_ref, mv_ref):
    h = _rms(mem_ref[0], g_ref[...]).astype(BF16)
    kv = _dot(h, w_ref[...])
    half = MEM_HEADS * MEM_HEAD_DIM
    mk_ref[0] = kv[:, :half].astype(BF16)
    mv_ref[0] = kv[:, half:].astype(BF16)


def _memkv(mem, g, w):
    b, m, _ = mem.shape
    half = MEM_HEADS * MEM_HEAD_DIM
    out = jax.ShapeDtypeStruct((b, m, half), BF16)
    return pl.pallas_call(
        _memkv_body,
        grid=(b,),
        in_specs=[pl.BlockSpec((1, m, D_MODEL), lambda i: (i, 0, 0)),
                  pl.BlockSpec((1, D_MODEL), lambda i: (0, 0)),
                  pl.BlockSpec((D_MODEL, 2 * half), lambda i: (0, 0))],
        out_specs=[pl.BlockSpec((1, m, half), lambda i: (i, 0, 0))] * 2,
        out_shape=[out, out],
        compiler_params=_params(1),
        name="memkv",
    )(mem, g, w)


def _mix_body(tiles_per_batch,
              x_ref, halo_ref, pos_ref, gpre_ref, tab_ref, w_ref, convw_ref, mk_ref, mv_ref,
              wbc_ref, wbm_ref,
              part_ref, gnsa_ref, qpt_ref, qrt_ref, kslc_ref, kwin_ref, vslct_ref, vwint_ref,
              kcr_ref, vcr_ref, gatest_ref,
              cu_scr):
    tm = MIX_TM
    i = pl.program_id(0)
    gpre = gpre_ref[...]
    h = _rms(x_ref[...], gpre).astype(BF16)
    hh = _rms(halo_ref[...], gpre).astype(BF16)

    def proj(hv, c0, width):
        return _dot(hv, w_ref[:, c0:c0 + width])

    cu = proj(h, C_CG, CONV_DIM) * proj(h, C_U, CONV_DIM)
    cu_h = proj(hh, C_CG, CONV_DIM) * proj(hh, C_U, CONV_DIM)
    first = (i % tiles_per_batch) == 0
    cu_scr[0:HALO, :] = jnp.where(first, 0.0, cu_h)
    cu_scr[HALO:HALO + tm, :] = cu
    cw = convw_ref[...]
    conv = (cu * cw[2:3, :]
            + cu_scr[HALO - 1:HALO - 1 + tm, :] * cw[1:2, :]
            + cu_scr[HALO - 2:HALO - 2 + tm, :] * cw[0:1, :])
    y_conv = (proj(h, C_BG, CONV_DIM) * conv).astype(BF16)

    ang = pos_ref[...].astype(F32) * tab_ref[0:1, :]
    cos_t = jnp.cos(ang)
    sin_t = jnp.sin(ang)
    sin_a = sin_t * tab_ref[1:2, :]
    sin_b = sin_t * tab_ref[2:3, :]

    def rope(t):
        half = ROT_DIM // 2
        return (t * cos_t + pltpu.roll(t, LANES - half, 1) * sin_a
                + pltpu.roll(t, half, 1) * sin_b)

    wide_starts = (C_Q, C_KSLC, C_QMEM, C_MERGE)
    wide = [proj(h, a, b - a) for a, b in zip(wide_starts[:-1], wide_starts[1:])]

    def nsa_cols(c0):
        k = max(j for j, a in enumerate(wide_starts[:-1]) if a <= c0)
        return wide[k][:, c0 - wide_starts[k]:c0 - wide_starts[k] + LANES]

    scale = HEAD_DIM ** -0.5 * LOG2_E
    for hd in range(NSA_HEADS):
        q = nsa_cols(C_Q + hd * LANES) * scale
        sl = slice(hd * LANES, (hd + 1) * LANES)
        for j in range(tm // ATT_T):
            rows = slice(j * ATT_T, (j + 1) * ATT_T)
            qpt_ref[0, sl, rows] = q[rows].T.astype(BF16)
            qrt_ref[0, sl, rows] = rope(q)[rows].T.astype(BF16)
    one_hot = tab_ref[3:4, :]
    seq_row = (i % tiles_per_batch) * tm + lax.broadcasted_iota(jnp.int32, (tm, LANES), 0)
    lane = lax.broadcasted_iota(jnp.int32, (tm, LANES), 1)
    blk_hot = jnp.where(lane - HEAD_DIM == (seq_row // SLC_BLOCK) % BIAS_ROWS, 1.0, 0.0)
    for g in range(NSA_GROUPS):
        sl = slice(g * LANES, (g + 1) * LANES)
        kslc_ref[:, sl] = (rope(nsa_cols(C_KSLC + g * LANES)) + blk_hot).astype(BF16)
        kwin_ref[:, sl] = rope(nsa_cols(C_KWIN + g * LANES)).astype(BF16)
        vs = nsa_cols(C_VSLC + g * LANES) + one_hot
        vw = nsa_cols(C_VWIN + g * LANES) + one_hot
        gt = jax.nn.sigmoid(nsa_cols(C_GATE + g * LANES))
        for j in range(tm // ATT_T):
            rows = slice(j * ATT_T, (j + 1) * ATT_T)
            vslct_ref[0, j, g] = vs[rows].T[0:VT_ROWS].astype(BF16)
            vwint_ref[0, j, g] = vw[rows].T[0:VT_ROWS].astype(BF16)
            gatest_ref[0, sl, rows] = gt[rows].T
    kcr_ref[...] = nsa_cols(C_KCR)
    vcr_ref[...] = nsa_cols(C_VCR)

    mem_scale = MEM_HEAD_DIM ** -0.5
    outs = []
    for hd in range(MEM_HEADS):
        sl = slice(hd * MEM_HEAD_DIM, (hd + 1) * MEM_HEAD_DIM)
        qm = (nsa_cols(C_QMEM + hd * MEM_HEAD_DIM) * mem_scale).astype(BF16)
        s = lax.dot_general(qm, mk_ref[0, :, sl], (((1,), (1,)), ((), ())),
                            preferred_element_type=F32)
        p = jnp.exp(s - jnp.max(s, axis=-1, keepdims=True))
        l = jnp.sum(p, axis=-1, keepdims=True)
        outs.append((_dot(p.astype(BF16), mv_ref[0, :, sl]) / l).astype(BF16))
    y_mem = jnp.concatenate(outs, axis=1)

    g_conv = jax.nn.sigmoid(proj(h, C_MERGE, D_MODEL))
    g_nsa = jax.nn.sigmoid(proj(h, C_MERGE + D_MODEL, D_MODEL))
    g_mem = jax.nn.sigmoid(proj(h, C_MERGE + 2 * D_MODEL, D_MODEL))
    part_ref[...] = g_conv * _dot(y_conv, wbc_ref[...]) + g_mem * _dot(y_mem, wbm_ref[...])
    gnsa_ref[...] = g_nsa.astype(BF16)


def _mixin(x, pos, gpre, tab, w_main, convw, mk, mv, wbc, wbm, batch, seq):
    t = batch * seq
    tm = MIX_TM
    tpb = seq // tm
    nt = tm // ATT_T
    row = lambda i: (i, 0)
    fix = lambda i: (0, 0)
    bt = lambda i: (i // tpb, 0, i % tpb)
    mem_len = mk.shape[1]
    half = MEM_HEADS * MEM_HEAD_DIM
    out_shape = [
        jax.ShapeDtypeStruct((t, D_MODEL), F32),
        jax.ShapeDtypeStruct((t, D_MODEL), BF16),
        jax.ShapeDtypeStruct((batch, NSA_HEADS * LANES, seq), BF16),
        jax.ShapeDtypeStruct((batch, NSA_HEADS * LANES, seq), BF16),
        jax.ShapeDtypeStruct((t, NSA_GROUPS * LANES), BF16),
        jax.ShapeDtypeStruct((t, NSA_GROUPS * LANES), BF16),
        jax.ShapeDtypeStruct((batch, seq // ATT_T, NSA_GROUPS, VT_ROWS, ATT_T), BF16),
        jax.ShapeDtypeStruct((batch, seq // ATT_T, NSA_GROUPS, VT_ROWS, ATT_T), BF16),
        jax.ShapeDtypeStruct((t, LANES), F32),
        jax.ShapeDtypeStruct((t, LANES), F32),
        jax.ShapeDtypeStruct((batch, NSA_GROUPS * LANES, seq), F32),
    ]
    vt_spec = pl.BlockSpec((1, nt, NSA_GROUPS, VT_ROWS, ATT_T), lambda i: (i // tpb, i % tpb, 0, 0, 0))
    out_specs = [
        pl.BlockSpec((tm, D_MODEL), row),
        pl.BlockSpec((tm, D_MODEL), row),
        pl.BlockSpec((1, NSA_HEADS * LANES, tm), bt),
        pl.BlockSpec((1, NSA_HEADS * LANES, tm), bt),
        pl.BlockSpec((tm, NSA_GROUPS * LANES), row),
        pl.BlockSpec((tm, NSA_GROUPS * LANES), row),
        vt_spec, vt_spec,
        pl.BlockSpec((tm, LANES), row),
        pl.BlockSpec((tm, LANES), row),
        pl.BlockSpec((1, NSA_GROUPS * LANES, tm), bt),
    ]
    in_specs = [
        pl.BlockSpec((tm, D_MODEL), row),
        pl.BlockSpec((HALO, D_MODEL), lambda i: (jnp.maximum(i * (tm // HALO) - 1, 0), 0)),
        pl.BlockSpec((tm, 1), row),
        _resident((1, D_MODEL), fix),
        _resident((8, LANES), fix),
        _resident((D_MODEL, C_TOTAL), fix),
        _resident((8, CONV_DIM), fix),
        pl.BlockSpec((1, mem_len, half), lambda i: (i // tpb, 0, 0)),
        pl.BlockSpec((1, mem_len, half), lambda i: (i // tpb, 0, 0)),
        _resident((CONV_DIM, D_MODEL), fix),
        _resident((half, D_MODEL), fix),
    ]
    return pl.pallas_call(
        functools.partial(_mix_body, tpb),
        grid=(t // tm,),
        in_specs=in_specs,
        out_specs=out_specs,
        out_shape=out_shape,
        scratch_shapes=[pltpu.VMEM((HALO + tm, CONV_DIM), F32)],
        compiler_params=_params(1),
        name="mixin",
    )(x, x, pos, gpre, tab, w_main, convw, mk, mv, wbc, wbm)


def _cmp_body(transpose_out, r_ref, pe_ref, w1a_ref, w1b_ref, b1_ref, w2_ref, one_ref, o_ref):
    r = r_ref[0]
    n = r.shape[0]
    a = _dot((r + pe_ref[0:1, :]).astype(BF16), w1a_ref[...])
    b = _dot((r + pe_ref[1:2, :]).astype(BF16), w1b_ref[...])
    hid = a + pltpu.roll(b, n - 1, 0) + b1_ref[...]
    o = _dot(jax.nn.gelu(hid).astype(BF16), w2_ref[...])
    if transpose_out:
        o = o + one_ref[...]
        o_ref[0] = jnp.concatenate(
            [o[:, g * LANES:(g + 1) * LANES].T for g in range(NSA_GROUPS)], axis=0).astype(BF16)
    else:
        o_ref[0] = o.astype(BF16)


def _compress(r, pe, w1a, w1b, b1, w2, one, transpose_out):
    b, n, width = r.shape
    fix = lambda i: (0, 0)
    oshape = (b, NSA_GROUPS * LANES, n) if transpose_out else (b, n, NSA_GROUPS * LANES)
    return pl.pallas_call(
        functools.partial(_cmp_body, transpose_out),
        grid=(b,),
        in_specs=[pl.BlockSpec((1, n, width), lambda i: (i, 0, 0)),
                  pl.BlockSpec((8, width), fix),
                  pl.BlockSpec(w1a.shape, fix), pl.BlockSpec(w1b.shape, fix),
                  pl.BlockSpec(b1.shape, fix), pl.BlockSpec(w2.shape, fix),
                  pl.BlockSpec(one.shape, fix)],
        out_specs=pl.BlockSpec((1,) + oshape[1:], lambda i: (i, 0, 0)),
        out_shape=jax.ShapeDtypeStruct(oshape, BF16),
        compiler_params=_params(1),
        name="compress_v" if transpose_out else "compress_k",
    )(r, pe, w1a, w1b, b1, w2, one)


def _nsa_body(qpt_ref, qrt_ref, gt_ref, kc_ref, vct_ref,
              kslc_ref, vslct_ref, kwin_ref, vwint_ref,
              y_ref,
              acc_ref, m_ref, selb_ref, sa_ref, sb_ref, e_ref, ga_ref, gb_ref, gmax_ref, yacc_ref):
    tq = ATT_T
    hg = HEADS_PER_GROUP
    i = pl.program_id(2)
    start = i * tq
    t_row = start + lax.broadcasted_iota(jnp.int32, (1, tq), 1)

    def lane_cat(ref):
        return jnp.concatenate([ref[0, h * LANES:(h + 1) * LANES, :] for h in range(hg)], axis=1)

    qp4 = lane_cat(qpt_ref)
    qr4 = lane_cat(qrt_ref)

    any_c = t_row >= CMP_BLOCK - 1
    per_slc = SLC_BLOCK // CMP_STRIDE
    n_half = tq // LANES
    cur = t_row // SLC_BLOCK
    e_ref[:, 0:E_PAD, :] = jnp.zeros((n_half, E_PAD, LANES), F32)

    def cmp_and_select(n_chunk):
        rows = n_chunk * CMP_CHUNK
        nblk = rows // per_slc
        n_idx = lax.broadcasted_iota(jnp.int32, (rows, tq), 0)
        bias_c = jnp.where((n_idx * CMP_STRIDE + (CMP_BLOCK - 1)) <= t_row, 0.0, NEG)
        m4 = jnp.full((1, hg * tq), NEG, F32)
        for c in range(n_chunk):
            rs = slice(c * CMP_CHUNK, (c + 1) * CMP_CHUNK)
            st = _dot(kc_ref[0, rs, :], qp4)
            gb_ref[rs, :] = st
            m4 = jnp.maximum(m4, jnp.max(st + jnp.concatenate([bias_c[rs]] * hg, axis=1),
                                         axis=0, keepdims=True))
        imp = jnp.zeros((nblk, tq), F32)
        for h in range(hg):
            hs = slice(h * tq, (h + 1) * tq)
            e = jnp.exp2(gb_ref[0:rows, hs] + bias_c - m4[:, hs])
            for c in range(n_half):
                e_ref[c, E_PAD:E_PAD + rows, :] = e[:, c * LANES:(c + 1) * LANES]
            oc = _dot(vct_ref[0, :, 0:rows], e.astype(BF16))
            rl = jnp.where(any_c, 1.0 / jnp.maximum(oc[HEAD_DIM:HEAD_DIM + 1, :], 1e-30), 0.0)
            yacc_ref[h * HEAD_DIM:(h + 1) * HEAD_DIM, :] = oc[0:HEAD_DIM, :] * (rl * gt_ref[0, 3 * h:3 * h + 1, :])
            tap = lambda d: jnp.concatenate(
                [e_ref[c, pl.ds(E_PAD + d, nblk, stride=per_slc), :] for c in range(n_half)], axis=1)
            imp_h = 0.5 * (tap(-1) + tap(per_slc - 1))
            for d in range(per_slc - 1):
                imp_h = imp_h + tap(d)
            imp = imp + imp_h * rl

        blk = lax.broadcasted_iota(jnp.int32, (nblk, tq), 0)
        blk_f = blk.astype(F32)
        forced = (blk == 0) | (blk == cur) | (blk == cur - 1)
        v = jnp.where(blk > cur, NEG, jnp.where(forced, REMOVED, imp))
        for _ in range(N_SELECT - N_FORCED):
            mx = jnp.max(v, axis=0, keepdims=True)
            first = jnp.min(jnp.where(v == mx, blk_f, float(nblk)), axis=0, keepdims=True)
            v = jnp.where(blk_f == first, REMOVED, v)
        selb_ref[0:nblk, :] = jnp.where((v == REMOVED) & (blk <= cur), 0.0, NEG)

    n_chunks_total = kc_ref.shape[1] // CMP_CHUNK
    visible = (start + tq - CMP_BLOCK) // (CMP_STRIDE * CMP_CHUNK) + 1
    for n_chunk in range(1, n_chunks_total + 1):
        pl.when(visible == n_chunk)(functools.partial(cmp_and_select, n_chunk))

    def reset():
        acc_ref[...] = jnp.zeros_like(acc_ref)
        m_ref[...] = jnp.full_like(m_ref, NEG)

    def soft_pv(scores, vt, bias=None):
        ps, alphas = [], []
        for h in range(hg):
            hs = slice(h * tq, (h + 1) * tq)
            load = (lambda: scores(hs)) if bias is None else (lambda: scores(hs) + bias)
            m_old = m_ref[0:1, hs]
            m_new = jnp.maximum(m_old, jnp.max(load(), axis=0, keepdims=True))
            alphas.append(jnp.exp2(m_old - m_new))
            ps.append(jnp.exp2(load() - m_new).astype(BF16))
            m_ref[0:1, hs] = m_new
        alpha = jnp.concatenate(alphas, axis=1)
        acc_ref[...] = acc_ref[...] * alpha + _dot(vt, jnp.concatenate(ps, axis=1))

    def finish(branch):
        for h in range(hg):
            hs = slice(h * tq, (h + 1) * tq)
            o = acc_ref[0:HEAD_DIM, hs] * (1.0 / acc_ref[HEAD_DIM:HEAD_DIM + 1, hs])
            yacc_ref[h * HEAD_DIM:(h + 1) * HEAD_DIM, :] += o * gt_ref[0, 3 * h + branch:3 * h + branch + 1, :]

    k_iota = lax.broadcasted_iota(jnp.int32, (tq, 1), 0)

    def q_biased(grp):
        band = selb_ref[pl.ds(pl.multiple_of(grp * BIAS_ROWS, BIAS_ROWS), BIAS_ROWS), :].astype(BF16)
        return jnp.concatenate([qr4[0:HEAD_DIM], jnp.concatenate([band] * hg, axis=1),
                                qr4[HEAD_DIM + BIAS_ROWS:]], axis=0)

    def key_rows(ref, kt):
        return ref[0, pl.ds(pl.multiple_of(kt * tq, tq), tq), :]

    def score_tile(grp, j, q4, dst_ref, gmax):
        st = _dot(key_rows(kslc_ref, grp * KT_GROUP + j), q4)
        dst_ref[j * tq:(j + 1) * tq, :] = st
        return jnp.maximum(gmax, jnp.max(st, axis=0, keepdims=True))

    def slc_group(grp, src_ref, dst_ref):
        m_old = m_ref[0:1, :]
        m_new = jnp.maximum(m_old, gmax_ref[0:1, :])
        q_next = q_biased(grp + 1)
        gmax = jnp.full((1, hg * tq), NEG, F32)
        part = jnp.zeros(acc_ref.shape, F32)
        for j in range(KT_GROUP):
            rows = slice(j * tq, (j + 1) * tq)
            gmax = score_tile(grp + 1, j, q_next, dst_ref, gmax)
            p = jnp.concatenate(
                [jnp.exp2(src_ref[rows, h * tq:(h + 1) * tq] - m_new[:, h * tq:(h + 1) * tq]).astype(BF16)
                 for h in range(hg)], axis=1)
            part = part + _dot(vslct_ref[0, grp * KT_GROUP + j], p)
        acc_ref[...] = acc_ref[...] * jnp.exp2(m_old - m_new) + part
        m_ref[0:1, :] = m_new
        gmax_ref[0:1, :] = gmax

    def score_first(dst_ref):
        gmax = jnp.full((1, hg * tq), NEG, F32)
        for j in range(KT_GROUP):
            gmax = score_tile(0, j, q_biased(0), dst_ref, gmax)
        gmax_ref[0:1, :] = gmax

    reset()
    n_grp = i // KT_GROUP
    odd = n_grp % 2

    @pl.when(odd == 0)
    def _():
        score_first(ga_ref)

    @pl.when(odd == 1)
    def _():
        score_first(gb_ref)
        slc_group(0, gb_ref, ga_ref)

    def slc_pair(pair, carry):
        slc_group(2 * pair + odd, ga_ref, gb_ref)
        slc_group(2 * pair + odd + 1, gb_ref, ga_ref)
        return carry

    lax.fori_loop(0, n_grp // 2, slc_pair, 0)

    def win_tile(back):
        kt = jnp.maximum(i - back, 0)
        kpos = kt * tq + k_iota
        ok = (kpos <= t_row) & (kpos > t_row - WINDOW) & (i >= back)
        return kt, jnp.where(ok, 0.0, NEG)

    kts, biases = zip(*[win_tile(back) for back in range(WIN_TILES, -1, -1)])
    sa_ref[...] = _dot(key_rows(kwin_ref, kts[0]), qr4)

    for j in range(KT_GROUP):
        kt = n_grp * KT_GROUP + j
        tile = lambda hs, j=j: ga_ref[j * tq:(j + 1) * tq, hs]

        @pl.when(kt < i)
        def _():
            soft_pv(tile, vslct_ref[0, kt])

        @pl.when(kt == i)
        def _():
            soft_pv(tile, vslct_ref[0, kt], jnp.where((start + k_iota) <= t_row, 0.0, NEG))
    finish(1)
    reset()
    bufs = (sa_ref, sb_ref)
    for w in range(WIN_TILES + 1):
        if w + 1 <= WIN_TILES:
            bufs[(w + 1) % 2][...] = _dot(key_rows(kwin_ref, kts[w + 1]), qr4)
        soft_pv(lambda hs, w=w: bufs[w % 2][:, hs], vwint_ref[0, kts[w]], biases[w])
    finish(2)

    y_ref[0] = yacc_ref[...].T.astype(BF16)


def _nsa(qpt, qrt, gt, kc, vct, kslc, vslct, kwin, vwint):
    b, _, seq = qpt.shape
    tq = ATT_T
    hg = HEADS_PER_GROUP
    n_cmp = kc.shape[1]
    nkt = seq // tq
    qspec = pl.BlockSpec((1, hg * LANES, tq), lambda bi, g, i: (bi, g, i))
    in_specs = [
        qspec, qspec,
        pl.BlockSpec((1, LANES, tq), lambda bi, g, i: (bi, g, i)),
        pl.BlockSpec((1, n_cmp, LANES), lambda bi, g, i: (bi, 0, g)),
        pl.BlockSpec((1, LANES, n_cmp), lambda bi, g, i: (bi, g, 0)),
        _resident((1, seq, LANES), lambda bi, g, i: (bi, 0, g)),
        _resident((1, nkt, None, VT_ROWS, tq), lambda bi, g, i: (bi, 0, g, 0, 0)),
        _resident((1, seq, LANES), lambda bi, g, i: (bi, 0, g)),
        _resident((1, nkt, None, VT_ROWS, tq), lambda bi, g, i: (bi, 0, g, 0, 0)),
    ]
    return pl.pallas_call(
        _nsa_body,
        grid=(b, NSA_GROUPS, seq // tq),
        in_specs=in_specs,
        out_specs=pl.BlockSpec((1, tq, hg * HEAD_DIM), lambda bi, g, i: (bi, i, g)),
        out_shape=jax.ShapeDtypeStruct((b, seq, NSA_HEADS * HEAD_DIM), BF16),
        scratch_shapes=[pltpu.VMEM((VT_ROWS, hg * tq), F32),
                        pltpu.VMEM((8, hg * tq), F32),
                        pltpu.VMEM((seq // SLC_BLOCK, tq), F32),
                        pltpu.VMEM((tq, hg * tq), F32),
                        pltpu.VMEM((tq, hg * tq), F32),
                        pltpu.VMEM((tq // LANES, E_PAD + n_cmp, LANES), F32),
                        pltpu.VMEM((KT_GROUP * tq, hg * tq), F32),
                        pltpu.VMEM((KT_GROUP * tq, hg * tq), F32),
                        pltpu.VMEM((8, hg * tq), F32),
                        pltpu.VMEM((hg * HEAD_DIM, tq), F32)],
        compiler_params=_params(3),
        name="nsa",
    )(qpt, qrt, gt, kc, vct, kslc, vslct, kwin, vwint)


def _merge_body(x_ref, part_ref, gnsa_ref, ynsa_ref, wbn_ref, wo_ref, gpost_ref, o_ref):
    merged = part_ref[...] + gnsa_ref[...].astype(F32) * _dot(ynsa_ref[...], wbn_ref[...])
    y = _dot(merged.astype(BF16), wo_ref[...])
    o_ref[...] = x_ref[...] + _rms(y, gpost_ref[...])


def _merge(x, part, gnsa, ynsa, wbn, wo, gpost):
    t = x.shape[0]
    tm = MERGE_TM
    row = lambda i: (i, 0)
    fix = lambda i: (0, 0)
    return pl.pallas_call(
        _merge_body,
        grid=(t // tm,),
        in_specs=[pl.BlockSpec((tm, D_MODEL), row), pl.BlockSpec((tm, D_MODEL), row),
                  pl.BlockSpec((tm, D_MODEL), row), pl.BlockSpec((tm, NSA_HEADS * HEAD_DIM), row),
                  _resident(wbn.shape, fix), _resident(wo.shape, fix), _resident((1, D_MODEL), fix)],
        out_specs=pl.BlockSpec((tm, D_MODEL), row),
        out_shape=jax.ShapeDtypeStruct((t, D_MODEL), F32),
        compiler_params=_params(1),
        name="merge",
    )(x, part, gnsa, ynsa, wbn, wo, gpost)


def _pad_lanes(w, groups, width):
    d = w.shape[0]
    w = w.reshape(d, groups, width)
    return jnp.pad(w, ((0, 0), (0, 0), (0, LANES - width))).reshape(d, groups * LANES)


def _mix_weight(w):
    o = 0
    u = w[:, o:o + CONV_DIM]; o += CONV_DIM
    bg = w[:, o:o + CONV_DIM]; o += CONV_DIM
    cg = w[:, o:o + CONV_DIM]; o += CONV_DIM
    q = w[:, o:o + NSA_HEADS * HEAD_DIM]; o += NSA_HEADS * HEAD_DIM
    kv = w[:, o:o + 6 * NSA_GROUPS * HEAD_DIM].reshape(D_MODEL, 6, NSA_GROUPS * HEAD_DIM)
    o += 6 * NSA_GROUPS * HEAD_DIM
    gate = w[:, o:o + 3 * NSA_HEADS]; o += 3 * NSA_HEADS
    qmem = w[:, o:o + MEM_HEADS * MEM_HEAD_DIM]; o += MEM_HEADS * MEM_HEAD_DIM
    merge = w[:, o:]
    slab = lambda c: _pad_lanes(kv[:, c], NSA_GROUPS, HEAD_DIM)
    cols = [u, bg, cg, _pad_lanes(q, NSA_HEADS, HEAD_DIM),
            slab(2), slab(4), slab(3), slab(5), kv[:, 0], kv[:, 1],
            _pad_lanes(gate, NSA_GROUPS, 3 * HEADS_PER_GROUP), qmem, merge]
    out = jnp.concatenate(cols, axis=1).astype(BF16)
    assert out.shape == (D_MODEL, C_TOTAL)
    return out


def _cmp_weights(pos_emb, w1, b1, w2):
    eye = jnp.eye(NSA_GROUPS, dtype=F32)
    w1r = w1.reshape(CMP_BLOCK, HEAD_DIM, CMP_HIDDEN)
    width = CMP_STRIDE * NSA_GROUPS * HEAD_DIM

    def spread(a):
        return jnp.einsum('pdj,gh->pgdhj', a, eye).reshape(width, NSA_GROUPS * CMP_HIDDEN).astype(BF16)

    def pe_row(p):
        return jnp.broadcast_to(p[:, None, :], (CMP_STRIDE, NSA_GROUPS, HEAD_DIM)).reshape(1, width)

    pe = jnp.concatenate([pe_row(pos_emb[:CMP_STRIDE]), pe_row(pos_emb[CMP_STRIDE:]),
                          jnp.zeros((6, width), F32)], axis=0)
    w2p = jnp.pad(w2, ((0, 0), (0, LANES - HEAD_DIM)))
    w2b = jnp.einsum('jc,gh->gjhc', w2p, eye).reshape(NSA_GROUPS * CMP_HIDDEN, NSA_GROUPS * LANES)
    return (pe, spread(w1r[:CMP_STRIDE]), spread(w1r[CMP_STRIDE:]),
            jnp.tile(b1, NSA_GROUPS)[None, :], w2b.astype(BF16))


def _rope_table():
    inv_freq = ROPE_THETA ** (-jnp.arange(0, ROT_DIM, 2, dtype=jnp.float32) / ROT_DIM)
    half = ROT_DIM // 2
    lane = np.arange(LANES)
    tab = jnp.zeros((8, LANES), F32)
    tab = tab.at[0, :ROT_DIM].set(jnp.tile(inv_freq, 2))
    tab = tab.at[1].set(jnp.asarray(np.where(lane < half, -1.0, 0.0), F32))
    tab = tab.at[2].set(jnp.asarray(np.where((lane >= half) & (lane < ROT_DIM), 1.0, 0.0), F32))
    tab = tab.at[3].set(jnp.asarray(np.where(lane == HEAD_DIM, 1.0, 0.0), F32))
    return tab


def kernel(x, mem, positions, ffn1_norm_pre, ffn1_norm_post, ffn1_w_in, ffn1_w_out, mix_norm_pre, mix_norm_post, mem_norm, w_mix_in, conv_w, cmp_pos_k, cmp_pos_v, cmp_k_w1, cmp_k_b1, cmp_k_w2, cmp_v_w1, cmp_v_b1, cmp_v_w2, w_mem_kv, w_branch_conv, w_branch_nsa, w_branch_mem, w_mix_out, ffn2_norm_pre, ffn2_norm_post, ffn2_w_in, ffn2_w_out):
    batch, seq, _ = x.shape
    t = batch * seq
    n_chunks = seq // CMP_STRIDE
    xf = x.reshape(t, D_MODEL)
    pos = positions.reshape(t, 1)
    tab = _rope_table()
    one = jnp.tile(tab[3:4, :], (1, NSA_GROUPS))
    row = lambda g: g[None, :]

    for l in range(DEPTH):
        xf = _ffn(xf, row(ffn1_norm_pre[l]), row(ffn1_norm_post[l]),
                  ffn1_w_in[l][:, :D_FF].astype(BF16), ffn1_w_in[l][:, D_FF:].astype(BF16),
                  ffn1_w_out[l].astype(BF16))

        mk, mv = _memkv(mem, row(mem_norm[l]), w_mem_kv[l].astype(BF16))
        convw = jnp.pad(conv_w[l], ((0, 8 - CONV_WIDTH), (0, 0)))
        (part, gnsa, qpt, qrt, kslc, kwin, vslct, vwint, kcr, vcr, gt) = _mixin(
            xf, pos, row(mix_norm_pre[l]), tab, _mix_weight(w_mix_in[l]), convw, mk, mv,
            w_branch_conv[l].astype(BF16), w_branch_mem[l].astype(BF16), batch, seq)

        width = CMP_STRIDE * NSA_GROUPS * HEAD_DIM
        kc = _compress(kcr.reshape(batch, n_chunks, width),
                       *_cmp_weights(cmp_pos_k[l], cmp_k_w1[l], cmp_k_b1[l], cmp_k_w2[l]), one, False)
        vct = _compress(vcr.reshape(batch, n_chunks, width),
                        *_cmp_weights(cmp_pos_v[l], cmp_v_w1[l], cmp_v_b1[l], cmp_v_w2[l]), one, True)

        ynsa = _nsa(qpt, qrt, gt, kc, vct,
                    kslc.reshape(batch, seq, NSA_GROUPS * LANES), vslct,
                    kwin.reshape(batch, seq, NSA_GROUPS * LANES), vwint)

        xf = _merge(xf, part, gnsa, ynsa.reshape(t, NSA_HEADS * HEAD_DIM),
                    w_branch_nsa[l].astype(BF16), w_mix_out[l].astype(BF16), row(mix_norm_post[l]))

        xf = _ffn(xf, row(ffn2_norm_pre[l]), row(ffn2_norm_post[l]),
                  ffn2_w_in[l][:, :D_FF].astype(BF16), ffn2_w_in[l][:, D_FF:].astype(BF16),
                  ffn2_w_out[l].astype(BF16))
    return xf.reshape(batch, seq, D_MODEL)
```

```python
import functools

import jax
import jax.numpy as jnp
import numpy as np
from jax import lax
from jax.experimental import pallas as pl
from jax.experimental.pallas import tpu as pltpu

D_MODEL = 1024
DEPTH = 2
CONV_DIM = 512
CONV_WIDTH = 3
NSA_HEADS = 8
NSA_GROUPS = 2
HEADS_PER_GROUP = NSA_HEADS // NSA_GROUPS
HEAD_DIM = 64
ROT_DIM = HEAD_DIM // 4
ROPE_THETA = 500000.0
CMP_BLOCK = 32
CMP_STRIDE = 16
CMP_HIDDEN = 256
SLC_BLOCK = 64
N_SELECT = 16
WINDOW = 512
MEM_HEADS = 4
MEM_HEAD_DIM = 128
D_FF = 2816
EPS = 1e-6
NEG = -1e30
REMOVED = -3e38

LANES = 128
F32 = jnp.float32
BF16 = jnp.bfloat16

FFN_TM = 512
MIX_TM = 512
MERGE_TM = 512
MERGE_FF_SPLITS = (0, 1536, D_FF)
HALO = 8
ATT_T = 256
WIN_TILES = WINDOW // ATT_T
KT_GROUP = 4
BIAS_ROWS = KT_GROUP * ATT_T // SLC_BLOCK
VT_ROWS = HEAD_DIM + BIAS_ROWS
LOG2_E = 1.4426950408889634
CMP_CHUNK = 256
E_PAD = 8
N_FORCED = 3

C_U = 0
C_BG = C_U + CONV_DIM
C_CG = C_BG + CONV_DIM
C_Q = C_CG + CONV_DIM
C_KSLC = C_Q + NSA_HEADS * LANES
C_KWIN = C_KSLC + NSA_GROUPS * LANES
C_VSLC = C_KWIN + NSA_GROUPS * LANES
C_VWIN = C_VSLC + NSA_GROUPS * LANES
C_KCR = C_VWIN + NSA_GROUPS * LANES
C_VCR = C_KCR + LANES
C_GATE = C_VCR + LANES
C_QMEM = C_GATE + NSA_GROUPS * LANES
C_MERGE = C_QMEM + MEM_HEADS * MEM_HEAD_DIM
C_TOTAL = C_MERGE + 3 * D_MODEL

VMEM_LIMIT = 56 * 1024 * 1024


def _params(n_axes):
    return pltpu.CompilerParams(dimension_semantics=("arbitrary",) * n_axes,
                                vmem_limit_bytes=VMEM_LIMIT)


def _resident(shape, index_map):
    return pl.BlockSpec(shape, index_map, pipeline_mode=pl.Buffered(1))


def _rms(x, g):
    return x * lax.rsqrt(jnp.mean(x * x, axis=-1, keepdims=True) + EPS) * g


def _dot(a, b):
    return jnp.dot(a, b, preferred_element_type=F32)


def _ffn_body(x_ref, gpre_ref, gpost_ref, wa_ref, wb_ref, wo_ref, o_ref):
    x = x_ref[...]
    h = _rms(x, gpre_ref[...]).astype(BF16)
    a = _dot(h, wa_ref[...])
    b = _dot(h, wb_ref[...])
    g = (a * jax.nn.sigmoid(a) * b).astype(BF16)
    y = _dot(g, wo_ref[...])
    o_ref[...] = x + 0.5 * _rms(y, gpost_ref[...])


def _ffn(x, gpre, gpost, wa, wb, wo):
    t = x.shape[0]
    row = lambda i: (i, 0)
    fix = lambda i: (0, 0)
    return pl.pallas_call(
        _ffn_body,
        grid=(t // FFN_TM,),
        in_specs=[pl.BlockSpec((FFN_TM, D_MODEL), row),
                  _resident((1, D_MODEL), fix), _resident((1, D_MODEL), fix),
                  _resident((D_MODEL, D_FF), fix), _resident((D_MODEL, D_FF), fix),
                  _resident((D_FF, D_MODEL), fix)],
        out_specs=pl.BlockSpec((FFN_TM, D_MODEL), row),
        out_shape=jax.ShapeDtypeStruct((t, D_MODEL), F32),
        compiler_params=_params(1),
        name="ffn",
    )(x, gpre, gpost, wa, wb, wo)


def _memkv_body(mem_ref, g_ref, w_ref, mk_ref, mv_ref):
    h = _rms(mem_ref[0], g_ref[...]).astype(BF16)
    kv = _dot(h, w_ref[...])
    half = MEM_HEADS * MEM_HEAD_DIM
    mk_ref[0] = kv[:, :half].astype(BF16)
    mv_ref[0] = kv[:, half:].astype(BF16)


def _memkv(mem, g, w):
    b, m, _ = mem.shape
    half = MEM_HEADS * MEM_HEAD_DIM
    out = jax.ShapeDtypeStruct((b, m, half), BF16)
    return pl.pallas_call(
        _memkv_body,
        grid=(b,),
        in_specs=[pl.BlockSpec((1, m, D_MODEL), lambda i: (i, 0, 0)),
                  pl.BlockSpec((1, D_MODEL), lambda i: (0, 0)),
                  pl.BlockSpec((D_MODEL, 2 * half), lambda i: (0, 0))],
        out_specs=[pl.BlockSpec((1, m, half), lambda i: (i, 0, 0))] * 2,
        out_shape=[out, out],
        compiler_params=_params(1),
        name="memkv",
    )(mem, g, w)


def _mix_body(tiles_per_batch,
              x_ref, halo_ref, pos_ref, gpre_ref, tab_ref, w_ref, convw_ref, mk_ref, mv_ref,
              wbc_ref, wbm_ref,
              part_ref, gnsa_ref, qpt_ref, qrt_ref, kslc_ref, kwin_ref, vslct_ref, vwint_ref,
              kcr_ref, vcr_ref, gatest_ref,
              cu_scr):
    tm = MIX_TM
    i = pl.program_id(0)
    gpre = gpre_ref[...]
    h = _rms(x_ref[...], gpre).astype(BF16)
    hh = _rms(halo_ref[...], gpre).astype(BF16)

    def proj(hv, c0, width):
        return _dot(hv, w_ref[:, c0:c0 + width])

    cu = proj(h, C_CG, CONV_DIM) * proj(h, C_U, CONV_DIM)
    cu_h = proj(hh, C_CG, CONV_DIM) * proj(hh, C_U, CONV_DIM)
    first = (i % tiles_per_batch) == 0
    cu_scr[0:HALO, :] = jnp.where(first, 0.0, cu_h)
    cu_scr[HALO:HALO + tm, :] = cu
    cw = convw_ref[...]
    conv = (cu * cw[2:3, :]
            + cu_scr[HALO - 1:HALO - 1 + tm, :] * cw[1:2, :]
            + cu_scr[HALO - 2:HALO - 2 + tm, :] * cw[0:1, :])
    y_conv = (proj(h, C_BG, CONV_DIM) * conv).astype(BF16)

    ang = pos_ref[...].astype(F32) * tab_ref[0:1, :]
    cos_t = jnp.cos(ang)
    sin_t = jnp.sin(ang)
    sin_a = sin_t * tab_ref[1:2, :]
    sin_b = sin_t * tab_ref[2:3, :]

    def rope(t):
        half = ROT_DIM // 2
        return (t * cos_t + pltpu.roll(t, LANES - half, 1) * sin_a
                + pltpu.roll(t, half, 1) * sin_b)

    wide_starts = (C_Q, C_KSLC, C_QMEM, C_MERGE)
    wide = [proj(h, a, b - a) for a, b in zip(wide_starts[:-1], wide_starts[1:])]

    def nsa_cols(c0):
        k = max(j for j, a in enumerate(wide_starts[:-1]) if a <= c0)
        return wide[k][:, c0 - wide_starts[k]:c0 - wide_starts[k] + LANES]

    scale = HEAD_DIM ** -0.5 * LOG2_E
    for hd in range(NSA_HEADS):
        q = nsa_cols(C_Q + hd * LANES) * scale
        sl = slice(hd * LANES, (hd + 1) * LANES)
        for j in range(tm // ATT_T):
            rows = slice(j * ATT_T, (j + 1) * ATT_T)
            qpt_ref[0, sl, rows] = q[rows].T.astype(BF16)
            qrt_ref[0, sl, rows] = rope(q)[rows].T.astype(BF16)
    one_hot = tab_ref[3:4, :]
    seq_row = (i % tiles_per_batch) * tm + lax.broadcasted_iota(jnp.int32, (tm, LANES), 0)
    lane = lax.broadcasted_iota(jnp.int32, (tm, LANES), 1)
    blk_hot = jnp.where(lane - HEAD_DIM == (seq_row // SLC_BLOCK) % BIAS_ROWS, 1.0, 0.0)
    for g in range(NSA_GROUPS):
        sl = slice(g * LANES, (g + 1) * LANES)
        kslc_ref[:, sl] = (rope(nsa_cols(C_KSLC + g * LANES)) + blk_hot).astype(BF16)
        kwin_ref[:, sl] = rope(nsa_cols(C_KWIN + g * LANES)).astype(BF16)
        vs = nsa_cols(C_VSLC + g * LANES) + one_hot
        vw = nsa_cols(C_VWIN + g * LANES) + one_hot
        gt = jax.nn.sigmoid(nsa_cols(C_GATE + g * LANES))
        for j in range(tm // ATT_T):
            rows = slice(j * ATT_T, (j + 1) * ATT_T)
            vslct_ref[0, j, g] = vs[rows].T[0:VT_ROWS].astype(BF16)
            vwint_ref[0, j, g] = vw[rows].T[0:VT_ROWS].astype(BF16)
            gatest_ref[0, sl, rows] = gt[rows].T
    kcr_ref[...] = nsa_cols(C_KCR)
    vcr_ref[...] = nsa_cols(C_VCR)

    mem_scale = MEM_HEAD_DIM ** -0.5
    outs = []
    for hd in range(MEM_HEADS):
        sl = slice(hd * MEM_HEAD_DIM, (hd + 1) * MEM_HEAD_DIM)
        qm = (nsa_cols(C_QMEM + hd * MEM_HEAD_DIM) * mem_scale).astype(BF16)
        s = lax.dot_general(qm, mk_ref[0, :, sl], (((1,), (1,)), ((), ())),
                            preferred_element_type=F32)
        p = jnp.exp(s - jnp.max(s, axis=-1, keepdims=True))
        l = jnp.sum(p, axis=-1, keepdims=True)
        outs.append((_dot(p.astype(BF16), mv_ref[0, :, sl]) / l).astype(BF16))
    y_mem = jnp.concatenate(outs, axis=1)

    g_conv = jax.nn.sigmoid(proj(h, C_MERGE, D_MODEL))
    g_nsa = jax.nn.sigmoid(proj(h, C_MERGE + D_MODEL, D_MODEL))
    g_mem = jax.nn.sigmoid(proj(h, C_MERGE + 2 * D_MODEL, D_MODEL))
    part_ref[...] = g_conv * _dot(y_conv, wbc_ref[...]) + g_mem * _dot(y_mem, wbm_ref[...])
    gnsa_ref[...] = g_nsa.astype(BF16)


def _mixin(x, pos, gpre, tab, w_main, convw, mk, mv, wbc, wbm, batch, seq):
    t = batch * seq
    tm = MIX_TM
    tpb = seq // tm
    nt = tm // ATT_T
    row = lambda i: (i, 0)
    fix = lambda i: (0, 0)
    bt = lambda i: (i // tpb, 0, i % tpb)
    mem_len = mk.shape[1]
    half = MEM_HEADS * MEM_HEAD_DIM
    out_shape = [
        jax.ShapeDtypeStruct((t, D_MODEL), F32),
        jax.ShapeDtypeStruct((t, D_MODEL), BF16),
        jax.ShapeDtypeStruct((batch, NSA_HEADS * LANES, seq), BF16),
        jax.ShapeDtypeStruct((batch, NSA_HEADS * LANES, seq), BF16),
        jax.ShapeDtypeStruct((t, NSA_GROUPS * LANES), BF16),
        jax.ShapeDtypeStruct((t, NSA_GROUPS * LANES), BF16),
        jax.ShapeDtypeStruct((batch, seq // ATT_T, NSA_GROUPS, VT_ROWS, ATT_T), BF16),
        jax.ShapeDtypeStruct((batch, seq // ATT_T, NSA_GROUPS, VT_ROWS, ATT_T), BF16),
        jax.ShapeDtypeStruct((t, LANES), F32),
        jax.ShapeDtypeStruct((t, LANES), F32),
        jax.ShapeDtypeStruct((batch, NSA_GROUPS * LANES, seq), F32),
    ]
    vt_spec = pl.BlockSpec((1, nt, NSA_GROUPS, VT_ROWS, ATT_T), lambda i: (i // tpb, i % tpb, 0, 0, 0))
    out_specs = [
        pl.BlockSpec((tm, D_MODEL), row),
        pl.BlockSpec((tm, D_MODEL), row),
        pl.BlockSpec((1, NSA_HEADS * LANES, tm), bt),
        pl.BlockSpec((1, NSA_HEADS * LANES, tm), bt),
        pl.BlockSpec((tm, NSA_GROUPS * LANES), row),
        pl.BlockSpec((tm, NSA_GROUPS * LANES), row),
        vt_spec, vt_spec,
        pl.BlockSpec((tm, LANES), row),
        pl.BlockSpec((tm, LANES), row),
        pl.BlockSpec((1, NSA_GROUPS * LANES, tm), bt),
    ]
    in_specs = [
        pl.BlockSpec((tm, D_MODEL), row),
        pl.BlockSpec((HALO, D_MODEL), lambda i: (jnp.maximum(i * (tm // HALO) - 1, 0), 0)),
        pl.BlockSpec((tm, 1), row),
        _resident((1, D_MODEL), fix),
        _resident((8, LANES), fix),
        _resident((D_MODEL, C_TOTAL), fix),
        _resident((8, CONV_DIM), fix),
        pl.BlockSpec((1, mem_len, half), lambda i: (i // tpb, 0, 0)),
        pl.BlockSpec((1, mem_len, half), lambda i: (i // tpb, 0, 0)),
        _resident((CONV_DIM, D_MODEL), fix),
        _resident((half, D_MODEL), fix),
    ]
    return pl.pallas_call(
        functools.partial(_mix_body, tpb),
        grid=(t // tm,),
        in_specs=in_specs,
        out_specs=out_specs,
        out_shape=out_shape,
        scratch_shapes=[pltpu.VMEM((HALO + tm, CONV_DIM), F32)],
        compiler_params=_params(1),
        name="mixin",
    )(x, x, pos, gpre, tab, w_main, convw, mk, mv, wbc, wbm)


def _cmp_body(transpose_out, r_ref, pe_ref, w1a_ref, w1b_ref, b1_ref, w2_ref, one_ref, o_ref):
    r = r_ref[0]
    n = r.shape[0]
    a = _dot((r + pe_ref[0:1, :]).astype(BF16), w1a_ref[...])
    b = _dot((r + pe_ref[1:2, :]).astype(BF16), w1b_ref[...])
    hid = a + pltpu.roll(b, n - 1, 0) + b1_ref[...]
    o = _dot(jax.nn.gelu(hid).astype(BF16), w2_ref[...])
    if transpose_out:
        o = o + one_ref[...]
        o_ref[0] = jnp.concatenate(
            [o[:, g * LANES:(g + 1) * LANES].T for g in range(NSA_GROUPS)], axis=0).astype(BF16)
    else:
        o_ref[0] = o.astype(BF16)


def _compress(r, pe, w1a, w1b, b1, w2, one, transpose_out):
    b, n, width = r.shape
    fix = lambda i: (0, 0)
    oshape = (b, NSA_GROUPS * LANES, n) if transpose_out else (b, n, NSA_GROUPS * LANES)
    return pl.pallas_call(
        functools.partial(_cmp_body, transpose_out),
        grid=(b,),
        in_specs=[pl.BlockSpec((1, n, width), lambda i: (i, 0, 0)),
                  pl.BlockSpec((8, width), fix),
                  pl.BlockSpec(w1a.shape, fix), pl.BlockSpec(w1b.shape, fix),
                  pl.BlockSpec(b1.shape, fix), pl.BlockSpec(w2.shape, fix),
                  pl.BlockSpec(one.shape, fix)],
        out_specs=pl.BlockSpec((1,) + oshape[1:], lambda i: (i, 0, 0)),
        out_shape=jax.ShapeDtypeStruct(oshape, BF16),
        compiler_params=_params(1),
        name="compress_v" if transpose_out else "compress_k",
    )(r, pe, w1a, w1b, b1, w2, one)


def _nsa_body(qpt_ref, qrt_ref, gt_ref, kc_ref, vct_ref,
              kslc_ref, vslct_ref, kwin_ref, vwint_ref,
              y_ref,
              acc_ref, m_ref, selb_ref, win_ref, wmax_ref, e_ref, ga_ref, gb_ref, gmax_ref, yacc_ref):
    tq = ATT_T
    hg = HEADS_PER_GROUP
    i = pl.program_id(2)
    start = i * tq
    t_row = start + lax.broadcasted_iota(jnp.int32, (1, tq), 1)

    def lane_cat(ref):
        return jnp.concatenate([ref[0, h * LANES:(h + 1) * LANES, :] for h in range(hg)], axis=1)

    qp4 = lane_cat(qpt_ref)
    qr4 = lane_cat(qrt_ref)

    any_c = t_row >= CMP_BLOCK - 1
    per_slc = SLC_BLOCK // CMP_STRIDE
    n_half = tq // LANES
    cur = t_row // SLC_BLOCK
    e_ref[:, 0:E_PAD, :] = jnp.zeros((n_half, E_PAD, LANES), F32)

    k_iota = lax.broadcasted_iota(jnp.int32, (tq, 1), 0)

    def key_rows(ref, kt):
        return ref[0, pl.ds(pl.multiple_of(kt * tq, tq), tq), :]

    win_kts = [jnp.maximum(i - back, 0) for back in range(WIN_TILES, -1, -1)]

    def win_bias(w):
        back = WIN_TILES - w
        kpos = win_kts[w] * tq + k_iota
        if back == WIN_TILES:
            ok = (kpos > t_row - WINDOW) & (i >= back)
        elif back == 0:
            ok = kpos <= t_row
        else:
            ok = jnp.broadcast_to(i >= back, (tq, tq))
        return jnp.where(ok, 0.0, NEG)

    def score_window():
        wmax = jnp.full((1, hg * tq), NEG, F32)
        for w in range(WIN_TILES + 1):
            st = _dot(key_rows(kwin_ref, win_kts[w]), qr4) + jnp.concatenate([win_bias(w)] * hg, axis=1)
            win_ref[w * tq:(w + 1) * tq, :] = st
            wmax = jnp.maximum(wmax, jnp.max(st, axis=0, keepdims=True))
        wmax_ref[0:1, :] = wmax

    def window_attention():
        wmax = wmax_ref[0:1, :]
        part = jnp.zeros(acc_ref.shape, F32)
        for w in range(WIN_TILES + 1):
            rows = slice(w * tq, (w + 1) * tq)
            p = jnp.concatenate(
                [jnp.exp2(win_ref[rows, h * tq:(h + 1) * tq] - wmax[:, h * tq:(h + 1) * tq]).astype(BF16)
                 for h in range(hg)], axis=1)
            part = part + _dot(vwint_ref[0, win_kts[w]], p)
        finish(2, part)

    def cmp_and_select(n_chunk):
        rows = n_chunk * CMP_CHUNK
        nblk = rows // per_slc
        n_idx = lax.broadcasted_iota(jnp.int32, (rows, tq), 0)
        bias_c = jnp.where((n_idx * CMP_STRIDE + (CMP_BLOCK - 1)) <= t_row, 0.0, NEG)
        m4 = jnp.full((1, hg * tq), NEG, F32)
        for c in range(n_chunk):
            rs = slice(c * CMP_CHUNK, (c + 1) * CMP_CHUNK)
            st = _dot(kc_ref[0, rs, :], qp4)
            gb_ref[rs, :] = st
            m4 = jnp.maximum(m4, jnp.max(st + jnp.concatenate([bias_c[rs]] * hg, axis=1),
                                         axis=0, keepdims=True))
        imp = jnp.zeros((nblk, tq), F32)
        for h in range(hg):
            hs = slice(h * tq, (h + 1) * tq)
            e = jnp.exp2(gb_ref[0:rows, hs] + bias_c - m4[:, hs])
            for c in range(n_half):
                e_ref[c, E_PAD:E_PAD + rows, :] = e[:, c * LANES:(c + 1) * LANES]
            oc = _dot(vct_ref[0, :, 0:rows], e.astype(BF16))
            rl = jnp.where(any_c, 1.0 / jnp.maximum(oc[HEAD_DIM:HEAD_DIM + 1, :], 1e-30), 0.0)
            yacc_ref[h * HEAD_DIM:(h + 1) * HEAD_DIM, :] = oc[0:HEAD_DIM, :] * (rl * gt_ref[0, 3 * h:3 * h + 1, :])
            tap = lambda d: jnp.concatenate(
                [e_ref[c, pl.ds(E_PAD + d, nblk, stride=per_slc), :] for c in range(n_half)], axis=1)
            imp_h = 0.5 * (tap(-1) + tap(per_slc - 1))
            for d in range(per_slc - 1):
                imp_h = imp_h + tap(d)
            imp = imp + imp_h * rl

        score_window()

        blk = lax.broadcasted_iota(jnp.int32, (nblk, tq), 0)
        blk_f = blk.astype(F32)
        forced = (blk == 0) | (blk == cur) | (blk == cur - 1)
        v = jnp.where(blk > cur, NEG, jnp.where(forced, REMOVED, imp))
        for _ in range(N_SELECT - N_FORCED):
            mx = jnp.max(v, axis=0, keepdims=True)
            first = jnp.min(jnp.where(v == mx, blk_f, float(nblk)), axis=0, keepdims=True)
            v = jnp.where(blk_f == first, REMOVED, v)
        selb_ref[0:nblk, :] = jnp.where((v == REMOVED) & (blk <= cur), 0.0, NEG)

    n_chunks_total = kc_ref.shape[1] // CMP_CHUNK
    visible = (start + tq - CMP_BLOCK) // (CMP_STRIDE * CMP_CHUNK) + 1
    for n_chunk in range(1, n_chunks_total + 1):
        pl.when(visible == n_chunk)(functools.partial(cmp_and_select, n_chunk))

    def reset():
        acc_ref[...] = jnp.zeros_like(acc_ref)
        m_ref[...] = jnp.full_like(m_ref, NEG)

    def soft_pv(scores, vt, bias=None):
        ps, alphas = [], []
        for h in range(hg):
            hs = slice(h * tq, (h + 1) * tq)
            load = (lambda: scores(hs)) if bias is None else (lambda: scores(hs) + bias)
            m_old = m_ref[0:1, hs]
            m_new = jnp.maximum(m_old, jnp.max(load(), axis=0, keepdims=True))
            alphas.append(jnp.exp2(m_old - m_new))
            ps.append(jnp.exp2(load() - m_new).astype(BF16))
            m_ref[0:1, hs] = m_new
        alpha = jnp.concatenate(alphas, axis=1)
        acc_ref[...] = acc_ref[...] * alpha + _dot(vt, jnp.concatenate(ps, axis=1))

    def finish(branch, acc):
        for h in range(hg):
            hs = slice(h * tq, (h + 1) * tq)
            o = acc[0:HEAD_DIM, hs] * (1.0 / acc[HEAD_DIM:HEAD_DIM + 1, hs])
            yacc_ref[h * HEAD_DIM:(h + 1) * HEAD_DIM, :] += o * gt_ref[0, 3 * h + branch:3 * h + branch + 1, :]

    def q_biased(grp):
        band = selb_ref[pl.ds(pl.multiple_of(grp * BIAS_ROWS, BIAS_ROWS), BIAS_ROWS), :].astype(BF16)
        return jnp.concatenate([qr4[0:HEAD_DIM], jnp.concatenate([band] * hg, axis=1),
                                qr4[HEAD_DIM + BIAS_ROWS:]], axis=0)

    def score_tile(grp, j, q4, dst_ref, gmax):
        st = _dot(key_rows(kslc_ref, grp * KT_GROUP + j), q4)
        dst_ref[j * tq:(j + 1) * tq, :] = st
        return jnp.maximum(gmax, jnp.max(st, axis=0, keepdims=True))

    def slc_group(grp, src_ref, dst_ref):
        m_old = m_ref[0:1, :]
        m_new = jnp.maximum(m_old, gmax_ref[0:1, :])
        q_next = q_biased(grp + 1)
        gmax = jnp.full((1, hg * tq), NEG, F32)
        part = jnp.zeros(acc_ref.shape, F32)
        for j in range(KT_GROUP):
            rows = slice(j * tq, (j + 1) * tq)
            gmax = score_tile(grp + 1, j, q_next, dst_ref, gmax)
            p = jnp.concatenate(
                [jnp.exp2(src_ref[rows, h * tq:(h + 1) * tq] - m_new[:, h * tq:(h + 1) * tq]).astype(BF16)
                 for h in range(hg)], axis=1)
            part = part + _dot(vslct_ref[0, grp * KT_GROUP + j], p)
        acc_ref[...] = acc_ref[...] * jnp.exp2(m_old - m_new) + part
        m_ref[0:1, :] = m_new
        gmax_ref[0:1, :] = gmax

    def score_first(dst_ref):
        gmax = jnp.full((1, hg * tq), NEG, F32)
        for j in range(KT_GROUP):
            gmax = score_tile(0, j, q_biased(0), dst_ref, gmax)
        gmax_ref[0:1, :] = gmax

    reset()
    n_grp = i // KT_GROUP
    odd = n_grp % 2

    @pl.when(odd == 0)
    def _():
        score_first(ga_ref)
        window_attention()

    @pl.when(odd == 1)
    def _():
        score_first(gb_ref)
        window_attention()
        slc_group(0, gb_ref, ga_ref)

    def slc_pair(pair, carry):
        slc_group(2 * pair + odd, ga_ref, gb_ref)
        slc_group(2 * pair + odd + 1, gb_ref, ga_ref)
        return carry

    lax.fori_loop(0, n_grp // 2, slc_pair, 0)

    for j in range(KT_GROUP):
        kt = n_grp * KT_GROUP + j
        tile = lambda hs, j=j: ga_ref[j * tq:(j + 1) * tq, hs]

        @pl.when(kt < i)
        def _():
            soft_pv(tile, vslct_ref[0, kt])

        @pl.when(kt == i)
        def _():
            soft_pv(tile, vslct_ref[0, kt], jnp.where((start + k_iota) <= t_row, 0.0, NEG))
    finish(1, acc_ref)

    y_ref[0] = yacc_ref[...].T.astype(BF16)


def _nsa(qpt, qrt, gt, kc, vct, kslc, vslct, kwin, vwint):
    b, _, seq = qpt.shape
    tq = ATT_T
    hg = HEADS_PER_GROUP
    n_cmp = kc.shape[1]
    nkt = seq // tq
    qspec = pl.BlockSpec((1, hg * LANES, tq), lambda bi, g, i: (bi, g, i))
    in_specs = [
        qspec, qspec,
        pl.BlockSpec((1, LANES, tq), lambda bi, g, i: (bi, g, i)),
        pl.BlockSpec((1, n_cmp, LANES), lambda bi, g, i: (bi, 0, g)),
        pl.BlockSpec((1, LANES, n_cmp), lambda bi, g, i: (bi, g, 0)),
        _resident((1, seq, LANES), lambda bi, g, i: (bi, 0, g)),
        _resident((1, nkt, None, VT_ROWS, tq), lambda bi, g, i: (bi, 0, g, 0, 0)),
        _resident((1, seq, LANES), lambda bi, g, i: (bi, 0, g)),
        _resident((1, nkt, None, VT_ROWS, tq), lambda bi, g, i: (bi, 0, g, 0, 0)),
    ]
    return pl.pallas_call(
        _nsa_body,
        grid=(b, NSA_GROUPS, seq // tq),
        in_specs=in_specs,
        out_specs=pl.BlockSpec((1, tq, hg * HEAD_DIM), lambda bi, g, i: (bi, i, g)),
        out_shape=jax.ShapeDtypeStruct((b, seq, NSA_HEADS * HEAD_DIM), BF16),
        scratch_shapes=[pltpu.VMEM((VT_ROWS, hg * tq), F32),
                        pltpu.VMEM((8, hg * tq), F32),
                        pltpu.VMEM((seq // SLC_BLOCK, tq), F32),
                        pltpu.VMEM(((WIN_TILES + 1) * tq, hg * tq), F32),
                        pltpu.VMEM((8, hg * tq), F32),
                        pltpu.VMEM((tq // LANES, E_PAD + n_cmp, LANES), F32),
                        pltpu.VMEM((KT_GROUP * tq, hg * tq), F32),
                        pltpu.VMEM((KT_GROUP * tq, hg * tq), F32),
                        pltpu.VMEM((8, hg * tq), F32),
                        pltpu.VMEM((hg * HEAD_DIM, tq), F32)],
        compiler_params=_params(3),
        name="nsa",
    )(qpt, qrt, gt, kc, vct, kslc, vslct, kwin, vwint)


def _merge_ffn_body(x_ref, part_ref, gnsa_ref, ynsa_ref, wbn_ref, wo_ref, gmix_ref,
                    gpre_ref, gpost_ref, wa_ref, wb_ref, wout_ref, o_ref):
    merged = part_ref[...] + gnsa_ref[...].astype(F32) * _dot(ynsa_ref[...], wbn_ref[...])
    x = x_ref[...] + _rms(_dot(merged.astype(BF16), wo_ref[...]), gmix_ref[...])
    h = _rms(x, gpre_ref[...]).astype(BF16)
    y = jnp.zeros(x.shape, F32)
    for c0, c1 in zip(MERGE_FF_SPLITS[:-1], MERGE_FF_SPLITS[1:]):
        cs = slice(c0, c1)
        a = _dot(h, wa_ref[:, cs])
        g = (a * jax.nn.sigmoid(a) * _dot(h, wb_ref[:, cs])).astype(BF16)
        y = y + _dot(g, wout_ref[cs, :])
    o_ref[...] = x + 0.5 * _rms(y, gpost_ref[...])


def _merge_ffn(x, part, gnsa, ynsa, wbn, wo, gmix, gpre, gpost, wa, wb, wout):
    t = x.shape[0]
    tm = MERGE_TM
    row = lambda i: (i, 0)
    fix = lambda i: (0, 0)
    vec = _resident((1, D_MODEL), fix)
    return pl.pallas_call(
        _merge_ffn_body,
        grid=(t // tm,),
        in_specs=[pl.BlockSpec((tm, D_MODEL), row), pl.BlockSpec((tm, D_MODEL), row),
                  pl.BlockSpec((tm, D_MODEL), row), pl.BlockSpec((tm, NSA_HEADS * HEAD_DIM), row),
                  _resident(wbn.shape, fix), _resident(wo.shape, fix), vec, vec, vec,
                  _resident(wa.shape, fix), _resident(wb.shape, fix), _resident(wout.shape, fix)],
        out_specs=pl.BlockSpec((tm, D_MODEL), row),
        out_shape=jax.ShapeDtypeStruct((t, D_MODEL), F32),
        compiler_params=_params(1),
        name="merge_ffn",
    )(x, part, gnsa, ynsa, wbn, wo, gmix, gpre, gpost, wa, wb, wout)


def _pad_lanes(w, groups, width):
    d = w.shape[0]
    w = w.reshape(d, groups, width)
    return jnp.pad(w, ((0, 0), (0, 0), (0, LANES - width))).reshape(d, groups * LANES)


def _mix_weight(w):
    o = 0
    u = w[:, o:o + CONV_DIM]; o += CONV_DIM
    bg = w[:, o:o + CONV_DIM]; o += CONV_DIM
    cg = w[:, o:o + CONV_DIM]; o += CONV_DIM
    q = w[:, o:o + NSA_HEADS * HEAD_DIM]; o += NSA_HEADS * HEAD_DIM
    kv = w[:, o:o + 6 * NSA_GROUPS * HEAD_DIM].reshape(D_MODEL, 6, NSA_GROUPS * HEAD_DIM)
    o += 6 * NSA_GROUPS * HEAD_DIM
    gate = w[:, o:o + 3 * NSA_HEADS]; o += 3 * NSA_HEADS
    qmem = w[:, o:o + MEM_HEADS * MEM_HEAD_DIM]; o += MEM_HEADS * MEM_HEAD_DIM
    merge = w[:, o:]
    slab = lambda c: _pad_lanes(kv[:, c], NSA_GROUPS, HEAD_DIM)
    cols = [u, bg, cg, _pad_lanes(q, NSA_HEADS, HEAD_DIM),
            slab(2), slab(4), slab(3), slab(5), kv[:, 0], kv[:, 1],
            _pad_lanes(gate, NSA_GROUPS, 3 * HEADS_PER_GROUP), qmem, merge]
    out = jnp.concatenate(cols, axis=1).astype(BF16)
    assert out.shape == (D_MODEL, C_TOTAL)
    return out


def _cmp_weights(pos_emb, w1, b1, w2):
    eye = jnp.eye(NSA_GROUPS, dtype=F32)
    w1r = w1.reshape(CMP_BLOCK, HEAD_DIM, CMP_HIDDEN)
    width = CMP_STRIDE * NSA_GROUPS * HEAD_DIM

    def spread(a):
        return jnp.einsum('pdj,gh->pgdhj', a, eye).reshape(width, NSA_GROUPS * CMP_HIDDEN).astype(BF16)

    def pe_row(p):
        return jnp.broadcast_to(p[:, None, :], (CMP_STRIDE, NSA_GROUPS, HEAD_DIM)).reshape(1, width)

    pe = jnp.concatenate([pe_row(pos_emb[:CMP_STRIDE]), pe_row(pos_emb[CMP_STRIDE:]),
                          jnp.zeros((6, width), F32)], axis=0)
    w2p = jnp.pad(w2, ((0, 0), (0, LANES - HEAD_DIM)))
    w2b = jnp.einsum('jc,gh->gjhc', w2p, eye).reshape(NSA_GROUPS * CMP_HIDDEN, NSA_GROUPS * LANES)
    return (pe, spread(w1r[:CMP_STRIDE]), spread(w1r[CMP_STRIDE:]),
            jnp.tile(b1, NSA_GROUPS)[None, :], w2b.astype(BF16))


def _rope_table():
    inv_freq = ROPE_THETA ** (-jnp.arange(0, ROT_DIM, 2, dtype=jnp.float32) / ROT_DIM)
    half = ROT_DIM // 2
    lane = np.arange(LANES)
    tab = jnp.zeros((8, LANES), F32)
    tab = tab.at[0, :ROT_DIM].set(jnp.tile(inv_freq, 2))
    tab = tab.at[1].set(jnp.asarray(np.where(lane < half, -1.0, 0.0), F32))
    tab = tab.at[2].set(jnp.asarray(np.where((lane >= half) & (lane < ROT_DIM), 1.0, 0.0), F32))
    tab = tab.at[3].set(jnp.asarray(np.where(lane == HEAD_DIM, 1.0, 0.0), F32))
    return tab


def kernel(x, mem, positions, ffn1_norm_pre, ffn1_norm_post, ffn1_w_in, ffn1_w_out, mix_norm_pre, mix_norm_post, mem_norm, w_mix_in, conv_w, cmp_pos_k, cmp_pos_v, cmp_k_w1, cmp_k_b1, cmp_k_w2, cmp_v_w1, cmp_v_b1, cmp_v_w2, w_mem_kv, w_branch_conv, w_branch_nsa, w_branch_mem, w_mix_out, ffn2_norm_pre, ffn2_norm_post, ffn2_w_in, ffn2_w_out):
    batch, seq, _ = x.shape
    t = batch * seq
    n_chunks = seq // CMP_STRIDE
    xf = x.reshape(t, D_MODEL)
    pos = positions.reshape(t, 1)
    tab = _rope_table()
    one = jnp.tile(tab[3:4, :], (1, NSA_GROUPS))
    row = lambda g: g[None, :]

    for l in range(DEPTH):
        xf = _ffn(xf, row(ffn1_norm_pre[l]), row(ffn1_norm_post[l]),
                  ffn1_w_in[l][:, :D_FF].astype(BF16), ffn1_w_in[l][:, D_FF:].astype(BF16),
                  ffn1_w_out[l].astype(BF16))

        mk, mv = _memkv(mem, row(mem_norm[l]), w_mem_kv[l].astype(BF16))
        convw = jnp.pad(conv_w[l], ((0, 8 - CONV_WIDTH), (0, 0)))
        (part, gnsa, qpt, qrt, kslc, kwin, vslct, vwint, kcr, vcr, gt) = _mixin(
            xf, pos, row(mix_norm_pre[l]), tab, _mix_weight(w_mix_in[l]), convw, mk, mv,
            w_branch_conv[l].astype(BF16), w_branch_mem[l].astype(BF16), batch, seq)

        width = CMP_STRIDE * NSA_GROUPS * HEAD_DIM
        kc = _compress(kcr.reshape(batch, n_chunks, width),
                       *_cmp_weights(cmp_pos_k[l], cmp_k_w1[l], cmp_k_b1[l], cmp_k_w2[l]), one, False)
        vct = _compress(vcr.reshape(batch, n_chunks, width),
                        *_cmp_weights(cmp_pos_v[l], cmp_v_w1[l], cmp_v_b1[l], cmp_v_w2[l]), one, True)

        ynsa = _nsa(qpt, qrt, gt, kc, vct,
                    kslc.reshape(batch, seq, NSA_GROUPS * LANES), vslct,
                    kwin.reshape(batch, seq, NSA_GROUPS * LANES), vwint)

        xf = _merge_ffn(xf, part, gnsa, ynsa.reshape(t, NSA_HEADS * HEAD_DIM),
                        w_branch_nsa[l].astype(BF16), w_mix_out[l].astype(BF16), row(mix_norm_post[l]),
                        row(ffn2_norm_pre[l]), row(ffn2_norm_post[l]),
                        ffn2_w_in[l][:, :D_FF].astype(BF16), ffn2_w_in[l][:, D_FF:].astype(BF16),
                        ffn2_w_out[l].astype(BF16))
    return xf.reshape(batch, seq, D_MODEL)
```

```python
import functools

import jax
import jax.numpy as jnp
import numpy as np
from jax import lax
from jax.experimental import pallas as pl
from jax.experimental.pallas import tpu as pltpu

D_MODEL = 1024
DEPTH = 2
CONV_DIM = 512
CONV_WIDTH = 3
NSA_HEADS = 8
NSA_GROUPS = 2
HEADS_PER_GROUP = NSA_HEADS // NSA_GROUPS
HEAD_DIM = 64
ROT_DIM = HEAD_DIM // 4
ROPE_THETA = 500000.0
CMP_BLOCK = 32
CMP_STRIDE = 16
CMP_HIDDEN = 256
SLC_BLOCK = 64
N_SELECT = 16
WINDOW = 512
MEM_HEADS = 4
MEM_HEAD_DIM = 128
D_FF = 2816
EPS = 1e-6
NEG = -1e30
REMOVED = -3e38

LANES = 128
F32 = jnp.float32
BF16 = jnp.bfloat16

FFN_TM = 512
MIX_TM = 512
MERGE_TM = 512
MERGE_FF_SPLITS = (0, 1536, D_FF)
HALO = 8
ATT_T = 256
WIN_TILES = WINDOW // ATT_T
KT_GROUP = 4
BIAS_ROWS = KT_GROUP * ATT_T // SLC_BLOCK
VT_ROWS = HEAD_DIM + BIAS_ROWS
LOG2_E = 1.4426950408889634
CMP_CHUNK = 256
E_PAD = 8
N_FORCED = 3

C_U = 0
C_BG = C_U + CONV_DIM
C_CG = C_BG + CONV_DIM
C_Q = C_CG + CONV_DIM
C_KSLC = C_Q + NSA_HEADS * HEAD_DIM
C_KWIN = C_KSLC + LANES
C_VSLC = C_KWIN + LANES
C_VWIN = C_VSLC + LANES
C_KCR = C_VWIN + LANES
C_VCR = C_KCR + LANES
C_GATE = C_VCR + LANES
C_QMEM = C_GATE + LANES
C_MERGE = C_QMEM + MEM_HEADS * MEM_HEAD_DIM
C_TOTAL = C_MERGE + 3 * D_MODEL

VMEM_LIMIT = 56 * 1024 * 1024


def _params(n_axes):
    return pltpu.CompilerParams(dimension_semantics=("arbitrary",) * n_axes,
                                vmem_limit_bytes=VMEM_LIMIT)


def _resident(shape, index_map):
    return pl.BlockSpec(shape, index_map, pipeline_mode=pl.Buffered(1))


def _rms(x, g):
    return x * lax.rsqrt(jnp.mean(x * x, axis=-1, keepdims=True) + EPS) * g


def _dot(a, b):
    return jnp.dot(a, b, preferred_element_type=F32)


def _ffn_body(x_ref, gpre_ref, gpost_ref, wa_ref, wb_ref, wo_ref, o_ref):
    x = x_ref[...]
    h = _rms(x, gpre_ref[...]).astype(BF16)
    a = _dot(h, wa_ref[...])
    b = _dot(h, wb_ref[...])
    g = (a * jax.nn.sigmoid(a) * b).astype(BF16)
    y = _dot(g, wo_ref[...])
    o_ref[...] = x + 0.5 * _rms(y, gpost_ref[...])


def _ffn(x, gpre, gpost, wa, wb, wo):
    t = x.shape[0]
    row = lambda i: (i, 0)
    fix = lambda i: (0, 0)
    return pl.pallas_call(
        _ffn_body,
        grid=(t // FFN_TM,),
        in_specs=[pl.BlockSpec((FFN_TM, D_MODEL), row),
                  _resident((1, D_MODEL), fix), _resident((1, D_MODEL), fix),
                  _resident((D_MODEL, D_FF), fix), _resident((D_MODEL, D_FF), fix),
                  _resident((D_FF, D_MODEL), fix)],
        out_specs=pl.BlockSpec((FFN_TM, D_MODEL), row),
        out_shape=jax.ShapeDtypeStruct((t, D_MODEL), F32),
        compiler_params=_params(1),
        name="ffn",
    )(x, gpre, gpost, wa, wb, wo)


def _memkv_body(mem_ref, g_ref, w_ref, mk_ref, mv_ref):
    h = _rms(mem_ref[0], g_ref[...]).astype(BF16)
    kv = _dot(h, w_ref[...])
    half = MEM_HEADS * MEM_HEAD_DIM
    mk_ref[0] = kv[:, :half].astype(BF16)
    mv_ref[0] = kv[:, half:].astype(BF16)


def _memkv(mem, g, w):
    b, m, _ = mem.shape
    half = MEM_HEADS * MEM_HEAD_DIM
    out = jax.ShapeDtypeStruct((b, m, half), BF16)
    return pl.pallas_call(
        _memkv_body,
        grid=(b,),
        in_specs=[pl.BlockSpec((1, m, D_MODEL), lambda i: (i, 0, 0)),
                  pl.BlockSpec((1, D_MODEL), lambda i: (0, 0)),
                  pl.BlockSpec((D_MODEL, 2 * half), lambda i: (0, 0))],
        out_specs=[pl.BlockSpec((1, m, half), lambda i: (i, 0, 0))] * 2,
        out_shape=[out, out],
        compiler_params=_params(1),
        name="memkv",
    )(mem, g, w)


def _mix_body(tiles_per_batch,
              x_ref, halo_ref, pos_ref, gpre_ref, tab_ref, w_ref, convw_ref, mk_ref, mv_ref,
              wbc_ref, wbm_ref,
              part_ref, gnsa_ref, qpt_ref, qrt_ref, kslc_ref, kwin_ref, vslct_ref, vwint_ref,
              kcr_ref, vcr_ref, gatest_ref,
              cu_scr):
    tm = MIX_TM
    i = pl.program_id(0)
    gpre = gpre_ref[...]
    h = _rms(x_ref[...], gpre).astype(BF16)
    hh = _rms(halo_ref[...], gpre).astype(BF16)

    def proj(hv, c0, width):
        return _dot(hv, w_ref[:, c0:c0 + width])

    cu = proj(h, C_CG, CONV_DIM) * proj(h, C_U, CONV_DIM)
    cu_h = proj(hh, C_CG, CONV_DIM) * proj(hh, C_U, CONV_DIM)
    first = (i % tiles_per_batch) == 0
    cu_scr[0:HALO, :] = jnp.where(first, 0.0, cu_h)
    cu_scr[HALO:HALO + tm, :] = cu
    cw = convw_ref[...]
    conv = (cu * cw[2:3, :]
            + cu_scr[HALO - 1:HALO - 1 + tm, :] * cw[1:2, :]
            + cu_scr[HALO - 2:HALO - 2 + tm, :] * cw[0:1, :])
    y_conv = (proj(h, C_BG, CONV_DIM) * conv).astype(BF16)

    ang = pos_ref[...].astype(F32) * tab_ref[0:1, :]
    cos_t = jnp.cos(ang)
    sin_t = jnp.sin(ang)
    sin_a = sin_t * tab_ref[1:2, :]
    sin_b = sin_t * tab_ref[2:3, :]

    def rope(t):
        half = ROT_DIM // 2
        return (t * cos_t + pltpu.roll(t, LANES - half, 1) * sin_a
                + pltpu.roll(t, half, 1) * sin_b)

    wide = proj(h, C_Q, C_MERGE - C_Q)

    def nsa_cols(c0):
        return wide[:, c0 - C_Q:c0 - C_Q + LANES]

    q_tiles = [slice(j * ATT_T, (j + 1) * ATT_T) for j in range(tm // ATT_T)]
    scale = HEAD_DIM ** -0.5 * LOG2_E
    for pair in range(NSA_HEADS // 2):
        q = nsa_cols(C_Q + pair * LANES) * scale
        sl = slice(pair * LANES, (pair + 1) * LANES)
        for rows in q_tiles:
            qpt_ref[0, sl, rows] = q[rows].T.astype(BF16)
            qrt_ref[0, sl, rows] = rope(q)[rows].T.astype(BF16)

    seq_row = (i % tiles_per_batch) * tm + lax.broadcasted_iota(jnp.int32, (tm, LANES), 0)
    lane = lax.broadcasted_iota(jnp.int32, (tm, LANES), 1)
    blk_hot = jnp.where(lane - HEAD_DIM == (seq_row // SLC_BLOCK) % BIAS_ROWS, 1.0, 0.0)

    def group_slabs(x):
        return [jnp.where(lane < HEAD_DIM, y, 0.0) for y in (x, pltpu.roll(x, HEAD_DIM, 1))]

    for g, (ks, kw) in enumerate(zip(group_slabs(rope(nsa_cols(C_KSLC))), group_slabs(rope(nsa_cols(C_KWIN))))):
        sl = slice(g * LANES, (g + 1) * LANES)
        kslc_ref[:, sl] = (ks + blk_hot).astype(BF16)
        kwin_ref[:, sl] = kw.astype(BF16)

    ones_rows = jnp.where(lax.broadcasted_iota(jnp.int32, (VT_ROWS - HEAD_DIM, ATT_T), 0) == 0, 1.0, 0.0)
    vs, vw = nsa_cols(C_VSLC), nsa_cols(C_VWIN)
    gt = jax.nn.sigmoid(nsa_cols(C_GATE))
    for j, rows in enumerate(q_tiles):
        vs_t, vw_t = vs[rows].T, vw[rows].T
        for g in range(NSA_GROUPS):
            band = slice(g * HEAD_DIM, (g + 1) * HEAD_DIM)
            vslct_ref[0, j, g] = jnp.concatenate([vs_t[band], ones_rows], axis=0).astype(BF16)
            vwint_ref[0, j, g] = jnp.concatenate([vw_t[band], ones_rows], axis=0).astype(BF16)
        gatest_ref[0, :, rows] = gt[rows].T
    kcr_ref[...] = nsa_cols(C_KCR)
    vcr_ref[...] = nsa_cols(C_VCR)

    mem_scale = MEM_HEAD_DIM ** -0.5
    outs = []
    for hd in range(MEM_HEADS):
        sl = slice(hd * MEM_HEAD_DIM, (hd + 1) * MEM_HEAD_DIM)
        qm = (nsa_cols(C_QMEM + hd * MEM_HEAD_DIM) * mem_scale).astype(BF16)
        s = lax.dot_general(qm, mk_ref[0, :, sl], (((1,), (1,)), ((), ())),
                            preferred_element_type=F32)
        p = jnp.exp(s - jnp.max(s, axis=-1, keepdims=True))
        l = jnp.sum(p, axis=-1, keepdims=True)
        outs.append((_dot(p.astype(BF16), mv_ref[0, :, sl]) / l).astype(BF16))
    y_mem = jnp.concatenate(outs, axis=1)

    g_conv = jax.nn.sigmoid(proj(h, C_MERGE, D_MODEL))
    g_nsa = jax.nn.sigmoid(proj(h, C_MERGE + D_MODEL, D_MODEL))
    g_mem = jax.nn.sigmoid(proj(h, C_MERGE + 2 * D_MODEL, D_MODEL))
    part_ref[...] = g_conv * _dot(y_conv, wbc_ref[...]) + g_mem * _dot(y_mem, wbm_ref[...])
    gnsa_ref[...] = g_nsa.astype(BF16)


def _mixin(x, pos, gpre, tab, w_main, convw, mk, mv, wbc, wbm, batch, seq):
    t = batch * seq
    tm = MIX_TM
    tpb = seq // tm
    nt = tm // ATT_T
    row = lambda i: (i, 0)
    fix = lambda i: (0, 0)
    bt = lambda i: (i // tpb, 0, i % tpb)
    mem_len = mk.shape[1]
    half = MEM_HEADS * MEM_HEAD_DIM
    out_shape = [
        jax.ShapeDtypeStruct((t, D_MODEL), F32),
        jax.ShapeDtypeStruct((t, D_MODEL), BF16),
        jax.ShapeDtypeStruct((batch, NSA_HEADS * HEAD_DIM, seq), BF16),
        jax.ShapeDtypeStruct((batch, NSA_HEADS * HEAD_DIM, seq), BF16),
        jax.ShapeDtypeStruct((t, NSA_GROUPS * LANES), BF16),
        jax.ShapeDtypeStruct((t, NSA_GROUPS * LANES), BF16),
        jax.ShapeDtypeStruct((batch, seq // ATT_T, NSA_GROUPS, VT_ROWS, ATT_T), BF16),
        jax.ShapeDtypeStruct((batch, seq // ATT_T, NSA_GROUPS, VT_ROWS, ATT_T), BF16),
        jax.ShapeDtypeStruct((t, LANES), F32),
        jax.ShapeDtypeStruct((t, LANES), F32),
        jax.ShapeDtypeStruct((batch, LANES, seq), F32),
    ]
    vt_spec = pl.BlockSpec((1, nt, NSA_GROUPS, VT_ROWS, ATT_T), lambda i: (i // tpb, i % tpb, 0, 0, 0))
    out_specs = [
        pl.BlockSpec((tm, D_MODEL), row),
        pl.BlockSpec((tm, D_MODEL), row),
        pl.BlockSpec((1, NSA_HEADS * HEAD_DIM, tm), bt),
        pl.BlockSpec((1, NSA_HEADS * HEAD_DIM, tm), bt),
        pl.BlockSpec((tm, NSA_GROUPS * LANES), row),
        pl.BlockSpec((tm, NSA_GROUPS * LANES), row),
        vt_spec, vt_spec,
        pl.BlockSpec((tm, LANES), row),
        pl.BlockSpec((tm, LANES), row),
        pl.BlockSpec((1, LANES, tm), bt),
    ]
    in_specs = [
        pl.BlockSpec((tm, D_MODEL), row),
        pl.BlockSpec((HALO, D_MODEL), lambda i: (jnp.maximum(i * (tm // HALO) - 1, 0), 0)),
        pl.BlockSpec((tm, 1), row),
        _resident((1, D_MODEL), fix),
        _resident((8, LANES), fix),
        _resident((D_MODEL, C_TOTAL), fix),
        _resident((8, CONV_DIM), fix),
        pl.BlockSpec((1, mem_len, half), lambda i: (i // tpb, 0, 0)),
        pl.BlockSpec((1, mem_len, half), lambda i: (i // tpb, 0, 0)),
        _resident((CONV_DIM, D_MODEL), fix),
        _resident((half, D_MODEL), fix),
    ]
    return pl.pallas_call(
        functools.partial(_mix_body, tpb),
        grid=(t // tm,),
        in_specs=in_specs,
        out_specs=out_specs,
        out_shape=out_shape,
        scratch_shapes=[pltpu.VMEM((HALO + tm, CONV_DIM), F32)],
        compiler_params=_params(1),
        name="mixin",
    )(x, x, pos, gpre, tab, w_main, convw, mk, mv, wbc, wbm)


def _cmp_body(transpose_out, r_ref, pe_ref, w1a_ref, w1b_ref, b1_ref, w2_ref, one_ref, o_ref):
    n = r_ref.shape[1] // CMP_STRIDE
    r = jnp.concatenate([r_ref[0, pl.ds(p, n, stride=CMP_STRIDE), :] for p in range(CMP_STRIDE)], axis=1)
    a = _dot((r + pe_ref[0:1, :]).astype(BF16), w1a_ref[...])
    b = _dot((r + pe_ref[1:2, :]).astype(BF16), w1b_ref[...])
    hid = a + pltpu.roll(b, n - 1, 0) + b1_ref[...]
    o = _dot(jax.nn.gelu(hid).astype(BF16), w2_ref[...])
    if transpose_out:
        o = o + one_ref[...]
        o_ref[0] = jnp.concatenate(
            [o[:, g * LANES:(g + 1) * LANES].T for g in range(NSA_GROUPS)], axis=0).astype(BF16)
    else:
        o_ref[0] = o.astype(BF16)


def _compress(r, pe, w1a, w1b, b1, w2, one, transpose_out):
    b, seq, lanes = r.shape
    n = seq // CMP_STRIDE
    fix = lambda i: (0, 0)
    oshape = (b, NSA_GROUPS * LANES, n) if transpose_out else (b, n, NSA_GROUPS * LANES)
    return pl.pallas_call(
        functools.partial(_cmp_body, transpose_out),
        grid=(b,),
        in_specs=[pl.BlockSpec((1, seq, lanes), lambda i: (i, 0, 0)),
                  pl.BlockSpec(pe.shape, fix),
                  pl.BlockSpec(w1a.shape, fix), pl.BlockSpec(w1b.shape, fix),
                  pl.BlockSpec(b1.shape, fix), pl.BlockSpec(w2.shape, fix),
                  pl.BlockSpec(one.shape, fix)],
        out_specs=pl.BlockSpec((1,) + oshape[1:], lambda i: (i, 0, 0)),
        out_shape=jax.ShapeDtypeStruct(oshape, BF16),
        compiler_params=_params(1),
        name="compress_v" if transpose_out else "compress_k",
    )(r, pe, w1a, w1b, b1, w2, one)


def _nsa_body(qpt_ref, qrt_ref, gt_ref, kc_ref, vct_ref,
              kslc_ref, vslct_ref, kwin_ref, vwint_ref,
              y_ref,
              acc_ref, m_ref, selb_ref, win_ref, wmax_ref, e_ref, ga_ref, gb_ref, gmax_ref, yacc_ref):
    tq = ATT_T
    hg = HEADS_PER_GROUP
    i = pl.program_id(2)
    start = i * tq
    t_row = start + lax.broadcasted_iota(jnp.int32, (1, tq), 1)

    def lane_cat(ref):
        return jnp.concatenate([ref[0, h * HEAD_DIM:(h + 1) * HEAD_DIM, :] for h in range(hg)], axis=1)

    q_pad = jnp.zeros((LANES - HEAD_DIM, hg * tq), BF16)
    qr_t = lane_cat(qrt_ref)
    qp4 = jnp.concatenate([lane_cat(qpt_ref), q_pad], axis=0)
    qr4 = jnp.concatenate([qr_t, q_pad], axis=0)

    any_c = t_row >= CMP_BLOCK - 1
    per_slc = SLC_BLOCK // CMP_STRIDE
    n_half = tq // LANES
    cur = t_row // SLC_BLOCK
    e_ref[:, 0:E_PAD, :] = jnp.zeros((n_half, E_PAD, LANES), F32)

    k_iota = lax.broadcasted_iota(jnp.int32, (tq, 1), 0)

    def key_rows(ref, kt):
        return ref[0, pl.ds(pl.multiple_of(kt * tq, tq), tq), :]

    win_kts = [jnp.maximum(i - back, 0) for back in range(WIN_TILES, -1, -1)]

    def win_bias(w):
        back = WIN_TILES - w
        kpos = win_kts[w] * tq + k_iota
        if back == WIN_TILES:
            ok = (kpos > t_row - WINDOW) & (i >= back)
        elif back == 0:
            ok = kpos <= t_row
        else:
            ok = jnp.broadcast_to(i >= back, (tq, tq))
        return jnp.where(ok, 0.0, NEG)

    def score_window():
        wmax = jnp.full((1, hg * tq), NEG, F32)
        for w in range(WIN_TILES + 1):
            st = _dot(key_rows(kwin_ref, win_kts[w]), qr4) + jnp.concatenate([win_bias(w)] * hg, axis=1)
            win_ref[w * tq:(w + 1) * tq, :] = st
            wmax = jnp.maximum(wmax, jnp.max(st, axis=0, keepdims=True))
        wmax_ref[0:1, :] = wmax

    def window_attention():
        wmax = wmax_ref[0:1, :]
        part = jnp.zeros(acc_ref.shape, F32)
        for w in range(WIN_TILES + 1):
            rows = slice(w * tq, (w + 1) * tq)
            p = jnp.concatenate(
                [jnp.exp2(win_ref[rows, h * tq:(h + 1) * tq] - wmax[:, h * tq:(h + 1) * tq]).astype(BF16)
                 for h in range(hg)], axis=1)
            part = part + _dot(vwint_ref[0, win_kts[w]], p)
        finish(2, part)

    def cmp_and_select(n_chunk):
        rows = n_chunk * CMP_CHUNK
        nblk = rows // per_slc
        n_idx = lax.broadcasted_iota(jnp.int32, (rows, tq), 0)
        bias_c = jnp.where((n_idx * CMP_STRIDE + (CMP_BLOCK - 1)) <= t_row, 0.0, NEG)
        m4 = jnp.full((1, hg * tq), NEG, F32)
        for c in range(n_chunk):
            rs = slice(c * CMP_CHUNK, (c + 1) * CMP_CHUNK)
            st = _dot(kc_ref[0, rs, :], qp4)
            gb_ref[rs, :] = st
            m4 = jnp.maximum(m4, jnp.max(st + jnp.concatenate([bias_c[rs]] * hg, axis=1),
                                         axis=0, keepdims=True))
        imp = jnp.zeros((nblk, tq), F32)
        for h in range(hg):
            hs = slice(h * tq, (h + 1) * tq)
            e = jnp.exp2(gb_ref[0:rows, hs] + bias_c - m4[:, hs])
            for c in range(n_half):
                e_ref[c, E_PAD:E_PAD + rows, :] = e[:, c * LANES:(c + 1) * LANES]
            oc = _dot(vct_ref[0, :, 0:rows], e.astype(BF16))
            rl = jnp.where(any_c, 1.0 / jnp.maximum(oc[HEAD_DIM:HEAD_DIM + 1, :], 1e-30), 0.0)
            yacc_ref[h * HEAD_DIM:(h + 1) * HEAD_DIM, :] = oc[0:HEAD_DIM, :] * (rl * gt_ref[0, 3 * h:3 * h + 1, :])
            tap = lambda d: jnp.concatenate(
                [e_ref[c, pl.ds(E_PAD + d, nblk, stride=per_slc), :] for c in range(n_half)], axis=1)
            imp_h = 0.5 * (tap(-1) + tap(per_slc - 1))
            for d in range(per_slc - 1):
                imp_h = imp_h + tap(d)
            imp = imp + imp_h * rl

        score_window()

        blk = lax.broadcasted_iota(jnp.int32, (nblk, tq), 0)
        blk_f = blk.astype(F32)
        forced = (blk == 0) | (blk == cur) | (blk == cur - 1)
        v = jnp.where(blk > cur, NEG, jnp.where(forced, REMOVED, imp))
        for _ in range(N_SELECT - N_FORCED):
            mx = jnp.max(v, axis=0, keepdims=True)
            first = jnp.min(jnp.where(v == mx, blk_f, float(nblk)), axis=0, keepdims=True)
            v = jnp.where(blk_f == first, REMOVED, v)
        selb_ref[0:nblk, :] = jnp.where((v == REMOVED) & (blk <= cur), 0.0, NEG)

    n_chunks_total = kc_ref.shape[1] // CMP_CHUNK
    visible = (start + tq - CMP_BLOCK) // (CMP_STRIDE * CMP_CHUNK) + 1
    for n_chunk in range(1, n_chunks_total + 1):
        pl.when(visible == n_chunk)(functools.partial(cmp_and_select, n_chunk))

    def reset():
        acc_ref[...] = jnp.zeros_like(acc_ref)
        m_ref[...] = jnp.full_like(m_ref, NEG)

    def finish(branch, acc):
        for h in range(hg):
            hs = slice(h * tq, (h + 1) * tq)
            o = acc[0:HEAD_DIM, hs] * (1.0 / acc[HEAD_DIM:HEAD_DIM + 1, hs])
            yacc_ref[h * HEAD_DIM:(h + 1) * HEAD_DIM, :] += o * gt_ref[0, 3 * h + branch:3 * h + branch + 1, :]

    def q_biased(grp):
        band = selb_ref[pl.ds(pl.multiple_of(grp * BIAS_ROWS, BIAS_ROWS), BIAS_ROWS), :].astype(BF16)
        return jnp.concatenate([qr_t, jnp.concatenate([band] * hg, axis=1),
                                q_pad[BIAS_ROWS:]], axis=0)

    def score_tile(grp, j, q4, dst_ref):
        st = _dot(key_rows(kslc_ref, grp * KT_GROUP + j), q4)
        dst_ref[j * tq:(j + 1) * tq, :] = st
        return jnp.max(st, axis=0, keepdims=True)

    def probs(scores, m):
        return jnp.concatenate(
            [jnp.exp2(scores(slice(h * tq, (h + 1) * tq)) - m[:, h * tq:(h + 1) * tq]).astype(BF16)
             for h in range(hg)], axis=1)

    def slc_group(grp, src_ref, dst_ref):
        m_old = m_ref[0:1, :]
        m_new = jnp.maximum(m_old, jnp.max(gmax_ref[0:KT_GROUP, :], axis=0, keepdims=True))
        q_next = q_biased(grp + 1)
        tmax = []
        part = jnp.zeros(acc_ref.shape, F32)
        for j in range(KT_GROUP):
            tmax.append(score_tile(grp + 1, j, q_next, dst_ref))
            p = probs(lambda hs, j=j: src_ref[j * tq:(j + 1) * tq, hs], m_new)
            part = part + _dot(vslct_ref[0, grp * KT_GROUP + j], p)
        acc_ref[...] = acc_ref[...] * jnp.exp2(m_old - m_new) + part
        m_ref[0:1, :] = m_new
        gmax_ref[0:KT_GROUP, :] = jnp.concatenate(tmax, axis=0)

    def score_first(dst_ref):
        q4 = q_biased(0)
        gmax_ref[0:KT_GROUP, :] = jnp.concatenate(
            [score_tile(0, j, q4, dst_ref) for j in range(KT_GROUP)], axis=0)

    reset()
    n_grp = i // KT_GROUP
    odd = n_grp % 2

    @pl.when(odd == 0)
    def _():
        score_first(ga_ref)
        window_attention()

    @pl.when(odd == 1)
    def _():
        score_first(gb_ref)
        window_attention()
        slc_group(0, gb_ref, ga_ref)

    def slc_pair(pair, carry):
        slc_group(2 * pair + odd, ga_ref, gb_ref)
        slc_group(2 * pair + odd + 1, gb_ref, ga_ref)
        return carry

    lax.fori_loop(0, n_grp // 2, slc_pair, 0)

    n_rest = i - n_grp * KT_GROUP
    causal = jnp.where((start + k_iota) <= t_row, 0.0, NEG)
    diag_rows = pl.ds(pl.multiple_of(n_rest * tq, tq), tq)
    diag = lambda hs: ga_ref[diag_rows, hs] + causal
    m_old = m_ref[0:1, :]
    m_new = jnp.maximum(m_old, jnp.concatenate(
        [jnp.max(diag(slice(h * tq, (h + 1) * tq)), axis=0, keepdims=True) for h in range(hg)], axis=1))
    for j in range(KT_GROUP - 1):
        m_new = jnp.maximum(m_new, jnp.where(j < n_rest, gmax_ref[j:j + 1, :], NEG))
    acc_ref[...] = acc_ref[...] * jnp.exp2(m_old - m_new) + _dot(vslct_ref[0, i], probs(diag, m_new))

    def rest_tile(j, carry):
        rows = pl.ds(pl.multiple_of(j * tq, tq), tq)
        acc_ref[...] += _dot(vslct_ref[0, n_grp * KT_GROUP + j], probs(lambda hs: ga_ref[rows, hs], m_new))
        return carry

    lax.fori_loop(0, n_rest, rest_tile, 0)
    finish(1, acc_ref)

    y_ref[0] = yacc_ref[...].T.astype(BF16)


def _nsa(qpt, qrt, gt, kc, vct, kslc, vslct, kwin, vwint):
    b, _, seq = qpt.shape
    tq = ATT_T
    hg = HEADS_PER_GROUP
    n_cmp = kc.shape[1]
    nkt = seq // tq
    qspec = pl.BlockSpec((1, hg * HEAD_DIM, tq), lambda bi, g, i: (bi, g, i))
    in_specs = [
        qspec, qspec,
        pl.BlockSpec((1, LANES // NSA_GROUPS, tq), lambda bi, g, i: (bi, g, i)),
        pl.BlockSpec((1, n_cmp, LANES), lambda bi, g, i: (bi, 0, g)),
        pl.BlockSpec((1, LANES, n_cmp), lambda bi, g, i: (bi, g, 0)),
        _resident((1, seq, LANES), lambda bi, g, i: (bi, 0, g)),
        _resident((1, nkt, None, VT_ROWS, tq), lambda bi, g, i: (bi, 0, g, 0, 0)),
        _resident((1, seq, LANES), lambda bi, g, i: (bi, 0, g)),
        _resident((1, nkt, None, VT_ROWS, tq), lambda bi, g, i: (bi, 0, g, 0, 0)),
    ]
    return pl.pallas_call(
        _nsa_body,
        grid=(b, NSA_GROUPS, seq // tq),
        in_specs=in_specs,
        out_specs=pl.BlockSpec((1, tq, hg * HEAD_DIM), lambda bi, g, i: (bi, i, g)),
        out_shape=jax.ShapeDtypeStruct((b, seq, NSA_HEADS * HEAD_DIM), BF16),
        scratch_shapes=[pltpu.VMEM((VT_ROWS, hg * tq), F32),
                        pltpu.VMEM((8, hg * tq), F32),
                        pltpu.VMEM((seq // SLC_BLOCK, tq), F32),
                        pltpu.VMEM(((WIN_TILES + 1) * tq, hg * tq), F32),
                        pltpu.VMEM((8, hg * tq), F32),
                        pltpu.VMEM((tq // LANES, E_PAD + n_cmp, LANES), F32),
                        pltpu.VMEM((KT_GROUP * tq, hg * tq), F32),
                        pltpu.VMEM((KT_GROUP * tq, hg * tq), F32),
                        pltpu.VMEM((8, hg * tq), F32),
                        pltpu.VMEM((hg * HEAD_DIM, tq), F32)],
        compiler_params=_params(3),
        name="nsa",
    )(qpt, qrt, gt, kc, vct, kslc, vslct, kwin, vwint)


def _merge_ffn_body(x_ref, part_ref, gnsa_ref, ynsa_ref, wbn_ref, wo_ref, gmix_ref,
                    gpre_ref, gpost_ref, wa_ref, wb_ref, wout_ref, o_ref):
    merged = part_ref[...] + gnsa_ref[...].astype(F32) * _dot(ynsa_ref[...], wbn_ref[...])
    x = x_ref[...] + _rms(_dot(merged.astype(BF16), wo_ref[...]), gmix_ref[...])
    h = _rms(x, gpre_ref[...]).astype(BF16)
    y = jnp.zeros(x.shape, F32)
    for c0, c1 in zip(MERGE_FF_SPLITS[:-1], MERGE_FF_SPLITS[1:]):
        cs = slice(c0, c1)
        a = _dot(h, wa_ref[:, cs])
        g = (a * jax.nn.sigmoid(a) * _dot(h, wb_ref[:, cs])).astype(BF16)
        y = y + _dot(g, wout_ref[cs, :])
    o_ref[...] = x + 0.5 * _rms(y, gpost_ref[...])


def _merge_ffn(x, part, gnsa, ynsa, wbn, wo, gmix, gpre, gpost, wa, wb, wout):
    t = x.shape[0]
    tm = MERGE_TM
    row = lambda i: (i, 0)
    fix = lambda i: (0, 0)
    vec = _resident((1, D_MODEL), fix)
    return pl.pallas_call(
        _merge_ffn_body,
        grid=(t // tm,),
        in_specs=[pl.BlockSpec((tm, D_MODEL), row), pl.BlockSpec((tm, D_MODEL), row),
                  pl.BlockSpec((tm, D_MODEL), row), pl.BlockSpec((tm, NSA_HEADS * HEAD_DIM), row),
                  _resident(wbn.shape, fix), _resident(wo.shape, fix), vec, vec, vec,
                  _resident(wa.shape, fix), _resident(wb.shape, fix), _resident(wout.shape, fix)],
        out_specs=pl.BlockSpec((tm, D_MODEL), row),
        out_shape=jax.ShapeDtypeStruct((t, D_MODEL), F32),
        compiler_params=_params(1),
        name="merge_ffn",
    )(x, part, gnsa, ynsa, wbn, wo, gmix, gpre, gpost, wa, wb, wout)


def _pad_groups(w, groups, width, stride):
    d = w.shape[0]
    w = w.reshape(d, groups, width)
    return jnp.pad(w, ((0, 0), (0, 0), (0, stride - width))).reshape(d, groups * stride)


def _mix_weight(w):
    o = 0
    u = w[:, o:o + CONV_DIM]; o += CONV_DIM
    bg = w[:, o:o + CONV_DIM]; o += CONV_DIM
    cg = w[:, o:o + CONV_DIM]; o += CONV_DIM
    q = w[:, o:o + NSA_HEADS * HEAD_DIM]; o += NSA_HEADS * HEAD_DIM
    kv = w[:, o:o + 6 * NSA_GROUPS * HEAD_DIM].reshape(D_MODEL, 6, NSA_GROUPS * HEAD_DIM)
    o += 6 * NSA_GROUPS * HEAD_DIM
    gate = w[:, o:o + 3 * NSA_HEADS]; o += 3 * NSA_HEADS
    qmem = w[:, o:o + MEM_HEADS * MEM_HEAD_DIM]; o += MEM_HEADS * MEM_HEAD_DIM
    merge = w[:, o:]
    cols = [u, bg, cg, q, kv[:, 2], kv[:, 4], kv[:, 3], kv[:, 5], kv[:, 0], kv[:, 1],
            _pad_groups(gate, NSA_GROUPS, 3 * HEADS_PER_GROUP, LANES // NSA_GROUPS), qmem, merge]
    out = jnp.concatenate(cols, axis=1).astype(BF16)
    assert out.shape == (D_MODEL, C_TOTAL)
    return out


def _cmp_weights(pos_emb, w1, b1, w2):
    eye = jnp.eye(NSA_GROUPS, dtype=F32)
    w1r = w1.reshape(CMP_BLOCK, HEAD_DIM, CMP_HIDDEN)
    width = CMP_STRIDE * NSA_GROUPS * HEAD_DIM

    def spread(a):
        return jnp.einsum('pdj,gh->pgdhj', a, eye).reshape(width, NSA_GROUPS * CMP_HIDDEN).astype(BF16)

    def pe_row(p):
        return jnp.broadcast_to(p[:, None, :], (CMP_STRIDE, NSA_GROUPS, HEAD_DIM)).reshape(1, width)

    pe = jnp.concatenate([pe_row(pos_emb[:CMP_STRIDE]), pe_row(pos_emb[CMP_STRIDE:]),
                          jnp.zeros((6, width), F32)], axis=0)
    w2p = jnp.pad(w2, ((0, 0), (0, LANES - HEAD_DIM)))
    w2b = jnp.einsum('jc,gh->gjhc', w2p, eye).reshape(NSA_GROUPS * CMP_HIDDEN, NSA_GROUPS * LANES)
    return (pe, spread(w1r[:CMP_STRIDE]), spread(w1r[CMP_STRIDE:]),
            jnp.tile(b1, NSA_GROUPS)[None, :], w2b.astype(BF16))


def _rope_table():
    inv_freq = ROPE_THETA ** (-jnp.arange(0, ROT_DIM, 2, dtype=jnp.float32) / ROT_DIM)
    half = ROT_DIM // 2
    lane = np.arange(LANES)
    tab = jnp.zeros((8, LANES), F32)
    d = lane % HEAD_DIM
    freq = jnp.tile(jnp.pad(jnp.tile(inv_freq, 2), (0, HEAD_DIM - ROT_DIM)), LANES // HEAD_DIM)
    tab = tab.at[0].set(freq)
    tab = tab.at[1].set(jnp.asarray(np.where(d < half, -1.0, 0.0), F32))
    tab = tab.at[2].set(jnp.asarray(np.where((d >= half) & (d < ROT_DIM), 1.0, 0.0), F32))
    tab = tab.at[3].set(jnp.asarray(np.where(lane == HEAD_DIM, 1.0, 0.0), F32))
    return tab


def kernel(x, mem, positions, ffn1_norm_pre, ffn1_norm_post, ffn1_w_in, ffn1_w_out, mix_norm_pre, mix_norm_post, mem_norm, w_mix_in, conv_w, cmp_pos_k, cmp_pos_v, cmp_k_w1, cmp_k_b1, cmp_k_w2, cmp_v_w1, cmp_v_b1, cmp_v_w2, w_mem_kv, w_branch_conv, w_branch_nsa, w_branch_mem, w_mix_out, ffn2_norm_pre, ffn2_norm_post, ffn2_w_in, ffn2_w_out):
    batch, seq, _ = x.shape
    t = batch * seq
    n_chunks = seq // CMP_STRIDE
    xf = x.reshape(t, D_MODEL)
    pos = positions.reshape(t, 1)
    tab = _rope_table()
    one = jnp.tile(tab[3:4, :], (1, NSA_GROUPS))
    row = lambda g: g[None, :]

    for l in range(DEPTH):
        xf = _ffn(xf, row(ffn1_norm_pre[l]), row(ffn1_norm_post[l]),
                  ffn1_w_in[l][:, :D_FF].astype(BF16), ffn1_w_in[l][:, D_FF:].astype(BF16),
                  ffn1_w_out[l].astype(BF16))

        mk, mv = _memkv(mem, row(mem_norm[l]), w_mem_kv[l].astype(BF16))
        convw = jnp.pad(conv_w[l], ((0, 8 - CONV_WIDTH), (0, 0)))
        (part, gnsa, qpt, qrt, kslc, kwin, vslct, vwint, kcr, vcr, gt) = _mixin(
            xf, pos, row(mix_norm_pre[l]), tab, _mix_weight(w_mix_in[l]), convw, mk, mv,
            w_branch_conv[l].astype(BF16), w_branch_mem[l].astype(BF16), batch, seq)

        kc = _compress(kcr.reshape(batch, seq, LANES),
                       *_cmp_weights(cmp_pos_k[l], cmp_k_w1[l], cmp_k_b1[l], cmp_k_w2[l]), one, False)
        vct = _compress(vcr.reshape(batch, seq, LANES),
                        *_cmp_weights(cmp_pos_v[l], cmp_v_w1[l], cmp_v_b1[l], cmp_v_w2[l]), one, True)

        ynsa = _nsa(qpt, qrt, gt, kc, vct,
                    kslc.reshape(batch, seq, NSA_GROUPS * LANES), vslct,
                    kwin.reshape(batch, seq, NSA_GROUPS * LANES), vwint)

        xf = _merge_ffn(xf, part, gnsa, ynsa.reshape(t, NSA_HEADS * HEAD_DIM),
                        w_branch_nsa[l].astype(BF16), w_mix_out[l].astype(BF16), row(mix_norm_post[l]),
                        row(ffn2_norm_pre[l]), row(ffn2_norm_post[l]),
                        ffn2_w_in[l][:, :D_FF].astype(BF16), ffn2_w_in[l][:, D_FF:].astype(BF16),
                        ffn2_w_out[l].astype(BF16))
    return xf.reshape(batch, seq, D_MODEL)
```

```python
import functools

import jax
import jax.numpy as jnp
import numpy as np
from jax import lax
from jax.experimental import pallas as pl
from jax.experimental.pallas import tpu as pltpu

D_MODEL = 1024
DEPTH = 2
CONV_DIM = 512
CONV_WIDTH = 3
NSA_HEADS = 8
NSA_GROUPS = 2
HEADS_PER_GROUP = NSA_HEADS // NSA_GROUPS
HEAD_DIM = 64
ROT_DIM = HEAD_DIM // 4
ROPE_THETA = 500000.0
CMP_BLOCK = 32
CMP_STRIDE = 16
CMP_HIDDEN = 256
SLC_BLOCK = 64
N_SELECT = 16
WINDOW = 512
MEM_HEADS = 4
MEM_HEAD_DIM = 128
D_FF = 2816
EPS = 1e-6
NEG = -1e30
REMOVED = -3e38

LANES = 128
F32 = jnp.float32
BF16 = jnp.bfloat16

FFN_TM = 512
MIX_TM = 512
MERGE_TM = 512
MERGE_FF_SPLITS = (0, 1536, D_FF)
HALO = 8
ATT_T = 256
WIN_TILES = WINDOW // ATT_T
KT_GROUP = 4
BIAS_ROWS = KT_GROUP * ATT_T // SLC_BLOCK
VT_ROWS = HEAD_DIM + BIAS_ROWS
LOG2_E = 1.4426950408889634
CMP_CHUNK = 256
VARIANT_STEP = 2
N_FORCED = 3

C_U = 0
C_BG = C_U + CONV_DIM
C_CG = C_BG + CONV_DIM
C_Q = C_CG + CONV_DIM
C_KSLC = C_Q + NSA_HEADS * HEAD_DIM
C_KWIN = C_KSLC + LANES
C_VSLC = C_KWIN + LANES
C_VWIN = C_VSLC + LANES
C_KCR = C_VWIN + LANES
C_VCR = C_KCR + LANES
C_GATE = C_VCR + LANES
C_QMEM = C_GATE + LANES
C_MERGE = C_QMEM + MEM_HEADS * MEM_HEAD_DIM
C_TOTAL = C_MERGE + 3 * D_MODEL

VMEM_LIMIT = 56 * 1024 * 1024


def _params(n_axes):
    return pltpu.CompilerParams(dimension_semantics=("arbitrary",) * n_axes,
                                vmem_limit_bytes=VMEM_LIMIT)


def _resident(shape, index_map):
    return pl.BlockSpec(shape, index_map, pipeline_mode=pl.Buffered(1))


def _layer(arr, l):
    zeros = (0,) * (arr.ndim - 1)
    return pl.BlockSpec((None,) + arr.shape[1:], lambda *_: (l,) + zeros, pipeline_mode=pl.Buffered(1))


def _rms(x, g):
    return x * lax.rsqrt(jnp.mean(x * x, axis=-1, keepdims=True) + EPS) * g


def _dot(a, b):
    return jnp.dot(a, b, preferred_element_type=F32)


def _ffn_body(x_ref, gpre_ref, gpost_ref, wa_ref, wb_ref, wo_ref, o_ref):
    x = x_ref[...]
    h = _rms(x, gpre_ref[...]).astype(BF16)
    a = _dot(h, wa_ref[...])
    b = _dot(h, wb_ref[...])
    g = (a * jax.nn.sigmoid(a) * b).astype(BF16)
    y = _dot(g, wo_ref[...])
    o_ref[...] = x + 0.5 * _rms(y, gpost_ref[...])


def _ffn(x, l, *params):
    t = x.shape[0]
    row = lambda i: (i, 0)
    return pl.pallas_call(
        _ffn_body,
        grid=(t // FFN_TM,),
        in_specs=[pl.BlockSpec((FFN_TM, D_MODEL), row)] + [_layer(p, l) for p in params],
        out_specs=pl.BlockSpec((FFN_TM, D_MODEL), row),
        out_shape=jax.ShapeDtypeStruct((t, D_MODEL), F32),
        compiler_params=_params(1),
        name="ffn",
    )(x, *params)


def _memkv_body(mem_ref, g_ref, w_ref, mk_ref, mv_ref):
    h = _rms(mem_ref[0], g_ref[...]).astype(BF16)
    kv = _dot(h, w_ref[...])
    half = MEM_HEADS * MEM_HEAD_DIM
    mk_ref[0] = kv[:, :half].astype(BF16)
    mv_ref[0] = kv[:, half:].astype(BF16)


def _memkv(mem, g, w, l):
    b, m, _ = mem.shape
    half = MEM_HEADS * MEM_HEAD_DIM
    out = jax.ShapeDtypeStruct((b, m, half), BF16)
    return pl.pallas_call(
        _memkv_body,
        grid=(b,),
        in_specs=[pl.BlockSpec((1, m, D_MODEL), lambda i: (i, 0, 0)), _layer(g, l), _layer(w, l)],
        out_specs=[pl.BlockSpec((1, m, half), lambda i: (i, 0, 0))] * 2,
        out_shape=[out, out],
        compiler_params=_params(1),
        name="memkv",
    )(mem, g, w)


def _mix_body(tiles_per_batch,
              x_ref, halo_ref, pos_ref, gpre_ref, tab_ref, w_ref, convw_ref, mk_ref, mv_ref,
              wbc_ref, wbm_ref,
              part_ref, gnsa_ref, qpt_ref, qrt_ref, kslc_ref, kwin_ref, vslct_ref, vwint_ref,
              kcr_ref, vcr_ref, gatest_ref,
              cu_scr):
    tm = MIX_TM
    i = pl.program_id(0)
    gpre = gpre_ref[...]
    h = _rms(x_ref[...], gpre).astype(BF16)
    hh = _rms(halo_ref[...], gpre).astype(BF16)

    def proj(hv, c0, width):
        return _dot(hv, w_ref[:, c0:c0 + width])

    cu = proj(h, C_CG, CONV_DIM) * proj(h, C_U, CONV_DIM)
    cu_h = proj(hh, C_CG, CONV_DIM) * proj(hh, C_U, CONV_DIM)
    first = (i % tiles_per_batch) == 0
    cu_scr[0:HALO, :] = jnp.where(first, 0.0, cu_h)
    cu_scr[HALO:HALO + tm, :] = cu
    cw = convw_ref[...]
    conv = (cu * cw[2:3, :]
            + cu_scr[HALO - 1:HALO - 1 + tm, :] * cw[1:2, :]
            + cu_scr[HALO - 2:HALO - 2 + tm, :] * cw[0:1, :])
    y_conv = (proj(h, C_BG, CONV_DIM) * conv).astype(BF16)

    ang = pos_ref[...].astype(F32) * tab_ref[0:1, :]
    cos_t = jnp.cos(ang)
    sin_t = jnp.sin(ang)
    sin_a = sin_t * tab_ref[1:2, :]
    sin_b = sin_t * tab_ref[2:3, :]

    def rope(t):
        half = ROT_DIM // 2
        return (t * cos_t + pltpu.roll(t, LANES - half, 1) * sin_a
                + pltpu.roll(t, half, 1) * sin_b)

    wide = proj(h, C_Q, C_MERGE - C_Q)

    def nsa_cols(c0):
        return wide[:, c0 - C_Q:c0 - C_Q + LANES]

    q_tiles = [slice(j * ATT_T, (j + 1) * ATT_T) for j in range(tm // ATT_T)]
    scale = HEAD_DIM ** -0.5 * LOG2_E
    for pair in range(NSA_HEADS // 2):
        q = nsa_cols(C_Q + pair * LANES) * scale
        sl = slice(pair * LANES, (pair + 1) * LANES)
        for rows in q_tiles:
            qpt_ref[0, sl, rows] = q[rows].T.astype(BF16)
            qrt_ref[0, sl, rows] = rope(q)[rows].T.astype(BF16)

    seq_row = (i % tiles_per_batch) * tm + lax.broadcasted_iota(jnp.int32, (tm, LANES), 0)
    lane = lax.broadcasted_iota(jnp.int32, (tm, LANES), 1)
    blk_hot = jnp.where(lane - HEAD_DIM == (seq_row // SLC_BLOCK) % BIAS_ROWS, 1.0, 0.0)

    def group_slabs(x):
        return [jnp.where(lane < HEAD_DIM, y, 0.0) for y in (x, pltpu.roll(x, HEAD_DIM, 1))]

    for g, (ks, kw) in enumerate(zip(group_slabs(rope(nsa_cols(C_KSLC))), group_slabs(rope(nsa_cols(C_KWIN))))):
        sl = slice(g * LANES, (g + 1) * LANES)
        kslc_ref[:, sl] = (ks + blk_hot).astype(BF16)
        kwin_ref[:, sl] = kw.astype(BF16)

    ones_rows = jnp.where(lax.broadcasted_iota(jnp.int32, (VT_ROWS - HEAD_DIM, ATT_T), 0) == 0, 1.0, 0.0)
    vs, vw = nsa_cols(C_VSLC), nsa_cols(C_VWIN)
    gt = jax.nn.sigmoid(nsa_cols(C_GATE))
    for j, rows in enumerate(q_tiles):
        vs_t, vw_t = vs[rows].T, vw[rows].T
        for g in range(NSA_GROUPS):
            band = slice(g * HEAD_DIM, (g + 1) * HEAD_DIM)
            vslct_ref[0, j, g] = jnp.concatenate([vs_t[band], ones_rows], axis=0).astype(BF16)
            vwint_ref[0, j, g] = jnp.concatenate([vw_t[band], ones_rows], axis=0).astype(BF16)
        gatest_ref[0, :, rows] = gt[rows].T
    kcr_ref[...] = nsa_cols(C_KCR)
    vcr_ref[...] = nsa_cols(C_VCR)

    mem_scale = MEM_HEAD_DIM ** -0.5
    outs = []
    for hd in range(MEM_HEADS):
        sl = slice(hd * MEM_HEAD_DIM, (hd + 1) * MEM_HEAD_DIM)
        qm = (nsa_cols(C_QMEM + hd * MEM_HEAD_DIM) * mem_scale).astype(BF16)
        s = lax.dot_general(qm, mk_ref[0, :, sl], (((1,), (1,)), ((), ())),
                            preferred_element_type=F32)
        p = jnp.exp(s - jnp.max(s, axis=-1, keepdims=True))
        l = jnp.sum(p, axis=-1, keepdims=True)
        outs.append((_dot(p.astype(BF16), mv_ref[0, :, sl]) / l).astype(BF16))
    y_mem = jnp.concatenate(outs, axis=1)

    g_conv = jax.nn.sigmoid(proj(h, C_MERGE, D_MODEL))
    g_nsa = jax.nn.sigmoid(proj(h, C_MERGE + D_MODEL, D_MODEL))
    g_mem = jax.nn.sigmoid(proj(h, C_MERGE + 2 * D_MODEL, D_MODEL))
    part_ref[...] = g_conv * _dot(y_conv, wbc_ref[...]) + g_mem * _dot(y_mem, wbm_ref[...])
    gnsa_ref[...] = g_nsa.astype(BF16)


def _mixin(x, pos, gpre, tab, w_main, convw, mk, mv, wbc, wbm, batch, seq, l):
    t = batch * seq
    tm = MIX_TM
    tpb = seq // tm
    nt = tm // ATT_T
    row = lambda i: (i, 0)
    fix = lambda i: (0, 0)
    bt = lambda i: (i // tpb, 0, i % tpb)
    mem_len = mk.shape[1]
    half = MEM_HEADS * MEM_HEAD_DIM
    out_shape = [
        jax.ShapeDtypeStruct((t, D_MODEL), F32),
        jax.ShapeDtypeStruct((t, D_MODEL), BF16),
        jax.ShapeDtypeStruct((batch, NSA_HEADS * HEAD_DIM, seq), BF16),
        jax.ShapeDtypeStruct((batch, NSA_HEADS * HEAD_DIM, seq), BF16),
        jax.ShapeDtypeStruct((t, NSA_GROUPS * LANES), BF16),
        jax.ShapeDtypeStruct((t, NSA_GROUPS * LANES), BF16),
        jax.ShapeDtypeStruct((batch, seq // ATT_T, NSA_GROUPS, VT_ROWS, ATT_T), BF16),
        jax.ShapeDtypeStruct((batch, seq // ATT_T, NSA_GROUPS, VT_ROWS, ATT_T), BF16),
        jax.ShapeDtypeStruct((t, LANES), F32),
        jax.ShapeDtypeStruct((t, LANES), F32),
        jax.ShapeDtypeStruct((batch, LANES, seq), F32),
    ]
    vt_spec = pl.BlockSpec((1, nt, NSA_GROUPS, VT_ROWS, ATT_T), lambda i: (i // tpb, i % tpb, 0, 0, 0))
    out_specs = [
        pl.BlockSpec((tm, D_MODEL), row),
        pl.BlockSpec((tm, D_MODEL), row),
        pl.BlockSpec((1, NSA_HEADS * HEAD_DIM, tm), bt),
        pl.BlockSpec((1, NSA_HEADS * HEAD_DIM, tm), bt),
        pl.BlockSpec((tm, NSA_GROUPS * LANES), row),
        pl.BlockSpec((tm, NSA_GROUPS * LANES), row),
        vt_spec, vt_spec,
        pl.BlockSpec((tm, LANES), row),
        pl.BlockSpec((tm, LANES), row),
        pl.BlockSpec((1, LANES, tm), bt),
    ]
    in_specs = [
        pl.BlockSpec((tm, D_MODEL), row),
        pl.BlockSpec((HALO, D_MODEL), lambda i: (jnp.maximum(i * (tm // HALO) - 1, 0), 0)),
        pl.BlockSpec((tm, 1), row),
        _layer(gpre, l),
        _resident((8, LANES), fix),
        _layer(w_main, l),
        _layer(convw, l),
        pl.BlockSpec((1, mem_len, half), lambda i: (i // tpb, 0, 0)),
        pl.BlockSpec((1, mem_len, half), lambda i: (i // tpb, 0, 0)),
        _layer(wbc, l),
        _layer(wbm, l),
    ]
    return pl.pallas_call(
        functools.partial(_mix_body, tpb),
        grid=(t // tm,),
        in_specs=in_specs,
        out_specs=out_specs,
        out_shape=out_shape,
        scratch_shapes=[pltpu.VMEM((HALO + tm, CONV_DIM), F32)],
        compiler_params=_params(1),
        name="mixin",
    )(x, x, pos, gpre, tab, w_main, convw, mk, mv, wbc, wbm)


def _cmp_body(transpose_out, r_ref, pe_ref, w1a_ref, w1b_ref, b1_ref, w2_ref, one_ref, o_ref, o_scr):
    n = r_ref.shape[1] // CMP_STRIDE
    r = jnp.concatenate([r_ref[0, pl.ds(p, n, stride=CMP_STRIDE), :] for p in range(CMP_STRIDE)], axis=1)
    a = _dot((r + pe_ref[0:1, :]).astype(BF16), w1a_ref[...])
    b = _dot((r + pe_ref[1:2, :]).astype(BF16), w1b_ref[...])
    hid = a + pltpu.roll(b, n - 1, 0) + b1_ref[...]
    o = _dot(jax.nn.gelu(hid).astype(BF16), w2_ref[...])
    if transpose_out:
        o = o + one_ref[...]
    per_slc = SLC_BLOCK // CMP_STRIDE
    sub = CMP_CHUNK // per_slc
    for g in range(NSA_GROUPS):
        o_scr[g] = o[:, g * LANES:(g + 1) * LANES]
    o = jnp.concatenate(
        [jnp.concatenate([o_scr[g, pl.ds(c * CMP_CHUNK + q, sub, stride=per_slc), :]
                          for c in range(n // CMP_CHUNK) for q in range(per_slc)], axis=0)
         for g in range(NSA_GROUPS)], axis=1)
    if transpose_out:
        o_ref[0] = jnp.concatenate(
            [o[:, g * LANES:(g + 1) * LANES].T for g in range(NSA_GROUPS)], axis=0).astype(BF16)
    else:
        o_ref[0] = o.astype(BF16)


def _compress(r, weights, one, transpose_out, l):
    b, seq, lanes = r.shape
    n = seq // CMP_STRIDE
    oshape = (b, NSA_GROUPS * LANES, n) if transpose_out else (b, n, NSA_GROUPS * LANES)
    return pl.pallas_call(
        functools.partial(_cmp_body, transpose_out),
        grid=(b,),
        in_specs=[pl.BlockSpec((1, seq, lanes), lambda i: (i, 0, 0))] + [_layer(w, l) for w in weights]
        + [pl.BlockSpec(one.shape, lambda i: (0, 0))],
        out_specs=pl.BlockSpec((1,) + oshape[1:], lambda i: (i, 0, 0)),
        out_shape=jax.ShapeDtypeStruct(oshape, BF16),
        scratch_shapes=[pltpu.VMEM((NSA_GROUPS, n, LANES), F32)],
        compiler_params=_params(1),
        name="compress_v" if transpose_out else "compress_k",
    )(r, *weights, one)


def _nsa_body(qpt_ref, qrt_ref, gt_ref, kc_ref, vct_ref,
              kslc_ref, vslct_ref, kwin_ref, vwint_ref,
              y_ref,
              acc_ref, m_ref, selb_ref, win_ref, wmax_ref, ga_ref, gb_ref, gmax_ref, yacc_ref):
    tq = ATT_T
    hg = HEADS_PER_GROUP
    i = pl.program_id(2)
    start = i * tq
    t_row = start + lax.broadcasted_iota(jnp.int32, (1, tq), 1)

    def lane_cat(ref):
        return jnp.concatenate([ref[0, h * HEAD_DIM:(h + 1) * HEAD_DIM, :] for h in range(hg)], axis=1)

    q_pad = jnp.zeros((LANES - HEAD_DIM, hg * tq), BF16)
    qr_t = lane_cat(qrt_ref)
    qp4 = jnp.concatenate([lane_cat(qpt_ref), q_pad], axis=0)
    qr4 = jnp.concatenate([qr_t, q_pad], axis=0)

    any_c = t_row >= CMP_BLOCK - 1
    per_slc = SLC_BLOCK // CMP_STRIDE
    sub = CMP_CHUNK // per_slc
    sub_iota = lax.broadcasted_iota(jnp.int32, (sub, tq), 0)
    cur = t_row // SLC_BLOCK

    k_iota = lax.broadcasted_iota(jnp.int32, (tq, 1), 0)

    def key_rows(ref, kt):
        return ref[0, pl.ds(pl.multiple_of(kt * tq, tq), tq), :]

    win_kts = [jnp.maximum(i - back, 0) for back in range(WIN_TILES, -1, -1)]

    def win_bias(w):
        back = WIN_TILES - w
        kpos = win_kts[w] * tq + k_iota
        if back == WIN_TILES:
            ok = (kpos > t_row - WINDOW) & (i >= back)
        elif back == 0:
            ok = kpos <= t_row
        else:
            ok = jnp.broadcast_to(i >= back, (tq, tq))
        return jnp.where(ok, 0.0, NEG)

    def score_window():
        wmax = jnp.full((1, hg * tq), NEG, F32)
        for w in range(WIN_TILES + 1):
            st = _dot(key_rows(kwin_ref, win_kts[w]), qr4) + jnp.concatenate([win_bias(w)] * hg, axis=1)
            win_ref[w * tq:(w + 1) * tq, :] = st
            wmax = jnp.maximum(wmax, jnp.max(st, axis=0, keepdims=True))
        wmax_ref[0:1, :] = wmax

    def window_tile(w, wmax):
        rows = slice(w * tq, (w + 1) * tq)
        p = jnp.concatenate(
            [jnp.exp2(win_ref[rows, h * tq:(h + 1) * tq] - wmax[:, h * tq:(h + 1) * tq]).astype(BF16)
             for h in range(hg)], axis=1)
        return _dot(vwint_ref[0, win_kts[w]], p)

    def cmp_and_select(n_chunk):
        rows = n_chunk * CMP_CHUNK
        nblk = rows // per_slc
        n_last = (t_row - (CMP_BLOCK - 1)) // CMP_STRIDE
        m4 = jnp.full((1, hg * tq), NEG, F32)
        for c in range(n_chunk):
            rs = slice(c * CMP_CHUNK, (c + 1) * CMP_CHUNK)
            st = _dot(kc_ref[0, rs, :], qp4)
            if c >= n_chunk - 1 - VARIANT_STEP:
                r_loc = lax.broadcasted_iota(jnp.int32, (CMP_CHUNK, tq), 0)
                n_idx = c * CMP_CHUNK + per_slc * (r_loc % sub) + r_loc // sub
                st = st + jnp.concatenate([jnp.where(n_idx <= n_last, 0.0, NEG)] * hg, axis=1)
            gb_ref[rs, :] = st
            m4 = jnp.maximum(m4, jnp.max(st, axis=0, keepdims=True))
        imp = jnp.zeros((nblk, tq), F32)
        for h in range(hg):
            hs = slice(h * tq, (h + 1) * tq)
            e = jnp.exp2(gb_ref[0:rows, hs] - m4[:, hs])
            oc = _dot(vct_ref[0, :, 0:rows], e.astype(BF16))
            rl = jnp.where(any_c, 1.0 / jnp.maximum(oc[HEAD_DIM:HEAD_DIM + 1, :], 1e-30), 0.0)
            yacc_ref[h * HEAD_DIM:(h + 1) * HEAD_DIM, :] = oc[0:HEAD_DIM, :] * (rl * gt_ref[0, 3 * h:3 * h + 1, :])
            carry = jnp.zeros((1, tq), F32)
            parts = []
            for c in range(n_chunk):
                run = [e[c * CMP_CHUNK + d * sub:c * CMP_CHUNK + (d + 1) * sub] for d in range(per_slc)]
                up = jnp.where(sub_iota == 0, carry, pltpu.roll(run[per_slc - 1], 1, 0))
                carry = run[per_slc - 1][sub - 1:sub]
                parts.append(sum(run[1:per_slc - 1], run[0]) + 0.5 * (run[per_slc - 1] + up))
            imp = imp + jnp.concatenate(parts, axis=0) * rl

        score_window()

        blk = lax.broadcasted_iota(jnp.int32, (nblk, tq), 0)
        blk_f = blk.astype(F32)
        forced = (blk == 0) | (blk == cur) | (blk == cur - 1)
        v = jnp.where(blk > cur, NEG, jnp.where(forced, REMOVED, imp))
        for _ in range(N_SELECT - N_FORCED):
            mx = jnp.max(v, axis=0, keepdims=True)
            first = jnp.min(jnp.where(v == mx, blk_f, float(nblk)), axis=0, keepdims=True)
            v = jnp.where(blk_f == first, REMOVED, v)
        selb_ref[0:nblk, :] = jnp.where((v == REMOVED) & (blk <= cur), 0.0, NEG)

    n_chunks_total = kc_ref.shape[1] // CMP_CHUNK
    visible = (start + tq - CMP_BLOCK) // (CMP_STRIDE * CMP_CHUNK) + 1
    for n_chunk in range(VARIANT_STEP, n_chunks_total + 1, VARIANT_STEP):
        pl.when(pl.cdiv(visible, VARIANT_STEP) * VARIANT_STEP == n_chunk)(
            functools.partial(cmp_and_select, n_chunk))

    def reset():
        acc_ref[...] = jnp.zeros_like(acc_ref)
        m_ref[...] = jnp.full_like(m_ref, NEG)

    def finish(branch, acc):
        for h in range(hg):
            hs = slice(h * tq, (h + 1) * tq)
            o = acc[0:HEAD_DIM, hs] * (1.0 / acc[HEAD_DIM:HEAD_DIM + 1, hs])
            yacc_ref[h * HEAD_DIM:(h + 1) * HEAD_DIM, :] += o * gt_ref[0, 3 * h + branch:3 * h + branch + 1, :]

    def q_biased(grp):
        band = selb_ref[pl.ds(pl.multiple_of(grp * BIAS_ROWS, BIAS_ROWS), BIAS_ROWS), :].astype(BF16)
        return jnp.concatenate([qr_t, jnp.concatenate([band] * hg, axis=1),
                                q_pad[BIAS_ROWS:]], axis=0)

    def score_tile(grp, j, q4, dst_ref):
        st = _dot(key_rows(kslc_ref, grp * KT_GROUP + j), q4)
        dst_ref[j * tq:(j + 1) * tq, :] = st
        return jnp.max(st, axis=0, keepdims=True)

    def probs(scores, m):
        return jnp.concatenate(
            [jnp.exp2(scores(slice(h * tq, (h + 1) * tq)) - m[:, h * tq:(h + 1) * tq]).astype(BF16)
             for h in range(hg)], axis=1)

    def slc_group(grp, src_ref, dst_ref):
        m_old = m_ref[0:1, :]
        m_new = jnp.maximum(m_old, jnp.max(gmax_ref[0:KT_GROUP, :], axis=0, keepdims=True))
        q_next = q_biased(grp + 1)
        tmax = []
        part = jnp.zeros(acc_ref.shape, F32)
        for j in range(KT_GROUP):
            tmax.append(score_tile(grp + 1, j, q_next, dst_ref))
            p = probs(lambda hs, j=j: src_ref[j * tq:(j + 1) * tq, hs], m_new)
            part = part + _dot(vslct_ref[0, grp * KT_GROUP + j], p)
        acc_ref[...] = acc_ref[...] * jnp.exp2(m_old - m_new) + part
        m_ref[0:1, :] = m_new
        gmax_ref[0:KT_GROUP, :] = jnp.concatenate(tmax, axis=0)

    def score_first(dst_ref):
        q4 = q_biased(0)
        wmax = wmax_ref[0:1, :]
        part = jnp.zeros(acc_ref.shape, F32)
        tmax = []
        for j in range(KT_GROUP):
            tmax.append(score_tile(0, j, q4, dst_ref))
            if j <= WIN_TILES:
                part = part + window_tile(j, wmax)
        gmax_ref[0:KT_GROUP, :] = jnp.concatenate(tmax, axis=0)
        finish(2, part)

    reset()
    n_grp = i // KT_GROUP
    odd = n_grp % 2

    @pl.when(odd == 0)
    def _():
        score_first(ga_ref)

    @pl.when(odd == 1)
    def _():
        score_first(gb_ref)
        slc_group(0, gb_ref, ga_ref)

    def slc_pair(pair, carry):
        slc_group(2 * pair + odd, ga_ref, gb_ref)
        slc_group(2 * pair + odd + 1, gb_ref, ga_ref)
        return carry

    lax.fori_loop(0, n_grp // 2, slc_pair, 0)

    n_rest = i - n_grp * KT_GROUP
    causal = jnp.where((start + k_iota) <= t_row, 0.0, NEG)
    diag_rows = pl.ds(pl.multiple_of(n_rest * tq, tq), tq)
    diag = lambda hs: ga_ref[diag_rows, hs] + causal
    m_old = m_ref[0:1, :]
    m_new = jnp.maximum(m_old, jnp.concatenate(
        [jnp.max(diag(slice(h * tq, (h + 1) * tq)), axis=0, keepdims=True) for h in range(hg)], axis=1))
    for j in range(KT_GROUP - 1):
        m_new = jnp.maximum(m_new, jnp.where(j < n_rest, gmax_ref[j:j + 1, :], NEG))
    acc_ref[...] = acc_ref[...] * jnp.exp2(m_old - m_new) + _dot(vslct_ref[0, i], probs(diag, m_new))

    def rest_tile(j, carry):
        rows = pl.ds(pl.multiple_of(j * tq, tq), tq)
        acc_ref[...] += _dot(vslct_ref[0, n_grp * KT_GROUP + j], probs(lambda hs: ga_ref[rows, hs], m_new))
        return carry

    lax.fori_loop(0, n_rest, rest_tile, 0)
    finish(1, acc_ref)

    y_ref[0] = yacc_ref[...].T.astype(BF16)


def _nsa(qpt, qrt, gt, kc, vct, kslc, vslct, kwin, vwint):
    b, _, seq = qpt.shape
    tq = ATT_T
    hg = HEADS_PER_GROUP
    n_cmp = kc.shape[1]
    nkt = seq // tq
    qspec = pl.BlockSpec((1, hg * HEAD_DIM, tq), lambda bi, g, i: (bi, g, i))
    in_specs = [
        qspec, qspec,
        pl.BlockSpec((1, LANES // NSA_GROUPS, tq), lambda bi, g, i: (bi, g, i)),
        pl.BlockSpec((1, n_cmp, LANES), lambda bi, g, i: (bi, 0, g)),
        pl.BlockSpec((1, LANES, n_cmp), lambda bi, g, i: (bi, g, 0)),
        _resident((1, seq, LANES), lambda bi, g, i: (bi, 0, g)),
        _resident((1, nkt, None, VT_ROWS, tq), lambda bi, g, i: (bi, 0, g, 0, 0)),
        _resident((1, seq, LANES), lambda bi, g, i: (bi, 0, g)),
        _resident((1, nkt, None, VT_ROWS, tq), lambda bi, g, i: (bi, 0, g, 0, 0)),
    ]
    return pl.pallas_call(
        _nsa_body,
        grid=(b, NSA_GROUPS, seq // tq),
        in_specs=in_specs,
        out_specs=pl.BlockSpec((1, tq, hg * HEAD_DIM), lambda bi, g, i: (bi, i, g)),
        out_shape=jax.ShapeDtypeStruct((b, seq, NSA_HEADS * HEAD_DIM), BF16),
        scratch_shapes=[pltpu.VMEM((VT_ROWS, hg * tq), F32),
                        pltpu.VMEM((8, hg * tq), F32),
                        pltpu.VMEM((seq // SLC_BLOCK, tq), F32),
                        pltpu.VMEM(((WIN_TILES + 1) * tq, hg * tq), F32),
                        pltpu.VMEM((8, hg * tq), F32),
                        pltpu.VMEM((KT_GROUP * tq, hg * tq), F32),
                        pltpu.VMEM((KT_GROUP * tq, hg * tq), F32),
                        pltpu.VMEM((8, hg * tq), F32),
                        pltpu.VMEM((hg * HEAD_DIM, tq), F32)],
        compiler_params=_params(3),
        name="nsa",
    )(qpt, qrt, gt, kc, vct, kslc, vslct, kwin, vwint)


def _merge_ffn_body(x_ref, part_ref, gnsa_ref, ynsa_ref, wbn_ref, wo_ref, gmix_ref,
                    gpre_ref, gpost_ref, wa_ref, wb_ref, wout_ref, o_ref):
    merged = part_ref[...] + gnsa_ref[...].astype(F32) * _dot(ynsa_ref[...], wbn_ref[...])
    x = x_ref[...] + _rms(_dot(merged.astype(BF16), wo_ref[...]), gmix_ref[...])
    h = _rms(x, gpre_ref[...]).astype(BF16)
    y = jnp.zeros(x.shape, F32)
    for c0, c1 in zip(MERGE_FF_SPLITS[:-1], MERGE_FF_SPLITS[1:]):
        cs = slice(c0, c1)
        a = _dot(h, wa_ref[:, cs])
        g = (a * jax.nn.sigmoid(a) * _dot(h, wb_ref[:, cs])).astype(BF16)
        y = y + _dot(g, wout_ref[cs, :])
    o_ref[...] = x + 0.5 * _rms(y, gpost_ref[...])


def _merge_ffn(x, part, gnsa, ynsa, l, *params):
    t = x.shape[0]
    tm = MERGE_TM
    row = lambda i: (i, 0)
    return pl.pallas_call(
        _merge_ffn_body,
        grid=(t // tm,),
        in_specs=[pl.BlockSpec((tm, D_MODEL), row), pl.BlockSpec((tm, D_MODEL), row),
                  pl.BlockSpec((tm, D_MODEL), row), pl.BlockSpec((tm, NSA_HEADS * HEAD_DIM), row)]
        + [_layer(p, l) for p in params],
        out_specs=pl.BlockSpec((tm, D_MODEL), row),
        out_shape=jax.ShapeDtypeStruct((t, D_MODEL), F32),
        compiler_params=_params(1),
        name="merge_ffn",
    )(x, part, gnsa, ynsa, *params)


def _pad_groups(w, groups, width, stride):
    d = w.shape[0]
    w = w.reshape(d, groups, width)
    return jnp.pad(w, ((0, 0), (0, 0), (0, stride - width))).reshape(d, groups * stride)


def _mix_weight(w):
    o = 0
    u = w[:, o:o + CONV_DIM]; o += CONV_DIM
    bg = w[:, o:o + CONV_DIM]; o += CONV_DIM
    cg = w[:, o:o + CONV_DIM]; o += CONV_DIM
    q = w[:, o:o + NSA_HEADS * HEAD_DIM]; o += NSA_HEADS * HEAD_DIM
    kv = w[:, o:o + 6 * NSA_GROUPS * HEAD_DIM].reshape(D_MODEL, 6, NSA_GROUPS * HEAD_DIM)
    o += 6 * NSA_GROUPS * HEAD_DIM
    gate = w[:, o:o + 3 * NSA_HEADS]; o += 3 * NSA_HEADS
    qmem = w[:, o:o + MEM_HEADS * MEM_HEAD_DIM]; o += MEM_HEADS * MEM_HEAD_DIM
    merge = w[:, o:]
    cols = [u, bg, cg, q, kv[:, 2], kv[:, 4], kv[:, 3], kv[:, 5], kv[:, 0], kv[:, 1],
            _pad_groups(gate, NSA_GROUPS, 3 * HEADS_PER_GROUP, LANES // NSA_GROUPS), qmem, merge]
    out = jnp.concatenate(cols, axis=1).astype(BF16)
    assert out.shape == (D_MODEL, C_TOTAL)
    return out


def _cmp_weights(pos_emb, w1, b1, w2):
    eye = jnp.eye(NSA_GROUPS, dtype=F32)
    w1r = w1.reshape(CMP_BLOCK, HEAD_DIM, CMP_HIDDEN)
    width = CMP_STRIDE * NSA_GROUPS * HEAD_DIM

    def spread(a):
        return jnp.einsum('pdj,gh->pgdhj', a, eye).reshape(width, NSA_GROUPS * CMP_HIDDEN).astype(BF16)

    def pe_row(p):
        return jnp.broadcast_to(p[:, None, :], (CMP_STRIDE, NSA_GROUPS, HEAD_DIM)).reshape(1, width)

    pe = jnp.concatenate([pe_row(pos_emb[:CMP_STRIDE]), pe_row(pos_emb[CMP_STRIDE:]),
                          jnp.zeros((6, width), F32)], axis=0)
    w2p = jnp.pad(w2, ((0, 0), (0, LANES - HEAD_DIM)))
    w2b = jnp.einsum('jc,gh->gjhc', w2p, eye).reshape(NSA_GROUPS * CMP_HIDDEN, NSA_GROUPS * LANES)
    return (pe, spread(w1r[:CMP_STRIDE]), spread(w1r[CMP_STRIDE:]),
            jnp.tile(b1, NSA_GROUPS)[None, :], w2b.astype(BF16))


def _rope_table():
    inv_freq = ROPE_THETA ** (-jnp.arange(0, ROT_DIM, 2, dtype=jnp.float32) / ROT_DIM)
    half = ROT_DIM // 2
    lane = np.arange(LANES)
    tab = jnp.zeros((8, LANES), F32)
    d = lane % HEAD_DIM
    freq = jnp.tile(jnp.pad(jnp.tile(inv_freq, 2), (0, HEAD_DIM - ROT_DIM)), LANES // HEAD_DIM)
    tab = tab.at[0].set(freq)
    tab = tab.at[1].set(jnp.asarray(np.where(d < half, -1.0, 0.0), F32))
    tab = tab.at[2].set(jnp.asarray(np.where((d >= half) & (d < ROT_DIM), 1.0, 0.0), F32))
    tab = tab.at[3].set(jnp.asarray(np.where(lane == HEAD_DIM, 1.0, 0.0), F32))
    return tab


def kernel(x, mem, positions, ffn1_norm_pre, ffn1_norm_post, ffn1_w_in, ffn1_w_out, mix_norm_pre, mix_norm_post, mem_norm, w_mix_in, conv_w, cmp_pos_k, cmp_pos_v, cmp_k_w1, cmp_k_b1, cmp_k_w2, cmp_v_w1, cmp_v_b1, cmp_v_w2, w_mem_kv, w_branch_conv, w_branch_nsa, w_branch_mem, w_mix_out, ffn2_norm_pre, ffn2_norm_post, ffn2_w_in, ffn2_w_out):
    batch, seq, _ = x.shape
    t = batch * seq
    assert x.shape[-1] == D_MODEL and seq % MIX_TM == 0 and t % FFN_TM == 0 and t % MERGE_TM == 0
    assert MIX_TM % ATT_T == 0 and (seq // CMP_STRIDE) % CMP_CHUNK == 0 and (seq // ATT_T) % KT_GROUP == 0
    assert KT_GROUP > WIN_TILES and KT_GROUP * ATT_T >= seq // CMP_STRIDE
    xf = x.reshape(t, D_MODEL)
    pos = positions.reshape(t, 1)
    tab = _rope_table()
    one = jnp.tile(tab[3:4, :], (1, NSA_GROUPS))

    vec = lambda g: g[:, None, :]
    cast = lambda w: w.astype(BF16)
    ffn1 = (vec(ffn1_norm_pre), vec(ffn1_norm_post), cast(ffn1_w_in[:, :, :D_FF]), cast(ffn1_w_in[:, :, D_FF:]),
            cast(ffn1_w_out))
    merge_ffn2 = (cast(w_branch_nsa), cast(w_mix_out), vec(mix_norm_post), vec(ffn2_norm_pre), vec(ffn2_norm_post),
                  cast(ffn2_w_in[:, :, :D_FF]), cast(ffn2_w_in[:, :, D_FF:]), cast(ffn2_w_out))
    w_main = jax.vmap(_mix_weight)(w_mix_in)
    convw = jnp.pad(conv_w, ((0, 0), (0, 8 - CONV_WIDTH), (0, 0)))
    cmp_k = jax.vmap(_cmp_weights)(cmp_pos_k, cmp_k_w1, cmp_k_b1, cmp_k_w2)
    cmp_v = jax.vmap(_cmp_weights)(cmp_pos_v, cmp_v_w1, cmp_v_b1, cmp_v_w2)
    mem_g, mem_w = vec(mem_norm), cast(w_mem_kv)
    mix_g, wbc, wbm = vec(mix_norm_pre), cast(w_branch_conv), cast(w_branch_mem)

    for l in range(DEPTH):
        xf = _ffn(xf, l, *ffn1)
        mk, mv = _memkv(mem, mem_g, mem_w, l)
        (part, gnsa, qpt, qrt, kslc, kwin, vslct, vwint, kcr, vcr, gt) = _mixin(
            xf, pos, mix_g, tab, w_main, convw, mk, mv, wbc, wbm, batch, seq, l)
        kc = _compress(kcr.reshape(batch, seq, LANES), cmp_k, one, False, l)
        vct = _compress(vcr.reshape(batch, seq, LANES), cmp_v, one, True, l)
        ynsa = _nsa(qpt, qrt, gt, kc, vct,
                    kslc.reshape(batch, seq, NSA_GROUPS * LANES), vslct,
                    kwin.reshape(batch, seq, NSA_GROUPS * LANES), vwint)
        xf = _merge_ffn(xf, part, gnsa, ynsa.reshape(t, NSA_HEADS * HEAD_DIM), l, *merge_ffn2)
    return xf.reshape(batch, seq, D_MODEL)
```

```python
import functools

import jax
import jax.numpy as jnp
import numpy as np
from jax import lax
from jax.experimental import pallas as pl
from jax.experimental.pallas import tpu as pltpu

D_MODEL = 1024
DEPTH = 2
CONV_DIM = 512
CONV_WIDTH = 3
NSA_HEADS = 8
NSA_GROUPS = 2
HEADS_PER_GROUP = NSA_HEADS // NSA_GROUPS
HEAD_DIM = 64
ROT_DIM = HEAD_DIM // 4
ROPE_THETA = 500000.0
CMP_BLOCK = 32
CMP_STRIDE = 16
CMP_HIDDEN = 256
SLC_BLOCK = 64
N_SELECT = 16
WINDOW = 512
MEM_HEADS = 4
MEM_HEAD_DIM = 128
D_FF = 2816
EPS = 1e-6
NEG = -1e30
REMOVED = -3e38

LANES = 128
F32 = jnp.float32
BF16 = jnp.bfloat16

FFN_TM = 512
MIX_TM = 512
MERGE_TM = 512
MERGE_FF_SPLITS = (0, 1536, D_FF)
HALO = 8
ATT_T = 256
WIN_TILES = WINDOW // ATT_T
KT_GROUP = 4
BIAS_ROWS = KT_GROUP * ATT_T // SLC_BLOCK
VT_ROWS = HEAD_DIM + BIAS_ROWS
LOG2_E = 1.4426950408889634
CMP_CHUNK = 256
N_FORCED = 3

C_U = 0
C_BG = C_U + CONV_DIM
C_CG = C_BG + CONV_DIM
C_Q = C_CG + CONV_DIM
C_KSLC = C_Q + NSA_HEADS * HEAD_DIM
C_KWIN = C_KSLC + LANES
C_VSLC = C_KWIN + LANES
C_VWIN = C_VSLC + LANES
C_KCR = C_VWIN + LANES
C_VCR = C_KCR + LANES
C_GATE = C_VCR + LANES
C_QMEM = C_GATE + LANES
C_MERGE = C_QMEM + MEM_HEADS * MEM_HEAD_DIM
C_TOTAL = C_MERGE + 3 * D_MODEL

VMEM_LIMIT = 56 * 1024 * 1024


def _params(n_axes):
    return pltpu.CompilerParams(dimension_semantics=("arbitrary",) * n_axes,
                                vmem_limit_bytes=VMEM_LIMIT)


def _resident(shape, index_map):
    return pl.BlockSpec(shape, index_map, pipeline_mode=pl.Buffered(1))


def _layer(arr, l):
    zeros = (0,) * (arr.ndim - 1)
    return pl.BlockSpec((None,) + arr.shape[1:], lambda *_: (l,) + zeros, pipeline_mode=pl.Buffered(1))


def _rms(x, g):
    return x * lax.rsqrt(jnp.mean(x * x, axis=-1, keepdims=True) + EPS) * g


def _dot(a, b):
    return jnp.dot(a, b, preferred_element_type=F32)


def _ffn_body(x_ref, gpre_ref, gpost_ref, wa_ref, wb_ref, wo_ref, o_ref):
    x = x_ref[...]
    h = _rms(x, gpre_ref[...]).astype(BF16)
    a = _dot(h, wa_ref[...])
    b = _dot(h, wb_ref[...])
    g = (a * jax.nn.sigmoid(a) * b).astype(BF16)
    y = _dot(g, wo_ref[...])
    o_ref[...] = x + 0.5 * _rms(y, gpost_ref[...])


def _ffn(x, l, *params):
    t = x.shape[0]
    row = lambda i: (i, 0)
    return pl.pallas_call(
        _ffn_body,
        grid=(t // FFN_TM,),
        in_specs=[pl.BlockSpec((FFN_TM, D_MODEL), row)] + [_layer(p, l) for p in params],
        out_specs=pl.BlockSpec((FFN_TM, D_MODEL), row),
        out_shape=jax.ShapeDtypeStruct((t, D_MODEL), F32),
        compiler_params=_params(1),
        name="ffn",
    )(x, *params)


def _memkv_body(mem_ref, g_ref, w_ref, mk_ref, mv_ref):
    h = _rms(mem_ref[0], g_ref[...]).astype(BF16)
    kv = _dot(h, w_ref[...])
    half = MEM_HEADS * MEM_HEAD_DIM
    mk_ref[0] = kv[:, :half].astype(BF16)
    mv_ref[0] = kv[:, half:].astype(BF16)


def _memkv(mem, g, w, l):
    b, m, _ = mem.shape
    half = MEM_HEADS * MEM_HEAD_DIM
    out = jax.ShapeDtypeStruct((b, m, half), BF16)
    return pl.pallas_call(
        _memkv_body,
        grid=(b,),
        in_specs=[pl.BlockSpec((1, m, D_MODEL), lambda i: (i, 0, 0)), _layer(g, l), _layer(w, l)],
        out_specs=[pl.BlockSpec((1, m, half), lambda i: (i, 0, 0))] * 2,
        out_shape=[out, out],
        compiler_params=_params(1),
        name="memkv",
    )(mem, g, w)


def _mix_body(tiles_per_batch,
              x_ref, halo_ref, pos_ref, gpre_ref, tab_ref, w_ref, convw_ref, mk_ref, mv_ref,
              wbc_ref, wbm_ref,
              part_ref, gnsa_ref, qpt_ref, qrt_ref, kslc_ref, kwin_ref, vslct_ref, vwint_ref,
              kcr_ref, vcr_ref, gatest_ref,
              cu_scr):
    tm = MIX_TM
    i = pl.program_id(0)
    gpre = gpre_ref[...]
    h = _rms(x_ref[...], gpre).astype(BF16)
    hh = _rms(halo_ref[...], gpre).astype(BF16)

    def proj(hv, c0, width):
        return _dot(hv, w_ref[:, c0:c0 + width])

    cu = proj(h, C_CG, CONV_DIM) * proj(h, C_U, CONV_DIM)
    cu_h = proj(hh, C_CG, CONV_DIM) * proj(hh, C_U, CONV_DIM)
    first = (i % tiles_per_batch) == 0
    cu_scr[0:HALO, :] = jnp.where(first, 0.0, cu_h)
    cu_scr[HALO:HALO + tm, :] = cu
    cw = convw_ref[...]
    conv = (cu * cw[2:3, :]
            + cu_scr[HALO - 1:HALO - 1 + tm, :] * cw[1:2, :]
            + cu_scr[HALO - 2:HALO - 2 + tm, :] * cw[0:1, :])
    y_conv = (proj(h, C_BG, CONV_DIM) * conv).astype(BF16)

    ang = pos_ref[...].astype(F32) * tab_ref[0:1, :]
    cos_t = jnp.cos(ang)
    sin_t = jnp.sin(ang)
    sin_a = sin_t * tab_ref[1:2, :]
    sin_b = sin_t * tab_ref[2:3, :]

    def rope(t):
        half = ROT_DIM // 2
        return (t * cos_t + pltpu.roll(t, LANES - half, 1) * sin_a
                + pltpu.roll(t, half, 1) * sin_b)

    wide = proj(h, C_Q, C_MERGE - C_Q)

    def nsa_cols(c0):
        return wide[:, c0 - C_Q:c0 - C_Q + LANES]

    q_tiles = [slice(j * ATT_T, (j + 1) * ATT_T) for j in range(tm // ATT_T)]
    scale = HEAD_DIM ** -0.5 * LOG2_E
    for pair in range(NSA_HEADS // 2):
        q = nsa_cols(C_Q + pair * LANES) * scale
        sl = slice(pair * LANES, (pair + 1) * LANES)
        for rows in q_tiles:
            qpt_ref[0, sl, rows] = q[rows].T.astype(BF16)
            qrt_ref[0, sl, rows] = rope(q)[rows].T.astype(BF16)

    seq_row = (i % tiles_per_batch) * tm + lax.broadcasted_iota(jnp.int32, (tm, LANES), 0)
    lane = lax.broadcasted_iota(jnp.int32, (tm, LANES), 1)
    blk_hot = jnp.where(lane - HEAD_DIM == (seq_row // SLC_BLOCK) % BIAS_ROWS, 1.0, 0.0)

    def group_slabs(x):
        return [jnp.where(lane < HEAD_DIM, y, 0.0) for y in (x, pltpu.roll(x, HEAD_DIM, 1))]

    for g, (ks, kw) in enumerate(zip(group_slabs(rope(nsa_cols(C_KSLC))), group_slabs(rope(nsa_cols(C_KWIN))))):
        sl = slice(g * LANES, (g + 1) * LANES)
        kslc_ref[:, sl] = (ks + blk_hot).astype(BF16)
        kwin_ref[:, sl] = kw.astype(BF16)

    ones_rows = jnp.where(lax.broadcasted_iota(jnp.int32, (VT_ROWS - HEAD_DIM, ATT_T), 0) == 0, 1.0, 0.0)
    vs, vw = nsa_cols(C_VSLC), nsa_cols(C_VWIN)
    gt = jax.nn.sigmoid(nsa_cols(C_GATE))
    for j, rows in enumerate(q_tiles):
        vs_t, vw_t = vs[rows].T, vw[rows].T
        for g in range(NSA_GROUPS):
            band = slice(g * HEAD_DIM, (g + 1) * HEAD_DIM)
            vslct_ref[0, j, g] = jnp.concatenate([vs_t[band], ones_rows], axis=0).astype(BF16)
            vwint_ref[0, j, g] = jnp.concatenate([vw_t[band], ones_rows], axis=0).astype(BF16)
        gatest_ref[0, :, rows] = gt[rows].T
    kcr_ref[...] = nsa_cols(C_KCR)
    vcr_ref[...] = nsa_cols(C_VCR)

    mem_scale = MEM_HEAD_DIM ** -0.5
    outs = []
    for hd in range(MEM_HEADS):
        sl = slice(hd * MEM_HEAD_DIM, (hd + 1) * MEM_HEAD_DIM)
        qm = (nsa_cols(C_QMEM + hd * MEM_HEAD_DIM) * mem_scale).astype(BF16)
        s = lax.dot_general(qm, mk_ref[0, :, sl], (((1,), (1,)), ((), ())),
                            preferred_element_type=F32)
        p = jnp.exp(s - jnp.max(s, axis=-1, keepdims=True))
        l = jnp.sum(p, axis=-1, keepdims=True)
        outs.append((_dot(p.astype(BF16), mv_ref[0, :, sl]) / l).astype(BF16))
    y_mem = jnp.concatenate(outs, axis=1)

    g_conv = jax.nn.sigmoid(proj(h, C_MERGE, D_MODEL))
    g_nsa = jax.nn.sigmoid(proj(h, C_MERGE + D_MODEL, D_MODEL))
    g_mem = jax.nn.sigmoid(proj(h, C_MERGE + 2 * D_MODEL, D_MODEL))
    part_ref[...] = g_conv * _dot(y_conv, wbc_ref[...]) + g_mem * _dot(y_mem, wbm_ref[...])
    gnsa_ref[...] = g_nsa.astype(BF16)


def _mixin(x, pos, gpre, tab, w_main, convw, mk, mv, wbc, wbm, batch, seq, l):
    t = batch * seq
    tm = MIX_TM
    tpb = seq // tm
    nt = tm // ATT_T
    row = lambda i: (i, 0)
    fix = lambda i: (0, 0)
    bt = lambda i: (i // tpb, 0, i % tpb)
    mem_len = mk.shape[1]
    half = MEM_HEADS * MEM_HEAD_DIM
    out_shape = [
        jax.ShapeDtypeStruct((t, D_MODEL), F32),
        jax.ShapeDtypeStruct((t, D_MODEL), BF16),
        jax.ShapeDtypeStruct((batch, NSA_HEADS * HEAD_DIM, seq), BF16),
        jax.ShapeDtypeStruct((batch, NSA_HEADS * HEAD_DIM, seq), BF16),
        jax.ShapeDtypeStruct((t, NSA_GROUPS * LANES), BF16),
        jax.ShapeDtypeStruct((t, NSA_GROUPS * LANES), BF16),
        jax.ShapeDtypeStruct((batch, seq // ATT_T, NSA_GROUPS, VT_ROWS, ATT_T), BF16),
        jax.ShapeDtypeStruct((batch, seq // ATT_T, NSA_GROUPS, VT_ROWS, ATT_T), BF16),
        jax.ShapeDtypeStruct((t, LANES), F32),
        jax.ShapeDtypeStruct((t, LANES), F32),
        jax.ShapeDtypeStruct((batch, LANES, seq), F32),
    ]
    vt_spec = pl.BlockSpec((1, nt, NSA_GROUPS, VT_ROWS, ATT_T), lambda i: (i // tpb, i % tpb, 0, 0, 0))
    out_specs = [
        pl.BlockSpec((tm, D_MODEL), row),
        pl.BlockSpec((tm, D_MODEL), row),
        pl.BlockSpec((1, NSA_HEADS * HEAD_DIM, tm), bt),
        pl.BlockSpec((1, NSA_HEADS * HEAD_DIM, tm), bt),
        pl.BlockSpec((tm, NSA_GROUPS * LANES), row),
        pl.BlockSpec((tm, NSA_GROUPS * LANES), row),
        vt_spec, vt_spec,
        pl.BlockSpec((tm, LANES), row),
        pl.BlockSpec((tm, LANES), row),
        pl.BlockSpec((1, LANES, tm), bt),
    ]
    in_specs = [
        pl.BlockSpec((tm, D_MODEL), row),
        pl.BlockSpec((HALO, D_MODEL), lambda i: (jnp.maximum(i * (tm // HALO) - 1, 0), 0)),
        pl.BlockSpec((tm, 1), row),
        _layer(gpre, l),
        _resident((8, LANES), fix),
        _layer(w_main, l),
        _layer(convw, l),
        pl.BlockSpec((1, mem_len, half), lambda i: (i // tpb, 0, 0)),
        pl.BlockSpec((1, mem_len, half), lambda i: (i // tpb, 0, 0)),
        _layer(wbc, l),
        _layer(wbm, l),
    ]
    return pl.pallas_call(
        functools.partial(_mix_body, tpb),
        grid=(t // tm,),
        in_specs=in_specs,
        out_specs=out_specs,
        out_shape=out_shape,
        scratch_shapes=[pltpu.VMEM((HALO + tm, CONV_DIM), F32)],
        compiler_params=_params(1),
        name="mixin",
    )(x, x, pos, gpre, tab, w_main, convw, mk, mv, wbc, wbm)


def _cmp_body(transpose_out, r_ref, pe_ref, w1a_ref, w1b_ref, b1_ref, w2_ref, one_ref, o_ref, o_scr):
    n = r_ref.shape[1] // CMP_STRIDE
    r = jnp.concatenate([r_ref[0, pl.ds(p, n, stride=CMP_STRIDE), :] for p in range(CMP_STRIDE)], axis=1)
    a = _dot((r + pe_ref[0:1, :]).astype(BF16), w1a_ref[...])
    b = _dot((r + pe_ref[1:2, :]).astype(BF16), w1b_ref[...])
    hid = a + pltpu.roll(b, n - 1, 0) + b1_ref[...]
    o = _dot(jax.nn.gelu(hid).astype(BF16), w2_ref[...])
    if transpose_out:
        o = o + one_ref[...]
    per_slc = SLC_BLOCK // CMP_STRIDE
    sub = CMP_CHUNK // per_slc
    for g in range(NSA_GROUPS):
        o_scr[g] = o[:, g * LANES:(g + 1) * LANES]
    o = jnp.concatenate(
        [jnp.concatenate([o_scr[g, pl.ds(c * CMP_CHUNK + q, sub, stride=per_slc), :]
                          for c in range(n // CMP_CHUNK) for q in range(per_slc)], axis=0)
         for g in range(NSA_GROUPS)], axis=1)
    if transpose_out:
        o_ref[0] = jnp.concatenate(
            [o[:, g * LANES:(g + 1) * LANES].T for g in range(NSA_GROUPS)], axis=0).astype(BF16)
    else:
        o_ref[0] = o.astype(BF16)


def _compress(r, weights, one, transpose_out, l):
    b, seq, lanes = r.shape
    n = seq // CMP_STRIDE
    oshape = (b, NSA_GROUPS * LANES, n) if transpose_out else (b, n, NSA_GROUPS * LANES)
    return pl.pallas_call(
        functools.partial(_cmp_body, transpose_out),
        grid=(b,),
        in_specs=[pl.BlockSpec((1, seq, lanes), lambda i: (i, 0, 0))] + [_layer(w, l) for w in weights]
        + [pl.BlockSpec(one.shape, lambda i: (0, 0))],
        out_specs=pl.BlockSpec((1,) + oshape[1:], lambda i: (i, 0, 0)),
        out_shape=jax.ShapeDtypeStruct(oshape, BF16),
        scratch_shapes=[pltpu.VMEM((NSA_GROUPS, n, LANES), F32)],
        compiler_params=_params(1),
        name="compress_v" if transpose_out else "compress_k",
    )(r, *weights, one)


def _nsa_body(qpt_ref, qrt_ref, gt_ref, kc_ref, vct_ref,
              kslc_ref, vslct_ref, kwin_ref, vwint_ref,
              y_ref,
              acc_ref, m_ref, selb_ref, win_ref, wmax_ref, ga_ref, gb_ref, gmax_ref, yacc_ref):
    tq = ATT_T
    hg = HEADS_PER_GROUP
    i = pl.program_id(2)
    start = i * tq
    t_row = start + lax.broadcasted_iota(jnp.int32, (1, tq), 1)

    def lane_cat(ref):
        return jnp.concatenate([ref[0, h * HEAD_DIM:(h + 1) * HEAD_DIM, :] for h in range(hg)], axis=1)

    q_pad = jnp.zeros((LANES - HEAD_DIM, hg * tq), BF16)
    qr_t = lane_cat(qrt_ref)
    qp4 = jnp.concatenate([lane_cat(qpt_ref), q_pad], axis=0)
    qr4 = jnp.concatenate([qr_t, q_pad], axis=0)

    any_c = t_row >= CMP_BLOCK - 1
    per_slc = SLC_BLOCK // CMP_STRIDE
    sub = CMP_CHUNK // per_slc
    sub_iota = lax.broadcasted_iota(jnp.int32, (sub, tq), 0)
    cur = t_row // SLC_BLOCK

    k_iota = lax.broadcasted_iota(jnp.int32, (tq, 1), 0)

    def key_rows(ref, kt):
        return ref[0, pl.ds(pl.multiple_of(kt * tq, tq), tq), :]

    win_kts = [jnp.maximum(i - back, 0) for back in range(WIN_TILES, -1, -1)]

    def win_bias(w):
        back = WIN_TILES - w
        kpos = win_kts[w] * tq + k_iota
        if back == WIN_TILES:
            ok = (kpos > t_row - WINDOW) & (i >= back)
        elif back == 0:
            ok = kpos <= t_row
        else:
            ok = jnp.broadcast_to(i >= back, (tq, tq))
        return jnp.where(ok, 0.0, NEG)

    def score_window():
        wmax = jnp.full((1, hg * tq), NEG, F32)
        for w in range(WIN_TILES + 1):
            st = _dot(key_rows(kwin_ref, win_kts[w]), qr4) + jnp.concatenate([win_bias(w)] * hg, axis=1)
            win_ref[w * tq:(w + 1) * tq, :] = st
            wmax = jnp.maximum(wmax, jnp.max(st, axis=0, keepdims=True))
        wmax_ref[0:1, :] = wmax

    def window_tile(w, wmax):
        rows = slice(w * tq, (w + 1) * tq)
        p = jnp.concatenate(
            [jnp.exp2(win_ref[rows, h * tq:(h + 1) * tq] - wmax[:, h * tq:(h + 1) * tq]).astype(BF16)
             for h in range(hg)], axis=1)
        return _dot(vwint_ref[0, win_kts[w]], p)

    def cmp_and_select(n_chunk):
        rows = n_chunk * CMP_CHUNK
        nblk = rows // per_slc
        n_last = (t_row - (CMP_BLOCK - 1)) // CMP_STRIDE
        m4 = jnp.full((1, hg * tq), NEG, F32)
        for c in range(n_chunk):
            rs = slice(c * CMP_CHUNK, (c + 1) * CMP_CHUNK)
            st = _dot(kc_ref[0, rs, :], qp4)
            if c >= n_chunk - 2:
                r_loc = lax.broadcasted_iota(jnp.int32, (CMP_CHUNK, tq), 0)
                n_idx = c * CMP_CHUNK + per_slc * (r_loc % sub) + r_loc // sub
                st = st + jnp.concatenate([jnp.where(n_idx <= n_last, 0.0, NEG)] * hg, axis=1)
            gb_ref[rs, :] = st
            m4 = jnp.maximum(m4, jnp.max(st, axis=0, keepdims=True))
        imp = jnp.zeros((nblk, tq), F32)
        for h in range(hg):
            hs = slice(h * tq, (h + 1) * tq)
            e = jnp.exp2(gb_ref[0:rows, hs] - m4[:, hs])
            oc = _dot(vct_ref[0, :, 0:rows], e.astype(BF16))
            rl = jnp.where(any_c, 1.0 / jnp.maximum(oc[HEAD_DIM:HEAD_DIM + 1, :], 1e-30), 0.0)
            yacc_ref[h * HEAD_DIM:(h + 1) * HEAD_DIM, :] = oc[0:HEAD_DIM, :] * (rl * gt_ref[0, 3 * h:3 * h + 1, :])
            carry = jnp.zeros((1, tq), F32)
            parts = []
            for c in range(n_chunk):
                run = [e[c * CMP_CHUNK + d * sub:c * CMP_CHUNK + (d + 1) * sub] for d in range(per_slc)]
                up = jnp.where(sub_iota == 0, carry, pltpu.roll(run[per_slc - 1], 1, 0))
                carry = run[per_slc - 1][sub - 1:sub]
                parts.append(sum(run[1:per_slc - 1], run[0]) + 0.5 * (run[per_slc - 1] + up))
            imp = imp + jnp.concatenate(parts, axis=0) * rl

        score_window()

        blk = lax.broadcasted_iota(jnp.int32, (nblk, tq), 0)
        blk_f = blk.astype(F32)
        forced = (blk == 0) | (blk == cur) | (blk == cur - 1)
        v = jnp.where(blk > cur, NEG, jnp.where(forced, REMOVED, imp))
        for _ in range(N_SELECT - N_FORCED):
            mx = jnp.max(v, axis=0, keepdims=True)
            first = jnp.min(jnp.where(v == mx, blk_f, float(nblk)), axis=0, keepdims=True)
            v = jnp.where(blk_f == first, REMOVED, v)
        selb_ref[0:nblk, :] = jnp.where((v == REMOVED) & (blk <= cur), 0.0, NEG)

    n_chunks_total = kc_ref.shape[1] // CMP_CHUNK
    visible = (start + tq - CMP_BLOCK) // (CMP_STRIDE * CMP_CHUNK) + 1
    for n_chunk in range(1, n_chunks_total + 1):
        pl.when(visible == n_chunk)(functools.partial(cmp_and_select, n_chunk))

    def reset():
        acc_ref[...] = jnp.zeros_like(acc_ref)
        m_ref[...] = jnp.full_like(m_ref, NEG)

    def finish(branch, acc):
        for h in range(hg):
            hs = slice(h * tq, (h + 1) * tq)
            o = acc[0:HEAD_DIM, hs] * (1.0 / acc[HEAD_DIM:HEAD_DIM + 1, hs])
            yacc_ref[h * HEAD_DIM:(h + 1) * HEAD_DIM, :] += o * gt_ref[0, 3 * h + branch:3 * h + branch + 1, :]

    def q_biased(grp):
        band = selb_ref[pl.ds(pl.multiple_of(grp * BIAS_ROWS, BIAS_ROWS), BIAS_ROWS), :].astype(BF16)
        return jnp.concatenate([qr_t, jnp.concatenate([band] * hg, axis=1),
                                q_pad[BIAS_ROWS:]], axis=0)

    def score_tile(grp, j, q4, dst_ref):
        st = _dot(key_rows(kslc_ref, grp * KT_GROUP + j), q4)
        dst_ref[j * tq:(j + 1) * tq, :] = st
        return jnp.max(st, axis=0, keepdims=True)

    def probs(scores, m):
        return jnp.concatenate(
            [jnp.exp2(scores(slice(h * tq, (h + 1) * tq)) - m[:, h * tq:(h + 1) * tq]).astype(BF16)
             for h in range(hg)], axis=1)

    def slc_group(grp, src_ref, dst_ref):
        m_old = m_ref[0:1, :]
        m_new = jnp.maximum(m_old, jnp.max(gmax_ref[0:KT_GROUP, :], axis=0, keepdims=True))
        q_next = q_biased(grp + 1)
        tmax = []
        part = jnp.zeros(acc_ref.shape, F32)
        for j in range(KT_GROUP):
            tmax.append(score_tile(grp + 1, j, q_next, dst_ref))
            p = probs(lambda hs, j=j: src_ref[j * tq:(j + 1) * tq, hs], m_new)
            part = part + _dot(vslct_ref[0, grp * KT_GROUP + j], p)
        acc_ref[...] = acc_ref[...] * jnp.exp2(m_old - m_new) + part
        m_ref[0:1, :] = m_new
        gmax_ref[0:KT_GROUP, :] = jnp.concatenate(tmax, axis=0)

    def score_first(dst_ref):
        q4 = q_biased(0)
        wmax = wmax_ref[0:1, :]
        part = jnp.zeros(acc_ref.shape, F32)
        tmax = []
        for j in range(KT_GROUP):
            tmax.append(score_tile(0, j, q4, dst_ref))
            if j <= WIN_TILES:
                part = part + window_tile(j, wmax)
        gmax_ref[0:KT_GROUP, :] = jnp.concatenate(tmax, axis=0)
        finish(2, part)

    reset()
    n_grp = i // KT_GROUP
    odd = n_grp % 2

    @pl.when(odd == 0)
    def _():
        score_first(ga_ref)

    @pl.when(odd == 1)
    def _():
        score_first(gb_ref)
        slc_group(0, gb_ref, ga_ref)

    def slc_pair(pair, carry):
        slc_group(2 * pair + odd, ga_ref, gb_ref)
        slc_group(2 * pair + odd + 1, gb_ref, ga_ref)
        return carry

    lax.fori_loop(0, n_grp // 2, slc_pair, 0)

    n_rest = i - n_grp * KT_GROUP
    causal = jnp.where((start + k_iota) <= t_row, 0.0, NEG)
    diag_rows = pl.ds(pl.multiple_of(n_rest * tq, tq), tq)
    diag = lambda hs: ga_ref[diag_rows, hs] + causal
    m_old = m_ref[0:1, :]
    m_new = jnp.maximum(m_old, jnp.concatenate(
        [jnp.max(diag(slice(h * tq, (h + 1) * tq)), axis=0, keepdims=True) for h in range(hg)], axis=1))
    for j in range(KT_GROUP - 1):
        m_new = jnp.maximum(m_new, jnp.where(j < n_rest, gmax_ref[j:j + 1, :], NEG))
    acc_ref[...] = acc_ref[...] * jnp.exp2(m_old - m_new) + _dot(vslct_ref[0, i], probs(diag, m_new))

    def rest_tile(j, carry):
        rows = pl.ds(pl.multiple_of(j * tq, tq), tq)
        acc_ref[...] += _dot(vslct_ref[0, n_grp * KT_GROUP + j], probs(lambda hs: ga_ref[rows, hs], m_new))
        return carry

    lax.fori_loop(0, n_rest, rest_tile, 0)
    finish(1, acc_ref)

    y_ref[0] = yacc_ref[...].T.astype(BF16)


def _nsa(qpt, qrt, gt, kc, vct, kslc, vslct, kwin, vwint):
    b, _, seq = qpt.shape
    tq = ATT_T
    hg = HEADS_PER_GROUP
    n_cmp = kc.shape[1]
    nkt = seq // tq
    qspec = pl.BlockSpec((1, hg * HEAD_DIM, tq), lambda bi, g, i: (bi, g, i))
    in_specs = [
        qspec, qspec,
        pl.BlockSpec((1, LANES // NSA_GROUPS, tq), lambda bi, g, i: (bi, g, i)),
        pl.BlockSpec((1, n_cmp, LANES), lambda bi, g, i: (bi, 0, g)),
        pl.BlockSpec((1, LANES, n_cmp), lambda bi, g, i: (bi, g, 0)),
        _resident((1, seq, LANES), lambda bi, g, i: (bi, 0, g)),
        _resident((1, nkt, None, VT_ROWS, tq), lambda bi, g, i: (bi, 0, g, 0, 0)),
        _resident((1, seq, LANES), lambda bi, g, i: (bi, 0, g)),
        _resident((1, nkt, None, VT_ROWS, tq), lambda bi, g, i: (bi, 0, g, 0, 0)),
    ]
    return pl.pallas_call(
        _nsa_body,
        grid=(b, NSA_GROUPS, seq // tq),
        in_specs=in_specs,
        out_specs=pl.BlockSpec((1, tq, hg * HEAD_DIM), lambda bi, g, i: (bi, i, g)),
        out_shape=jax.ShapeDtypeStruct((b, seq, NSA_HEADS * HEAD_DIM), BF16),
        scratch_shapes=[pltpu.VMEM((VT_ROWS, hg * tq), F32),
                        pltpu.VMEM((8, hg * tq), F32),
                        pltpu.VMEM((seq // SLC_BLOCK, tq), F32),
                        pltpu.VMEM(((WIN_TILES + 1) * tq, hg * tq), F32),
                        pltpu.VMEM((8, hg * tq), F32),
                        pltpu.VMEM((KT_GROUP * tq, hg * tq), F32),
                        pltpu.VMEM((KT_GROUP * tq, hg * tq), F32),
                        pltpu.VMEM((8, hg * tq), F32),
                        pltpu.VMEM((hg * HEAD_DIM, tq), F32)],
        compiler_params=_params(3),
        name="nsa",
    )(qpt, qrt, gt, kc, vct, kslc, vslct, kwin, vwint)


def _merge_ffn_body(x_ref, part_ref, gnsa_ref, ynsa_ref, wbn_ref, wo_ref, gmix_ref,
                    gpre_ref, gpost_ref, wa_ref, wb_ref, wout_ref, o_ref):
    merged = part_ref[...] + gnsa_ref[...].astype(F32) * _dot(ynsa_ref[...], wbn_ref[...])
    x = x_ref[...] + _rms(_dot(merged.astype(BF16), wo_ref[...]), gmix_ref[...])
    h = _rms(x, gpre_ref[...]).astype(BF16)
    y = jnp.zeros(x.shape, F32)
    for c0, c1 in zip(MERGE_FF_SPLITS[:-1], MERGE_FF_SPLITS[1:]):
        cs = slice(c0, c1)
        a = _dot(h, wa_ref[:, cs])
        g = (a * jax.nn.sigmoid(a) * _dot(h, wb_ref[:, cs])).astype(BF16)
        y = y + _dot(g, wout_ref[cs, :])
    o_ref[...] = x + 0.5 * _rms(y, gpost_ref[...])


def _merge_ffn(x, part, gnsa, ynsa, l, *params):
    t = x.shape[0]
    tm = MERGE_TM
    row = lambda i: (i, 0)
    return pl.pallas_call(
        _merge_ffn_body,
        grid=(t // tm,),
        in_specs=[pl.BlockSpec((tm, D_MODEL), row), pl.BlockSpec((tm, D_MODEL), row),
                  pl.BlockSpec((tm, D_MODEL), row), pl.BlockSpec((tm, NSA_HEADS * HEAD_DIM), row)]
        + [_layer(p, l) for p in params],
        out_specs=pl.BlockSpec((tm, D_MODEL), row),
        out_shape=jax.ShapeDtypeStruct((t, D_MODEL), F32),
        compiler_params=_params(1),
        name="merge_ffn",
    )(x, part, gnsa, ynsa, *params)


def _pad_groups(w, groups, width, stride):
    d = w.shape[0]
    w = w.reshape(d, groups, width)
    return jnp.pad(w, ((0, 0), (0, 0), (0, stride - width))).reshape(d, groups * stride)


def _mix_weight(w):
    o = 0
    u = w[:, o:o + CONV_DIM]; o += CONV_DIM
    bg = w[:, o:o + CONV_DIM]; o += CONV_DIM
    cg = w[:, o:o + CONV_DIM]; o += CONV_DIM
    q = w[:, o:o + NSA_HEADS * HEAD_DIM]; o += NSA_HEADS * HEAD_DIM
    kv = w[:, o:o + 6 * NSA_GROUPS * HEAD_DIM].reshape(D_MODEL, 6, NSA_GROUPS * HEAD_DIM)
    o += 6 * NSA_GROUPS * HEAD_DIM
    gate = w[:, o:o + 3 * NSA_HEADS]; o += 3 * NSA_HEADS
    qmem = w[:, o:o + MEM_HEADS * MEM_HEAD_DIM]; o += MEM_HEADS * MEM_HEAD_DIM
    merge = w[:, o:]
    cols = [u, bg, cg, q, kv[:, 2], kv[:, 4], kv[:, 3], kv[:, 5], kv[:, 0], kv[:, 1],
            _pad_groups(gate, NSA_GROUPS, 3 * HEADS_PER_GROUP, LANES // NSA_GROUPS), qmem, merge]
    out = jnp.concatenate(cols, axis=1).astype(BF16)
    assert out.shape == (D_MODEL, C_TOTAL)
    return out


def _cmp_weights(pos_emb, w1, b1, w2):
    eye = jnp.eye(NSA_GROUPS, dtype=F32)
    w1r = w1.reshape(CMP_BLOCK, HEAD_DIM, CMP_HIDDEN)
    width = CMP_STRIDE * NSA_GROUPS * HEAD_DIM

    def spread(a):
        return jnp.einsum('pdj,gh->pgdhj', a, eye).reshape(width, NSA_GROUPS * CMP_HIDDEN).astype(BF16)

    def pe_row(p):
        return jnp.broadcast_to(p[:, None, :], (CMP_STRIDE, NSA_GROUPS, HEAD_DIM)).reshape(1, width)

    pe = jnp.concatenate([pe_row(pos_emb[:CMP_STRIDE]), pe_row(pos_emb[CMP_STRIDE:]),
                          jnp.zeros((6, width), F32)], axis=0)
    w2p = jnp.pad(w2, ((0, 0), (0, LANES - HEAD_DIM)))
    w2b = jnp.einsum('jc,gh->gjhc', w2p, eye).reshape(NSA_GROUPS * CMP_HIDDEN, NSA_GROUPS * LANES)
    return (pe, spread(w1r[:CMP_STRIDE]), spread(w1r[CMP_STRIDE:]),
            jnp.tile(b1, NSA_GROUPS)[None, :], w2b.astype(BF16))


def _rope_table():
    inv_freq = ROPE_THETA ** (-jnp.arange(0, ROT_DIM, 2, dtype=jnp.float32) / ROT_DIM)
    half = ROT_DIM // 2
    lane = np.arange(LANES)
    tab = jnp.zeros((8, LANES), F32)
    d = lane % HEAD_DIM
    freq = jnp.tile(jnp.pad(jnp.tile(inv_freq, 2), (0, HEAD_DIM - ROT_DIM)), LANES // HEAD_DIM)
    tab = tab.at[0].set(freq)
    tab = tab.at[1].set(jnp.asarray(np.where(d < half, -1.0, 0.0), F32))
    tab = tab.at[2].set(jnp.asarray(np.where((d >= half) & (d < ROT_DIM), 1.0, 0.0), F32))
    tab = tab.at[3].set(jnp.asarray(np.where(lane == HEAD_DIM, 1.0, 0.0), F32))
    return tab


def kernel(x, mem, positions, ffn1_norm_pre, ffn1_norm_post, ffn1_w_in, ffn1_w_out, mix_norm_pre, mix_norm_post, mem_norm, w_mix_in, conv_w, cmp_pos_k, cmp_pos_v, cmp_k_w1, cmp_k_b1, cmp_k_w2, cmp_v_w1, cmp_v_b1, cmp_v_w2, w_mem_kv, w_branch_conv, w_branch_nsa, w_branch_mem, w_mix_out, ffn2_norm_pre, ffn2_norm_post, ffn2_w_in, ffn2_w_out):
    batch, seq, _ = x.shape
    t = batch * seq
    assert x.shape[-1] == D_MODEL and seq % MIX_TM == 0 and t % FFN_TM == 0 and t % MERGE_TM == 0
    assert MIX_TM % ATT_T == 0 and (seq // CMP_STRIDE) % CMP_CHUNK == 0 and (seq // ATT_T) % KT_GROUP == 0
    assert KT_GROUP > WIN_TILES and KT_GROUP * ATT_T >= seq // CMP_STRIDE
    xf = x.reshape(t, D_MODEL)
    pos = positions.reshape(t, 1)
    tab = _rope_table()
    one = jnp.tile(tab[3:4, :], (1, NSA_GROUPS))

    vec = lambda g: g[:, None, :]
    cast = lambda w: w.astype(BF16)
    ffn1 = (vec(ffn1_norm_pre), vec(ffn1_norm_post), cast(ffn1_w_in[:, :, :D_FF]), cast(ffn1_w_in[:, :, D_FF:]),
            cast(ffn1_w_out))
    merge_ffn2 = (cast(w_branch_nsa), cast(w_mix_out), vec(mix_norm_post), vec(ffn2_norm_pre), vec(ffn2_norm_post),
                  cast(ffn2_w_in[:, :, :D_FF]), cast(ffn2_w_in[:, :, D_FF:]), cast(ffn2_w_out))
    w_main = jax.vmap(_mix_weight)(w_mix_in)
    convw = jnp.pad(conv_w, ((0, 0), (0, 8 - CONV_WIDTH), (0, 0)))
    cmp_k = jax.vmap(_cmp_weights)(cmp_pos_k, cmp_k_w1, cmp_k_b1, cmp_k_w2)
    cmp_v = jax.vmap(_cmp_weights)(cmp_pos_v, cmp_v_w1, cmp_v_b1, cmp_v_w2)
    mem_g, mem_w = vec(mem_norm), cast(w_mem_kv)
    mix_g, wbc, wbm = vec(mix_norm_pre), cast(w_branch_conv), cast(w_branch_mem)

    for l in range(DEPTH):
        xf = _ffn(xf, l, *ffn1)
        mk, mv = _memkv(mem, mem_g, mem_w, l)
        (part, gnsa, qpt, qrt, kslc, kwin, vslct, vwint, kcr, vcr, gt) = _mixin(
            xf, pos, mix_g, tab, w_main, convw, mk, mv, wbc, wbm, batch, seq, l)
        kc = _compress(kcr.reshape(batch, seq, LANES), cmp_k, one, False, l)
        vct = _compress(vcr.reshape(batch, seq, LANES), cmp_v, one, True, l)
        ynsa = _nsa(qpt, qrt, gt, kc, vct,
                    kslc.reshape(batch, seq, NSA_GROUPS * LANES), vslct,
                    kwin.reshape(batch, seq, NSA_GROUPS * LANES), vwint)
        xf = _merge_ffn(xf, part, gnsa, ynsa.reshape(t, NSA_HEADS * HEAD_DIM), l, *merge_ffn2)
    return xf.reshape(batch, seq, D_MODEL)
```

```python
import functools

import jax
import jax.numpy as jnp
import numpy as np
from jax import lax
from jax.experimental import pallas as pl
from jax.experimental.pallas import tpu as pltpu

D_MODEL = 1024
DEPTH = 2
CONV_DIM = 512
CONV_WIDTH = 3
NSA_HEADS = 8
NSA_GROUPS = 2
HEADS_PER_GROUP = NSA_HEADS // NSA_GROUPS
HEAD_DIM = 64
ROT_DIM = HEAD_DIM // 4
ROPE_THETA = 500000.0
CMP_BLOCK = 32
CMP_STRIDE = 16
CMP_HIDDEN = 256
SLC_BLOCK = 64
N_SELECT = 16
WINDOW = 512
MEM_HEADS = 4
MEM_HEAD_DIM = 128
D_FF = 2816
EPS = 1e-6
NEG = -1e30
REMOVED = -3e38

LANES = 128
F32 = jnp.float32
BF16 = jnp.bfloat16

FFN_TM = 512
MIX_TM = 512
MERGE_TM = 512
MERGE_FF_SPLITS = (0, 1536, D_FF)
HALO = 8
TRIG_ROWS = 512
ATT_T = 256
WIN_TILES = WINDOW // ATT_T
KT_GROUP = 4
BIAS_ROWS = KT_GROUP * ATT_T // SLC_BLOCK
VT_ROWS = HEAD_DIM + BIAS_ROWS
LOG2_E = 1.4426950408889634
CMP_CHUNK = 256
N_FORCED = 3

C_U = 0
C_BG = C_U + CONV_DIM
C_CG = C_BG + CONV_DIM
C_Q = C_CG + CONV_DIM
C_KSLC = C_Q + NSA_HEADS * HEAD_DIM
C_KWIN = C_KSLC + LANES
C_VSLC = C_KWIN + LANES
C_VWIN = C_VSLC + LANES
C_KCR = C_VWIN + LANES
C_VCR = C_KCR + LANES
C_GATE = C_VCR + LANES
C_QMEM = C_GATE + LANES
C_MERGE = C_QMEM + MEM_HEADS * MEM_HEAD_DIM
C_TOTAL = C_MERGE + 3 * D_MODEL

VMEM_LIMIT = 56 * 1024 * 1024


def _params(n_axes):
    return pltpu.CompilerParams(dimension_semantics=("arbitrary",) * n_axes,
                                vmem_limit_bytes=VMEM_LIMIT)


def _resident(shape, index_map):
    return pl.BlockSpec(shape, index_map, pipeline_mode=pl.Buffered(1))


def _layer(arr, l):
    zeros = (0,) * (arr.ndim - 1)
    return pl.BlockSpec((None,) + arr.shape[1:], lambda *_: (l,) + zeros, pipeline_mode=pl.Buffered(1))


def _rms(x, g):
    return x * lax.rsqrt(jnp.mean(x * x, axis=-1, keepdims=True) + EPS) * g


def _dot(a, b):
    return jnp.dot(a, b, preferred_element_type=F32)


def _ffn_body(x_ref, gpre_ref, gpost_ref, wa_ref, wb_ref, wo_ref, o_ref):
    x = x_ref[...]
    h = _rms(x, gpre_ref[...]).astype(BF16)
    a = _dot(h, wa_ref[...])
    b = _dot(h, wb_ref[...])
    g = (a * jax.nn.sigmoid(a) * b).astype(BF16)
    y = _dot(g, wo_ref[...])
    o_ref[...] = x + 0.5 * _rms(y, gpost_ref[...])


def _ffn(x, l, *params):
    t = x.shape[0]
    row = lambda i: (i, 0)
    return pl.pallas_call(
        _ffn_body,
        grid=(t // FFN_TM,),
        in_specs=[pl.BlockSpec((FFN_TM, D_MODEL), row)] + [_layer(p, l) for p in params],
        out_specs=pl.BlockSpec((FFN_TM, D_MODEL), row),
        out_shape=jax.ShapeDtypeStruct((t, D_MODEL), F32),
        compiler_params=_params(1),
        name="ffn",
    )(x, *params)


def _memkv_body(mem_ref, g_ref, w_ref, mk_ref, mv_ref):
    h = _rms(mem_ref[0], g_ref[...]).astype(BF16)
    kv = _dot(h, w_ref[...])
    half = MEM_HEADS * MEM_HEAD_DIM
    mk_ref[0] = kv[:, :half].astype(BF16)
    mv_ref[0] = kv[:, half:].astype(BF16)


def _memkv(mem, g, w, l):
    b, m, _ = mem.shape
    half = MEM_HEADS * MEM_HEAD_DIM
    out = jax.ShapeDtypeStruct((b, m, half), BF16)
    return pl.pallas_call(
        _memkv_body,
        grid=(b,),
        in_specs=[pl.BlockSpec((1, m, D_MODEL), lambda i: (i, 0, 0)), _layer(g, l), _layer(w, l)],
        out_specs=[pl.BlockSpec((1, m, half), lambda i: (i, 0, 0))] * 2,
        out_shape=[out, out],
        compiler_params=_params(1),
        name="memkv",
    )(mem, g, w)


def _trig_body(pos_ref, freq_ref, cos_ref, sin_ref):
    ang = pos_ref[...].astype(F32) * freq_ref[...]
    cos_ref[...] = jnp.cos(ang)
    sin_ref[...] = jnp.sin(ang)


def _rope_trig(positions):
    t = positions.size
    per_row = LANES // ROT_DIM
    rows = t // per_row
    tr = min(rows, TRIG_ROWS)
    inv_freq = ROPE_THETA ** (-jnp.arange(0, ROT_DIM, 2, dtype=jnp.float32) / ROT_DIM)
    freq = jnp.tile(jnp.tile(inv_freq, 2), per_row)[None, :]
    pos = jnp.repeat(positions.reshape(rows, per_row), ROT_DIM, axis=1)
    spec = pl.BlockSpec((tr, LANES), lambda i: (i, 0))
    packed = jax.ShapeDtypeStruct((rows, LANES), F32)
    cos_p, sin_p = pl.pallas_call(
        _trig_body,
        grid=(rows // tr,),
        in_specs=[spec, pl.BlockSpec((1, LANES), lambda i: (0, 0))],
        out_specs=[spec, spec],
        out_shape=[packed, packed],
        compiler_params=_params(1),
        name="rope_trig",
    )(pos, freq)

    def unpack(x, fill):
        head = jnp.concatenate([x.reshape(t, ROT_DIM), jnp.full((t, HEAD_DIM - ROT_DIM), fill, F32)], axis=1)
        return jnp.tile(head, (1, LANES // HEAD_DIM))

    return unpack(cos_p, 1.0), unpack(sin_p, 0.0)


def _mix_body(tiles_per_batch,
              x_ref, halo_ref, cos_ref, sin_ref, gpre_ref, tab_ref, w_ref, convw_ref, mk_ref, mv_ref,
              wbc_ref, wbm_ref,
              part_ref, gnsa_ref, qpt_ref, qrt_ref, kslc_ref, kwin_ref, vslct_ref, vwint_ref,
              kcr_ref, vcr_ref, gatest_ref,
              cu_scr):
    tm = MIX_TM
    i = pl.program_id(0)
    gpre = gpre_ref[...]
    h = _rms(x_ref[...], gpre).astype(BF16)
    hh = _rms(halo_ref[...], gpre).astype(BF16)

    def proj(hv, c0, width):
        return _dot(hv, w_ref[:, c0:c0 + width])

    cu = proj(h, C_CG, CONV_DIM) * proj(h, C_U, CONV_DIM)
    cu_h = proj(hh, C_CG, CONV_DIM) * proj(hh, C_U, CONV_DIM)
    first = (i % tiles_per_batch) == 0
    cu_scr[0:HALO, :] = jnp.where(first, 0.0, cu_h)
    cu_scr[HALO:HALO + tm, :] = cu
    cw = convw_ref[...]
    conv = (cu * cw[2:3, :]
            + cu_scr[HALO - 1:HALO - 1 + tm, :] * cw[1:2, :]
            + cu_scr[HALO - 2:HALO - 2 + tm, :] * cw[0:1, :])
    y_conv = (proj(h, C_BG, CONV_DIM) * conv).astype(BF16)

    cos_t = cos_ref[...]
    sin_t = sin_ref[...]
    sin_a = sin_t * tab_ref[1:2, :]
    sin_b = sin_t * tab_ref[2:3, :]

    def rope(t):
        half = ROT_DIM // 2
        return (t * cos_t + pltpu.roll(t, LANES - half, 1) * sin_a
                + pltpu.roll(t, half, 1) * sin_b)

    wide = proj(h, C_Q, C_MERGE - C_Q)

    def nsa_cols(c0):
        return wide[:, c0 - C_Q:c0 - C_Q + LANES]

    q_tiles = [slice(j * ATT_T, (j + 1) * ATT_T) for j in range(tm // ATT_T)]
    scale = HEAD_DIM ** -0.5 * LOG2_E
    for pair in range(NSA_HEADS // 2):
        q = nsa_cols(C_Q + pair * LANES) * scale
        sl = slice(pair * LANES, (pair + 1) * LANES)
        for rows in q_tiles:
            qpt_ref[0, sl, rows] = q[rows].T.astype(BF16)
            qrt_ref[0, sl, rows] = rope(q)[rows].T.astype(BF16)

    seq_row = (i % tiles_per_batch) * tm + lax.broadcasted_iota(jnp.int32, (tm, LANES), 0)
    lane = lax.broadcasted_iota(jnp.int32, (tm, LANES), 1)
    blk_hot = jnp.where(lane - HEAD_DIM == (seq_row // SLC_BLOCK) % BIAS_ROWS, 1.0, 0.0)

    def group_slabs(x):
        return [jnp.where(lane < HEAD_DIM, y, 0.0) for y in (x, pltpu.roll(x, HEAD_DIM, 1))]

    for g, (ks, kw) in enumerate(zip(group_slabs(rope(nsa_cols(C_KSLC))), group_slabs(rope(nsa_cols(C_KWIN))))):
        sl = slice(g * LANES, (g + 1) * LANES)
        kslc_ref[:, sl] = (ks + blk_hot).astype(BF16)
        kwin_ref[:, sl] = kw.astype(BF16)

    ones_rows = jnp.where(lax.broadcasted_iota(jnp.int32, (VT_ROWS - HEAD_DIM, ATT_T), 0) == 0, 1.0, 0.0)
    vs, vw = nsa_cols(C_VSLC), nsa_cols(C_VWIN)
    gt = jax.nn.sigmoid(nsa_cols(C_GATE))
    for j, rows in enumerate(q_tiles):
        vs_t, vw_t = vs[rows].T, vw[rows].T
        for g in range(NSA_GROUPS):
            band = slice(g * HEAD_DIM, (g + 1) * HEAD_DIM)
            vslct_ref[0, j, g] = jnp.concatenate([vs_t[band], ones_rows], axis=0).astype(BF16)
            vwint_ref[0, j, g] = jnp.concatenate([vw_t[band], ones_rows], axis=0).astype(BF16)
        gatest_ref[0, :, rows] = gt[rows].T
    kcr_ref[...] = nsa_cols(C_KCR)
    vcr_ref[...] = nsa_cols(C_VCR)

    mem_scale = MEM_HEAD_DIM ** -0.5
    outs = []
    for hd in range(MEM_HEADS):
        sl = slice(hd * MEM_HEAD_DIM, (hd + 1) * MEM_HEAD_DIM)
        qm = (nsa_cols(C_QMEM + hd * MEM_HEAD_DIM) * mem_scale).astype(BF16)
        s = lax.dot_general(qm, mk_ref[0, :, sl], (((1,), (1,)), ((), ())),
                            preferred_element_type=F32)
        p = jnp.exp(s - jnp.max(s, axis=-1, keepdims=True))
        l = jnp.sum(p, axis=-1, keepdims=True)
        outs.append((_dot(p.astype(BF16), mv_ref[0, :, sl]) / l).astype(BF16))
    y_mem = jnp.concatenate(outs, axis=1)

    g_conv = jax.nn.sigmoid(proj(h, C_MERGE, D_MODEL))
    g_nsa = jax.nn.sigmoid(proj(h, C_MERGE + D_MODEL, D_MODEL))
    g_mem = jax.nn.sigmoid(proj(h, C_MERGE + 2 * D_MODEL, D_MODEL))
    part_ref[...] = g_conv * _dot(y_conv, wbc_ref[...]) + g_mem * _dot(y_mem, wbm_ref[...])
    gnsa_ref[...] = g_nsa.astype(BF16)


def _mixin(x, cos_t, sin_t, gpre, tab, w_main, convw, mk, mv, wbc, wbm, batch, seq, l):
    t = batch * seq
    tm = MIX_TM
    tpb = seq // tm
    nt = tm // ATT_T
    row = lambda i: (i, 0)
    fix = lambda i: (0, 0)
    bt = lambda i: (i // tpb, 0, i % tpb)
    mem_len = mk.shape[1]
    half = MEM_HEADS * MEM_HEAD_DIM
    out_shape = [
        jax.ShapeDtypeStruct((t, D_MODEL), F32),
        jax.ShapeDtypeStruct((t, D_MODEL), BF16),
        jax.ShapeDtypeStruct((batch, NSA_HEADS * HEAD_DIM, seq), BF16),
        jax.ShapeDtypeStruct((batch, NSA_HEADS * HEAD_DIM, seq), BF16),
        jax.ShapeDtypeStruct((t, NSA_GROUPS * LANES), BF16),
        jax.ShapeDtypeStruct((t, NSA_GROUPS * LANES), BF16),
        jax.ShapeDtypeStruct((batch, seq // ATT_T, NSA_GROUPS, VT_ROWS, ATT_T), BF16),
        jax.ShapeDtypeStruct((batch, seq // ATT_T, NSA_GROUPS, VT_ROWS, ATT_T), BF16),
        jax.ShapeDtypeStruct((t, LANES), F32),
        jax.ShapeDtypeStruct((t, LANES), F32),
        jax.ShapeDtypeStruct((batch, LANES, seq), F32),
    ]
    vt_spec = pl.BlockSpec((1, nt, NSA_GROUPS, VT_ROWS, ATT_T), lambda i: (i // tpb, i % tpb, 0, 0, 0))
    out_specs = [
        pl.BlockSpec((tm, D_MODEL), row),
        pl.BlockSpec((tm, D_MODEL), row),
        pl.BlockSpec((1, NSA_HEADS * HEAD_DIM, tm), bt),
        pl.BlockSpec((1, NSA_HEADS * HEAD_DIM, tm), bt),
        pl.BlockSpec((tm, NSA_GROUPS * LANES), row),
        pl.BlockSpec((tm, NSA_GROUPS * LANES), row),
        vt_spec, vt_spec,
        pl.BlockSpec((tm, LANES), row),
        pl.BlockSpec((tm, LANES), row),
        pl.BlockSpec((1, LANES, tm), bt),
    ]
    in_specs = [
        pl.BlockSpec((tm, D_MODEL), row),
        pl.BlockSpec((HALO, D_MODEL), lambda i: (jnp.maximum(i * (tm // HALO) - 1, 0), 0)),
        pl.BlockSpec((tm, LANES), row),
        pl.BlockSpec((tm, LANES), row),
        _layer(gpre, l),
        _resident((8, LANES), fix),
        _layer(w_main, l),
        _layer(convw, l),
        pl.BlockSpec((1, mem_len, half), lambda i: (i // tpb, 0, 0)),
        pl.BlockSpec((1, mem_len, half), lambda i: (i // tpb, 0, 0)),
        _layer(wbc, l),
        _layer(wbm, l),
    ]
    return pl.pallas_call(
        functools.partial(_mix_body, tpb),
        grid=(t // tm,),
        in_specs=in_specs,
        out_specs=out_specs,
        out_shape=out_shape,
        scratch_shapes=[pltpu.VMEM((HALO + tm, CONV_DIM), F32)],
        compiler_params=_params(1),
        name="mixin",
    )(x, x, cos_t, sin_t, gpre, tab, w_main, convw, mk, mv, wbc, wbm)


def _cmp_body(transpose_out, r_ref, pe_ref, w1a_ref, w1b_ref, b1_ref, w2_ref, one_ref, o_ref, o_scr):
    n = r_ref.shape[1] // CMP_STRIDE
    r = jnp.concatenate([r_ref[0, pl.ds(p, n, stride=CMP_STRIDE), :] for p in range(CMP_STRIDE)], axis=1)
    a = _dot((r + pe_ref[0:1, :]).astype(BF16), w1a_ref[...])
    b = _dot((r + pe_ref[1:2, :]).astype(BF16), w1b_ref[...])
    hid = a + pltpu.roll(b, n - 1, 0) + b1_ref[...]
    o = _dot(jax.nn.gelu(hid).astype(BF16), w2_ref[...])
    if transpose_out:
        o = o + one_ref[...]
    per_slc = SLC_BLOCK // CMP_STRIDE
    sub = CMP_CHUNK // per_slc
    for g in range(NSA_GROUPS):
        o_scr[g] = o[:, g * LANES:(g + 1) * LANES]
    o = jnp.concatenate(
        [jnp.concatenate([o_scr[g, pl.ds(c * CMP_CHUNK + q, sub, stride=per_slc), :]
                          for c in range(n // CMP_CHUNK) for q in range(per_slc)], axis=0)
         for g in range(NSA_GROUPS)], axis=1)
    if transpose_out:
        o_ref[0] = jnp.concatenate(
            [o[:, g * LANES:(g + 1) * LANES].T for g in range(NSA_GROUPS)], axis=0).astype(BF16)
    else:
        o_ref[0] = o.astype(BF16)


def _compress(r, weights, one, transpose_out, l):
    b, seq, lanes = r.shape
    n = seq // CMP_STRIDE
    oshape = (b, NSA_GROUPS * LANES, n) if transpose_out else (b, n, NSA_GROUPS * LANES)
    return pl.pallas_call(
        functools.partial(_cmp_body, transpose_out),
        grid=(b,),
        in_specs=[pl.BlockSpec((1, seq, lanes), lambda i: (i, 0, 0))] + [_layer(w, l) for w in weights]
        + [pl.BlockSpec(one.shape, lambda i: (0, 0))],
        out_specs=pl.BlockSpec((1,) + oshape[1:], lambda i: (i, 0, 0)),
        out_shape=jax.ShapeDtypeStruct(oshape, BF16),
        scratch_shapes=[pltpu.VMEM((NSA_GROUPS, n, LANES), F32)],
        compiler_params=_params(1),
        name="compress_v" if transpose_out else "compress_k",
    )(r, *weights, one)


def _nsa_body(qpt_ref, qrt_ref, gt_ref, kc_ref, vct_ref,
              kslc_ref, vslct_ref, kwin_ref, vwint_ref,
              y_ref,
              acc_ref, m_ref, selb_ref, win_ref, wmax_ref, ga_ref, gb_ref, gmax_ref, yacc_ref):
    tq = ATT_T
    hg = HEADS_PER_GROUP
    i = pl.program_id(2)
    start = i * tq
    t_row = start + lax.broadcasted_iota(jnp.int32, (1, tq), 1)

    def lane_cat(ref):
        return jnp.concatenate([ref[0, h * HEAD_DIM:(h + 1) * HEAD_DIM, :] for h in range(hg)], axis=1)

    q_pad = jnp.zeros((LANES - HEAD_DIM, hg * tq), BF16)
    qr_t = lane_cat(qrt_ref)
    qp4 = jnp.concatenate([lane_cat(qpt_ref), q_pad], axis=0)
    qr4 = jnp.concatenate([qr_t, q_pad], axis=0)

    any_c = t_row >= CMP_BLOCK - 1
    per_slc = SLC_BLOCK // CMP_STRIDE
    sub = CMP_CHUNK // per_slc
    sub_iota = lax.broadcasted_iota(jnp.int32, (sub, tq), 0)
    cur = t_row // SLC_BLOCK

    k_iota = lax.broadcasted_iota(jnp.int32, (tq, 1), 0)

    def key_rows(ref, kt):
        return ref[0, pl.ds(pl.multiple_of(kt * tq, tq), tq), :]

    win_kts = [jnp.maximum(i - back, 0) for back in range(WIN_TILES, -1, -1)]

    def win_bias(w):
        back = WIN_TILES - w
        kpos = win_kts[w] * tq + k_iota
        if back == WIN_TILES:
            ok = (kpos > t_row - WINDOW) & (i >= back)
        elif back == 0:
            ok = kpos <= t_row
        else:
            ok = jnp.broadcast_to(i >= back, (tq, tq))
        return jnp.where(ok, 0.0, NEG)

    def score_window():
        wmax = jnp.full((1, hg * tq), NEG, F32)
        for w in range(WIN_TILES + 1):
            st = _dot(key_rows(kwin_ref, win_kts[w]), qr4) + jnp.concatenate([win_bias(w)] * hg, axis=1)
            win_ref[w * tq:(w + 1) * tq, :] = st
            wmax = jnp.maximum(wmax, jnp.max(st, axis=0, keepdims=True))
        wmax_ref[0:1, :] = wmax

    def window_tile(w, wmax):
        rows = slice(w * tq, (w + 1) * tq)
        p = jnp.concatenate(
            [jnp.exp2(win_ref[rows, h * tq:(h + 1) * tq] - wmax[:, h * tq:(h + 1) * tq]).astype(BF16)
             for h in range(hg)], axis=1)
        return _dot(vwint_ref[0, win_kts[w]], p)

    def cmp_and_select(n_chunk):
        rows = n_chunk * CMP_CHUNK
        nblk = rows // per_slc
        n_last = (t_row - (CMP_BLOCK - 1)) // CMP_STRIDE
        m4 = jnp.full((1, hg * tq), NEG, F32)
        for c in range(n_chunk):
            rs = slice(c * CMP_CHUNK, (c + 1) * CMP_CHUNK)
            st = _dot(kc_ref[0, rs, :], qp4)
            if c >= n_chunk - 2:
                r_loc = lax.broadcasted_iota(jnp.int32, (CMP_CHUNK, tq), 0)
                n_idx = c * CMP_CHUNK + per_slc * (r_loc % sub) + r_loc // sub
                st = st + jnp.concatenate([jnp.where(n_idx <= n_last, 0.0, NEG)] * hg, axis=1)
            gb_ref[rs, :] = st
            m4 = jnp.maximum(m4, jnp.max(st, axis=0, keepdims=True))
        imp = jnp.zeros((nblk, tq), F32)
        for h in range(hg):
            hs = slice(h * tq, (h + 1) * tq)
            e = jnp.exp2(gb_ref[0:rows, hs] - m4[:, hs])
            oc = _dot(vct_ref[0, :, 0:rows], e.astype(BF16))
            rl = jnp.where(any_c, 1.0 / jnp.maximum(oc[HEAD_DIM:HEAD_DIM + 1, :], 1e-30), 0.0)
            yacc_ref[h * HEAD_DIM:(h + 1) * HEAD_DIM, :] = oc[0:HEAD_DIM, :] * (rl * gt_ref[0, 3 * h:3 * h + 1, :])
            carry = jnp.zeros((1, tq), F32)
            parts = []
            for c in range(n_chunk):
                run = [e[c * CMP_CHUNK + d * sub:c * CMP_CHUNK + (d + 1) * sub] for d in range(per_slc)]
                up = jnp.where(sub_iota == 0, carry, pltpu.roll(run[per_slc - 1], 1, 0))
                carry = run[per_slc - 1][sub - 1:sub]
                parts.append(sum(run[1:per_slc - 1], run[0]) + 0.5 * (run[per_slc - 1] + up))
            imp = imp + jnp.concatenate(parts, axis=0) * rl

        score_window()

        blk = lax.broadcasted_iota(jnp.int32, (nblk, tq), 0)
        blk_f = blk.astype(F32)
        forced = (blk == 0) | (blk == cur) | (blk == cur - 1)
        v = jnp.where(blk > cur, NEG, jnp.where(forced, REMOVED, imp))
        for _ in range(N_SELECT - N_FORCED):
            mx = jnp.max(v, axis=0, keepdims=True)
            first = jnp.min(jnp.where(v == mx, blk_f, float(nblk)), axis=0, keepdims=True)
            v = jnp.where(blk_f == first, REMOVED, v)
        selb_ref[0:nblk, :] = jnp.where((v == REMOVED) & (blk <= cur), 0.0, NEG)

    n_chunks_total = kc_ref.shape[1] // CMP_CHUNK
    visible = (start + tq - CMP_BLOCK) // (CMP_STRIDE * CMP_CHUNK) + 1
    for n_chunk in range(1, n_chunks_total + 1):
        pl.when(visible == n_chunk)(functools.partial(cmp_and_select, n_chunk))

    def reset():
        acc_ref[...] = jnp.zeros_like(acc_ref)
        m_ref[...] = jnp.full_like(m_ref, NEG)

    def finish(branch, acc):
        for h in range(hg):
            hs = slice(h * tq, (h + 1) * tq)
            o = acc[0:HEAD_DIM, hs] * (1.0 / acc[HEAD_DIM:HEAD_DIM + 1, hs])
            yacc_ref[h * HEAD_DIM:(h + 1) * HEAD_DIM, :] += o * gt_ref[0, 3 * h + branch:3 * h + branch + 1, :]

    def q_biased(grp):
        band = selb_ref[pl.ds(pl.multiple_of(grp * BIAS_ROWS, BIAS_ROWS), BIAS_ROWS), :].astype(BF16)
        return jnp.concatenate([qr_t, jnp.concatenate([band] * hg, axis=1),
                                q_pad[BIAS_ROWS:]], axis=0)

    def score_tile(grp, j, q4, dst_ref):
        st = _dot(key_rows(kslc_ref, grp * KT_GROUP + j), q4)
        dst_ref[j * tq:(j + 1) * tq, :] = st
        return jnp.max(st, axis=0, keepdims=True)

    def probs(scores, m):
        return jnp.concatenate(
            [jnp.exp2(scores(slice(h * tq, (h + 1) * tq)) - m[:, h * tq:(h + 1) * tq]).astype(BF16)
             for h in range(hg)], axis=1)

    def slc_group(grp, src_ref, dst_ref):
        m_old = m_ref[0:1, :]
        m_new = jnp.maximum(m_old, jnp.max(gmax_ref[0:KT_GROUP, :], axis=0, keepdims=True))
        q_next = q_biased(grp + 1)
        tmax = []
        part = jnp.zeros(acc_ref.shape, F32)
        for j in range(KT_GROUP):
            tmax.append(score_tile(grp + 1, j, q_next, dst_ref))
            p = probs(lambda hs, j=j: src_ref[j * tq:(j + 1) * tq, hs], m_new)
            part = part + _dot(vslct_ref[0, grp * KT_GROUP + j], p)
        acc_ref[...] = acc_ref[...] * jnp.exp2(m_old - m_new) + part
        m_ref[0:1, :] = m_new
        gmax_ref[0:KT_GROUP, :] = jnp.concatenate(tmax, axis=0)

    def score_first(dst_ref):
        q4 = q_biased(0)
        wmax = wmax_ref[0:1, :]
        part = jnp.zeros(acc_ref.shape, F32)
        tmax = []
        for j in range(KT_GROUP):
            tmax.append(score_tile(0, j, q4, dst_ref))
            if j <= WIN_TILES:
                part = part + window_tile(j, wmax)
        gmax_ref[0:KT_GROUP, :] = jnp.concatenate(tmax, axis=0)
        finish(2, part)

    reset()
    n_grp = i // KT_GROUP
    odd = n_grp % 2

    @pl.when(odd == 0)
    def _():
        score_first(ga_ref)

    @pl.when(odd == 1)
    def _():
        score_first(gb_ref)
        slc_group(0, gb_ref, ga_ref)

    def slc_pair(pair, carry):
        slc_group(2 * pair + odd, ga_ref, gb_ref)
        slc_group(2 * pair + odd + 1, gb_ref, ga_ref)
        return carry

    lax.fori_loop(0, n_grp // 2, slc_pair, 0)

    n_rest = i - n_grp * KT_GROUP
    causal = jnp.where((start + k_iota) <= t_row, 0.0, NEG)
    diag_rows = pl.ds(pl.multiple_of(n_rest * tq, tq), tq)
    diag = lambda hs: ga_ref[diag_rows, hs] + causal
    m_old = m_ref[0:1, :]
    m_new = jnp.maximum(m_old, jnp.concatenate(
        [jnp.max(diag(slice(h * tq, (h + 1) * tq)), axis=0, keepdims=True) for h in range(hg)], axis=1))
    for j in range(KT_GROUP - 1):
        m_new = jnp.maximum(m_new, jnp.where(j < n_rest, gmax_ref[j:j + 1, :], NEG))
    acc_ref[...] = acc_ref[...] * jnp.exp2(m_old - m_new) + _dot(vslct_ref[0, i], probs(diag, m_new))

    def rest_tile(j, carry):
        rows = pl.ds(pl.multiple_of(j * tq, tq), tq)
        acc_ref[...] += _dot(vslct_ref[0, n_grp * KT_GROUP + j], probs(lambda hs: ga_ref[rows, hs], m_new))
        return carry

    lax.fori_loop(0, n_rest, rest_tile, 0)
    finish(1, acc_ref)

    y_ref[0] = yacc_ref[...].T.astype(BF16)


def _nsa(qpt, qrt, gt, kc, vct, kslc, vslct, kwin, vwint):
    b, _, seq = qpt.shape
    tq = ATT_T
    hg = HEADS_PER_GROUP
    n_cmp = kc.shape[1]
    nkt = seq // tq
    qspec = pl.BlockSpec((1, hg * HEAD_DIM, tq), lambda bi, g, i: (bi, g, i))
    in_specs = [
        qspec, qspec,
        pl.BlockSpec((1, LANES // NSA_GROUPS, tq), lambda bi, g, i: (bi, g, i)),
        pl.BlockSpec((1, n_cmp, LANES), lambda bi, g, i: (bi, 0, g)),
        pl.BlockSpec((1, LANES, n_cmp), lambda bi, g, i: (bi, g, 0)),
        _resident((1, seq, LANES), lambda bi, g, i: (bi, 0, g)),
        _resident((1, nkt, None, VT_ROWS, tq), lambda bi, g, i: (bi, 0, g, 0, 0)),
        _resident((1, seq, LANES), lambda bi, g, i: (bi, 0, g)),
        _resident((1, nkt, None, VT_ROWS, tq), lambda bi, g, i: (bi, 0, g, 0, 0)),
    ]
    return pl.pallas_call(
        _nsa_body,
        grid=(b, NSA_GROUPS, seq // tq),
        in_specs=in_specs,
        out_specs=pl.BlockSpec((1, tq, hg * HEAD_DIM), lambda bi, g, i: (bi, i, g)),
        out_shape=jax.ShapeDtypeStruct((b, seq, NSA_HEADS * HEAD_DIM), BF16),
        scratch_shapes=[pltpu.VMEM((VT_ROWS, hg * tq), F32),
                        pltpu.VMEM((8, hg * tq), F32),
                        pltpu.VMEM((seq // SLC_BLOCK, tq), F32),
                        pltpu.VMEM(((WIN_TILES + 1) * tq, hg * tq), F32),
                        pltpu.VMEM((8, hg * tq), F32),
                        pltpu.VMEM((KT_GROUP * tq, hg * tq), F32),
                        pltpu.VMEM((KT_GROUP * tq, hg * tq), F32),
                        pltpu.VMEM((8, hg * tq), F32),
                        pltpu.VMEM((hg * HEAD_DIM, tq), F32)],
        compiler_params=_params(3),
        name="nsa",
    )(qpt, qrt, gt, kc, vct, kslc, vslct, kwin, vwint)


def _merge_ffn_body(x_ref, part_ref, gnsa_ref, ynsa_ref, wbn_ref, wo_ref, gmix_ref,
                    gpre_ref, gpost_ref, wa_ref, wb_ref, wout_ref, o_ref):
    merged = part_ref[...] + gnsa_ref[...].astype(F32) * _dot(ynsa_ref[...], wbn_ref[...])
    x = x_ref[...] + _rms(_dot(merged.astype(BF16), wo_ref[...]), gmix_ref[...])
    h = _rms(x, gpre_ref[...]).astype(BF16)
    y = jnp.zeros(x.shape, F32)
    for c0, c1 in zip(MERGE_FF_SPLITS[:-1], MERGE_FF_SPLITS[1:]):
        cs = slice(c0, c1)
        a = _dot(h, wa_ref[:, cs])
        g = (a * jax.nn.sigmoid(a) * _dot(h, wb_ref[:, cs])).astype(BF16)
        y = y + _dot(g, wout_ref[cs, :])
    o_ref[...] = x + 0.5 * _rms(y, gpost_ref[...])


def _merge_ffn(x, part, gnsa, ynsa, l, *params):
    t = x.shape[0]
    tm = MERGE_TM
    row = lambda i: (i, 0)
    return pl.pallas_call(
        _merge_ffn_body,
        grid=(t // tm,),
        in_specs=[pl.BlockSpec((tm, D_MODEL), row), pl.BlockSpec((tm, D_MODEL), row),
                  pl.BlockSpec((tm, D_MODEL), row), pl.BlockSpec((tm, NSA_HEADS * HEAD_DIM), row)]
        + [_layer(p, l) for p in params],
        out_specs=pl.BlockSpec((tm, D_MODEL), row),
        out_shape=jax.ShapeDtypeStruct((t, D_MODEL), F32),
        compiler_params=_params(1),
        name="merge_ffn",
    )(x, part, gnsa, ynsa, *params)


def _pad_groups(w, groups, width, stride):
    d = w.shape[0]
    w = w.reshape(d, groups, width)
    return jnp.pad(w, ((0, 0), (0, 0), (0, stride - width))).reshape(d, groups * stride)


def _mix_weight(w):
    o = 0
    u = w[:, o:o + CONV_DIM]; o += CONV_DIM
    bg = w[:, o:o + CONV_DIM]; o += CONV_DIM
    cg = w[:, o:o + CONV_DIM]; o += CONV_DIM
    q = w[:, o:o + NSA_HEADS * HEAD_DIM]; o += NSA_HEADS * HEAD_DIM
    kv = w[:, o:o + 6 * NSA_GROUPS * HEAD_DIM].reshape(D_MODEL, 6, NSA_GROUPS * HEAD_DIM)
    o += 6 * NSA_GROUPS * HEAD_DIM
    gate = w[:, o:o + 3 * NSA_HEADS]; o += 3 * NSA_HEADS
    qmem = w[:, o:o + MEM_HEADS * MEM_HEAD_DIM]; o += MEM_HEADS * MEM_HEAD_DIM
    merge = w[:, o:]
    cols = [u, bg, cg, q, kv[:, 2], kv[:, 4], kv[:, 3], kv[:, 5], kv[:, 0], kv[:, 1],
            _pad_groups(gate, NSA_GROUPS, 3 * HEADS_PER_GROUP, LANES // NSA_GROUPS), qmem, merge]
    out = jnp.concatenate(cols, axis=1).astype(BF16)
    assert out.shape == (D_MODEL, C_TOTAL)
    return out


def _cmp_weights(pos_emb, w1, b1, w2):
    eye = jnp.eye(NSA_GROUPS, dtype=F32)
    w1r = w1.reshape(CMP_BLOCK, HEAD_DIM, CMP_HIDDEN)
    width = CMP_STRIDE * NSA_GROUPS * HEAD_DIM

    def spread(a):
        return jnp.einsum('pdj,gh->pgdhj', a, eye).reshape(width, NSA_GROUPS * CMP_HIDDEN).astype(BF16)

    def pe_row(p):
        return jnp.broadcast_to(p[:, None, :], (CMP_STRIDE, NSA_GROUPS, HEAD_DIM)).reshape(1, width)

    pe = jnp.concatenate([pe_row(pos_emb[:CMP_STRIDE]), pe_row(pos_emb[CMP_STRIDE:]),
                          jnp.zeros((6, width), F32)], axis=0)
    w2p = jnp.pad(w2, ((0, 0), (0, LANES - HEAD_DIM)))
    w2b = jnp.einsum('jc,gh->gjhc', w2p, eye).reshape(NSA_GROUPS * CMP_HIDDEN, NSA_GROUPS * LANES)
    return (pe, spread(w1r[:CMP_STRIDE]), spread(w1r[CMP_STRIDE:]),
            jnp.tile(b1, NSA_GROUPS)[None, :], w2b.astype(BF16))


def _lane_table():
    half = ROT_DIM // 2
    lane = np.arange(LANES)
    tab = jnp.zeros((8, LANES), F32)
    d = lane % HEAD_DIM
    tab = tab.at[1].set(jnp.asarray(np.where(d < half, -1.0, 0.0), F32))
    tab = tab.at[2].set(jnp.asarray(np.where((d >= half) & (d < ROT_DIM), 1.0, 0.0), F32))
    tab = tab.at[3].set(jnp.asarray(np.where(lane == HEAD_DIM, 1.0, 0.0), F32))
    return tab


def kernel(x, mem, positions, ffn1_norm_pre, ffn1_norm_post, ffn1_w_in, ffn1_w_out, mix_norm_pre, mix_norm_post, mem_norm, w_mix_in, conv_w, cmp_pos_k, cmp_pos_v, cmp_k_w1, cmp_k_b1, cmp_k_w2, cmp_v_w1, cmp_v_b1, cmp_v_w2, w_mem_kv, w_branch_conv, w_branch_nsa, w_branch_mem, w_mix_out, ffn2_norm_pre, ffn2_norm_post, ffn2_w_in, ffn2_w_out):
    batch, seq, _ = x.shape
    t = batch * seq
    assert x.shape[-1] == D_MODEL and seq % MIX_TM == 0 and t % FFN_TM == 0 and t % MERGE_TM == 0
    assert MIX_TM % ATT_T == 0 and (seq // CMP_STRIDE) % CMP_CHUNK == 0 and (seq // ATT_T) % KT_GROUP == 0
    assert KT_GROUP > WIN_TILES and KT_GROUP * ATT_T >= seq // CMP_STRIDE
    xf = x.reshape(t, D_MODEL)
    cos_t, sin_t = _rope_trig(positions)
    tab = _lane_table()
    one = jnp.tile(tab[3:4, :], (1, NSA_GROUPS))

    vec = lambda g: g[:, None, :]
    cast = lambda w: w.astype(BF16)
    ffn1 = (vec(ffn1_norm_pre), vec(ffn1_norm_post), cast(ffn1_w_in[:, :, :D_FF]), cast(ffn1_w_in[:, :, D_FF:]),
            cast(ffn1_w_out))
    merge_ffn2 = (cast(w_branch_nsa), cast(w_mix_out), vec(mix_norm_post), vec(ffn2_norm_pre), vec(ffn2_norm_post),
                  cast(ffn2_w_in[:, :, :D_FF]), cast(ffn2_w_in[:, :, D_FF:]), cast(ffn2_w_out))
    w_main = jax.vmap(_mix_weight)(w_mix_in)
    convw = jnp.pad(conv_w, ((0, 0), (0, 8 - CONV_WIDTH), (0, 0)))
    cmp_k = jax.vmap(_cmp_weights)(cmp_pos_k, cmp_k_w1, cmp_k_b1, cmp_k_w2)
    cmp_v = jax.vmap(_cmp_weights)(cmp_pos_v, cmp_v_w1, cmp_v_b1, cmp_v_w2)
    mem_g, mem_w = vec(mem_norm), cast(w_mem_kv)
    mix_g, wbc, wbm = vec(mix_norm_pre), cast(w_branch_conv), cast(w_branch_mem)

    for l in range(DEPTH):
        xf = _ffn(xf, l, *ffn1)
        mk, mv = _memkv(mem, mem_g, mem_w, l)
        (part, gnsa, qpt, qrt, kslc, kwin, vslct, vwint, kcr, vcr, gt) = _mixin(
            xf, cos_t, sin_t, mix_g, tab, w_main, convw, mk, mv, wbc, wbm, batch, seq, l)
        kc = _compress(kcr.reshape(batch, seq, LANES), cmp_k, one, False, l)
        vct = _compress(vcr.reshape(batch, seq, LANES), cmp_v, one, True, l)
        ynsa = _nsa(qpt, qrt, gt, kc, vct,
                    kslc.reshape(batch, seq, NSA_GROUPS * LANES), vslct,
                    kwin.reshape(batch, seq, NSA_GROUPS * LANES), vwint)
        xf = _merge_ffn(xf, part, gnsa, ynsa.reshape(t, NSA_HEADS * HEAD_DIM), l, *merge_ffn2)
    return xf.reshape(batch, seq, D_MODEL)
```

```python
import functools

import jax
import jax.numpy as jnp
import numpy as np
from jax import lax
from jax.experimental import pallas as pl
from jax.experimental.pallas import tpu as pltpu

D_MODEL = 1024
DEPTH = 2
CONV_DIM = 512
CONV_WIDTH = 3
NSA_HEADS = 8
NSA_GROUPS = 2
HEADS_PER_GROUP = NSA_HEADS // NSA_GROUPS
HEAD_DIM = 64
ROT_DIM = HEAD_DIM // 4
ROPE_THETA = 500000.0
CMP_BLOCK = 32
CMP_STRIDE = 16
CMP_HIDDEN = 256
SLC_BLOCK = 64
N_SELECT = 16
WINDOW = 512
MEM_HEADS = 4
MEM_HEAD_DIM = 128
D_FF = 2816
EPS = 1e-6
NEG = -1e30
REMOVED = -3e38

LANES = 128
F32 = jnp.float32
BF16 = jnp.bfloat16

FFN_TM = 512
MIX_TM = 512
MERGE_TM = 512
MERGE_FF_SPLITS = (0, 1536, D_FF)
HALO = 8
TRIG_ROWS = 512
ATT_T = 256
WIN_TILES = WINDOW // ATT_T
KT_GROUP = 4
BIAS_ROWS = KT_GROUP * ATT_T // SLC_BLOCK
VT_ROWS = HEAD_DIM + BIAS_ROWS
LOG2_E = 1.4426950408889634
CMP_CHUNK = 256
N_FORCED = 3

C_U = 0
C_BG = C_U + CONV_DIM
C_CG = C_BG + CONV_DIM
C_Q = C_CG + CONV_DIM
C_KSLC = C_Q + NSA_HEADS * HEAD_DIM
C_KWIN = C_KSLC + LANES
C_VSLC = C_KWIN + LANES
C_VWIN = C_VSLC + LANES
C_KCR = C_VWIN + LANES
C_VCR = C_KCR + LANES
C_GATE = C_VCR + LANES
C_QMEM = C_GATE + LANES
C_MERGE = C_QMEM + MEM_HEADS * MEM_HEAD_DIM
C_TOTAL = C_MERGE + 3 * D_MODEL

VMEM_LIMIT = 56 * 1024 * 1024


def _params(n_axes):
    return pltpu.CompilerParams(dimension_semantics=("arbitrary",) * n_axes,
                                vmem_limit_bytes=VMEM_LIMIT)


def _resident(shape, index_map):
    return pl.BlockSpec(shape, index_map, pipeline_mode=pl.Buffered(1))


def _layer(arr, l):
    zeros = (0,) * (arr.ndim - 1)
    return pl.BlockSpec((None,) + arr.shape[1:], lambda *_: (l,) + zeros, pipeline_mode=pl.Buffered(1))


def _rms(x, g):
    return x * lax.rsqrt(jnp.mean(x * x, axis=-1, keepdims=True) + EPS) * g


def _dot(a, b):
    return jnp.dot(a, b, preferred_element_type=F32)


def _ffn_body(x_ref, gpre_ref, gpost_ref, wa_ref, wb_ref, wo_ref, o_ref):
    x = x_ref[...]
    h = _rms(x, gpre_ref[...]).astype(BF16)
    a = _dot(h, wa_ref[...])
    b = _dot(h, wb_ref[...])
    g = (a * jax.nn.sigmoid(a) * b).astype(BF16)
    y = _dot(g, wo_ref[...])
    o_ref[...] = x + 0.5 * _rms(y, gpost_ref[...])


def _ffn(x, l, *params):
    t = x.shape[0]
    row = lambda i: (i, 0)
    return pl.pallas_call(
        _ffn_body,
        grid=(t // FFN_TM,),
        in_specs=[pl.BlockSpec((FFN_TM, D_MODEL), row)] + [_layer(p, l) for p in params],
        out_specs=pl.BlockSpec((FFN_TM, D_MODEL), row),
        out_shape=jax.ShapeDtypeStruct((t, D_MODEL), F32),
        compiler_params=_params(1),
        name="ffn",
    )(x, *params)


def _memkv_body(mem_ref, g_ref, w_ref, mk_ref, mv_ref):
    h = _rms(mem_ref[0], g_ref[...]).astype(BF16)
    kv = _dot(h, w_ref[...])
    half = MEM_HEADS * MEM_HEAD_DIM
    mk_ref[0] = kv[:, :half].astype(BF16)
    mv_ref[0] = kv[:, half:].astype(BF16)


def _memkv(mem, g, w, l):
    b, m, _ = mem.shape
    half = MEM_HEADS * MEM_HEAD_DIM
    out = jax.ShapeDtypeStruct((b, m, half), BF16)
    return pl.pallas_call(
        _memkv_body,
        grid=(b,),
        in_specs=[pl.BlockSpec((1, m, D_MODEL), lambda i: (i, 0, 0)), _layer(g, l), _layer(w, l)],
        out_specs=[pl.BlockSpec((1, m, half), lambda i: (i, 0, 0))] * 2,
        out_shape=[out, out],
        compiler_params=_params(1),
        name="memkv",
    )(mem, g, w)


def _trig_body(pos_ref, freq_ref, cos_ref, sin_ref):
    ang = pos_ref[...].astype(F32) * freq_ref[...]
    cos_ref[...] = jnp.cos(ang)
    sin_ref[...] = jnp.sin(ang)


def _rope_trig(positions):
    t = positions.size
    per_row = LANES // ROT_DIM
    rows = t // per_row
    tr = min(rows, TRIG_ROWS)
    inv_freq = ROPE_THETA ** (-jnp.arange(0, ROT_DIM, 2, dtype=jnp.float32) / ROT_DIM)
    freq = jnp.tile(jnp.tile(inv_freq, 2), per_row)[None, :]
    pos = jnp.repeat(positions.reshape(rows, per_row), ROT_DIM, axis=1)
    spec = pl.BlockSpec((tr, LANES), lambda i: (i, 0))
    packed = jax.ShapeDtypeStruct((rows, LANES), F32)
    cos_p, sin_p = pl.pallas_call(
        _trig_body,
        grid=(rows // tr,),
        in_specs=[spec, pl.BlockSpec((1, LANES), lambda i: (0, 0))],
        out_specs=[spec, spec],
        out_shape=[packed, packed],
        compiler_params=_params(1),
        name="rope_trig",
    )(pos, freq)
    unpack = lambda x: jnp.pad(x.reshape(t, ROT_DIM), ((0, 0), (0, LANES - ROT_DIM)))
    return unpack(cos_p), unpack(sin_p)


def _mix_body(tiles_per_batch,
              x_ref, halo_ref, cos_ref, sin_ref, gpre_ref, tab_ref, w_ref, convw_ref, mk_ref, mv_ref,
              wbc_ref, wbm_ref,
              part_ref, gnsa_ref, qpt_ref, qrt_ref, kslc_ref, kwin_ref, vslct_ref, vwint_ref,
              kcr_ref, vcr_ref, gatest_ref,
              cu_scr):
    tm = MIX_TM
    i = pl.program_id(0)
    gpre = gpre_ref[...]
    h = _rms(x_ref[...], gpre).astype(BF16)
    hh = _rms(halo_ref[...], gpre).astype(BF16)

    def proj(hv, c0, width):
        return _dot(hv, w_ref[:, c0:c0 + width])

    cu = proj(h, C_CG, CONV_DIM) * proj(h, C_U, CONV_DIM)
    cu_h = proj(hh, C_CG, CONV_DIM) * proj(hh, C_U, CONV_DIM)
    first = (i % tiles_per_batch) == 0
    cu_scr[0:HALO, :] = jnp.where(first, 0.0, cu_h)
    cu_scr[HALO:HALO + tm, :] = cu
    cw = convw_ref[...]
    conv = (cu * cw[2:3, :]
            + cu_scr[HALO - 1:HALO - 1 + tm, :] * cw[1:2, :]
            + cu_scr[HALO - 2:HALO - 2 + tm, :] * cw[0:1, :])
    y_conv = (proj(h, C_BG, CONV_DIM) * conv).astype(BF16)

    lane_t = lax.broadcasted_iota(jnp.int32, (tm, LANES), 1)
    rot_lo = lane_t < ROT_DIM
    rot_hi = (lane_t >= HEAD_DIM) & (lane_t < HEAD_DIM + ROT_DIM)
    spread = lambda v, fill: jnp.where(rot_lo, v, jnp.where(rot_hi, pltpu.roll(v, HEAD_DIM, 1), fill))
    cos_t = spread(cos_ref[...], 1.0)
    sin_t = spread(sin_ref[...], 0.0)
    sin_a = sin_t * tab_ref[1:2, :]
    sin_b = sin_t * tab_ref[2:3, :]

    def rope(t):
        half = ROT_DIM // 2
        return (t * cos_t + pltpu.roll(t, LANES - half, 1) * sin_a
                + pltpu.roll(t, half, 1) * sin_b)

    wide = proj(h, C_Q, C_MERGE - C_Q)

    def nsa_cols(c0):
        return wide[:, c0 - C_Q:c0 - C_Q + LANES]

    q_tiles = [slice(j * ATT_T, (j + 1) * ATT_T) for j in range(tm // ATT_T)]
    scale = HEAD_DIM ** -0.5 * LOG2_E
    for pair in range(NSA_HEADS // 2):
        q = nsa_cols(C_Q + pair * LANES) * scale
        sl = slice(pair * LANES, (pair + 1) * LANES)
        for rows in q_tiles:
            qpt_ref[0, sl, rows] = q[rows].T.astype(BF16)
            qrt_ref[0, sl, rows] = rope(q)[rows].T.astype(BF16)

    seq_row = (i % tiles_per_batch) * tm + lax.broadcasted_iota(jnp.int32, (tm, LANES), 0)
    lane = lax.broadcasted_iota(jnp.int32, (tm, LANES), 1)
    blk_hot = jnp.where(lane - HEAD_DIM == (seq_row // SLC_BLOCK) % BIAS_ROWS, 1.0, 0.0)

    def group_slabs(x):
        return [jnp.where(lane < HEAD_DIM, y, 0.0) for y in (x, pltpu.roll(x, HEAD_DIM, 1))]

    for g, (ks, kw) in enumerate(zip(group_slabs(rope(nsa_cols(C_KSLC))), group_slabs(rope(nsa_cols(C_KWIN))))):
        sl = slice(g * LANES, (g + 1) * LANES)
        kslc_ref[:, sl] = (ks + blk_hot).astype(BF16)
        kwin_ref[:, sl] = kw.astype(BF16)

    ones_rows = jnp.where(lax.broadcasted_iota(jnp.int32, (VT_ROWS - HEAD_DIM, ATT_T), 0) == 0, 1.0, 0.0)
    vs, vw = nsa_cols(C_VSLC), nsa_cols(C_VWIN)
    gt = jax.nn.sigmoid(nsa_cols(C_GATE))
    for j, rows in enumerate(q_tiles):
        vs_t, vw_t = vs[rows].T, vw[rows].T
        for g in range(NSA_GROUPS):
            band = slice(g * HEAD_DIM, (g + 1) * HEAD_DIM)
            vslct_ref[0, j, g] = jnp.concatenate([vs_t[band], ones_rows], axis=0).astype(BF16)
            vwint_ref[0, j, g] = jnp.concatenate([vw_t[band], ones_rows], axis=0).astype(BF16)
        gatest_ref[0, :, rows] = gt[rows].T
    kcr_ref[...] = nsa_cols(C_KCR)
    vcr_ref[...] = nsa_cols(C_VCR)

    mem_scale = MEM_HEAD_DIM ** -0.5
    outs = []
    for hd in range(MEM_HEADS):
        sl = slice(hd * MEM_HEAD_DIM, (hd + 1) * MEM_HEAD_DIM)
        qm = (nsa_cols(C_QMEM + hd * MEM_HEAD_DIM) * mem_scale).astype(BF16)
        s = lax.dot_general(qm, mk_ref[0, :, sl], (((1,), (1,)), ((), ())),
                            preferred_element_type=F32)
        p = jnp.exp(s - jnp.max(s, axis=-1, keepdims=True))
        l = jnp.sum(p, axis=-1, keepdims=True)
        outs.append((_dot(p.astype(BF16), mv_ref[0, :, sl]) / l).astype(BF16))
    y_mem = jnp.concatenate(outs, axis=1)

    g_conv = jax.nn.sigmoid(proj(h, C_MERGE, D_MODEL))
    g_nsa = jax.nn.sigmoid(proj(h, C_MERGE + D_MODEL, D_MODEL))
    g_mem = jax.nn.sigmoid(proj(h, C_MERGE + 2 * D_MODEL, D_MODEL))
    part_ref[...] = g_conv * _dot(y_conv, wbc_ref[...]) + g_mem * _dot(y_mem, wbm_ref[...])
    gnsa_ref[...] = g_nsa.astype(BF16)


def _mixin(x, cos_t, sin_t, gpre, tab, w_main, convw, mk, mv, wbc, wbm, batch, seq, l):
    t = batch * seq
    tm = MIX_TM
    tpb = seq // tm
    nt = tm // ATT_T
    row = lambda i: (i, 0)
    fix = lambda i: (0, 0)
    bt = lambda i: (i // tpb, 0, i % tpb)
    mem_len = mk.shape[1]
    half = MEM_HEADS * MEM_HEAD_DIM
    out_shape = [
        jax.ShapeDtypeStruct((t, D_MODEL), F32),
        jax.ShapeDtypeStruct((t, D_MODEL), BF16),
        jax.ShapeDtypeStruct((batch, NSA_HEADS * HEAD_DIM, seq), BF16),
        jax.ShapeDtypeStruct((batch, NSA_HEADS * HEAD_DIM, seq), BF16),
        jax.ShapeDtypeStruct((t, NSA_GROUPS * LANES), BF16),
        jax.ShapeDtypeStruct((t, NSA_GROUPS * LANES), BF16),
        jax.ShapeDtypeStruct((batch, seq // ATT_T, NSA_GROUPS, VT_ROWS, ATT_T), BF16),
        jax.ShapeDtypeStruct((batch, seq // ATT_T, NSA_GROUPS, VT_ROWS, ATT_T), BF16),
        jax.ShapeDtypeStruct((t, LANES), F32),
        jax.ShapeDtypeStruct((t, LANES), F32),
        jax.ShapeDtypeStruct((batch, LANES, seq), F32),
    ]
    vt_spec = pl.BlockSpec((1, nt, NSA_GROUPS, VT_ROWS, ATT_T), lambda i: (i // tpb, i % tpb, 0, 0, 0))
    out_specs = [
        pl.BlockSpec((tm, D_MODEL), row),
        pl.BlockSpec((tm, D_MODEL), row),
        pl.BlockSpec((1, NSA_HEADS * HEAD_DIM, tm), bt),
        pl.BlockSpec((1, NSA_HEADS * HEAD_DIM, tm), bt),
        pl.BlockSpec((tm, NSA_GROUPS * LANES), row),
        pl.BlockSpec((tm, NSA_GROUPS * LANES), row),
        vt_spec, vt_spec,
        pl.BlockSpec((tm, LANES), row),
        pl.BlockSpec((tm, LANES), row),
        pl.BlockSpec((1, LANES, tm), bt),
    ]
    in_specs = [
        pl.BlockSpec((tm, D_MODEL), row),
        pl.BlockSpec((HALO, D_MODEL), lambda i: (jnp.maximum(i * (tm // HALO) - 1, 0), 0)),
        pl.BlockSpec((tm, LANES), row),
        pl.BlockSpec((tm, LANES), row),
        _layer(gpre, l),
        _resident((8, LANES), fix),
        _layer(w_main, l),
        _layer(convw, l),
        pl.BlockSpec((1, mem_len, half), lambda i: (i // tpb, 0, 0)),
        pl.BlockSpec((1, mem_len, half), lambda i: (i // tpb, 0, 0)),
        _layer(wbc, l),
        _layer(wbm, l),
    ]
    return pl.pallas_call(
        functools.partial(_mix_body, tpb),
        grid=(t // tm,),
        in_specs=in_specs,
        out_specs=out_specs,
        out_shape=out_shape,
        scratch_shapes=[pltpu.VMEM((HALO + tm, CONV_DIM), F32)],
        compiler_params=_params(1),
        name="mixin",
    )(x, x, cos_t, sin_t, gpre, tab, w_main, convw, mk, mv, wbc, wbm)


def _cmp_body(transpose_out, r_ref, pe_ref, w1a_ref, w1b_ref, b1_ref, w2_ref, one_ref, o_ref, o_scr):
    n = r_ref.shape[1] // CMP_STRIDE
    r = jnp.concatenate([r_ref[0, pl.ds(p, n, stride=CMP_STRIDE), :] for p in range(CMP_STRIDE)], axis=1)
    a = _dot((r + pe_ref[0:1, :]).astype(BF16), w1a_ref[...])
    b = _dot((r + pe_ref[1:2, :]).astype(BF16), w1b_ref[...])
    hid = a + pltpu.roll(b, n - 1, 0) + b1_ref[...]
    o = _dot(jax.nn.gelu(hid).astype(BF16), w2_ref[...])
    if transpose_out:
        o = o + one_ref[...]
    per_slc = SLC_BLOCK // CMP_STRIDE
    sub = CMP_CHUNK // per_slc
    for g in range(NSA_GROUPS):
        o_scr[g] = o[:, g * LANES:(g + 1) * LANES]
    o = jnp.concatenate(
        [jnp.concatenate([o_scr[g, pl.ds(c * CMP_CHUNK + q, sub, stride=per_slc), :]
                          for c in range(n // CMP_CHUNK) for q in range(per_slc)], axis=0)
         for g in range(NSA_GROUPS)], axis=1)
    if transpose_out:
        o_ref[0] = jnp.concatenate(
            [o[:, g * LANES:(g + 1) * LANES].T for g in range(NSA_GROUPS)], axis=0).astype(BF16)
    else:
        o_ref[0] = o.astype(BF16)


def _compress(r, weights, one, transpose_out, l):
    b, seq, lanes = r.shape
    n = seq // CMP_STRIDE
    oshape = (b, NSA_GROUPS * LANES, n) if transpose_out else (b, n, NSA_GROUPS * LANES)
    return pl.pallas_call(
        functools.partial(_cmp_body, transpose_out),
        grid=(b,),
        in_specs=[pl.BlockSpec((1, seq, lanes), lambda i: (i, 0, 0))] + [_layer(w, l) for w in weights]
        + [pl.BlockSpec(one.shape, lambda i: (0, 0))],
        out_specs=pl.BlockSpec((1,) + oshape[1:], lambda i: (i, 0, 0)),
        out_shape=jax.ShapeDtypeStruct(oshape, BF16),
        scratch_shapes=[pltpu.VMEM((NSA_GROUPS, n, LANES), F32)],
        compiler_params=_params(1),
        name="compress_v" if transpose_out else "compress_k",
    )(r, *weights, one)


def _nsa_body(qpt_ref, qrt_ref, gt_ref, kc_ref, vct_ref,
              kslc_ref, vslct_ref, kwin_ref, vwint_ref,
              y_ref,
              acc_ref, m_ref, selb_ref, win_ref, wmax_ref, ga_ref, gb_ref, gmax_ref, yacc_ref):
    tq = ATT_T
    hg = HEADS_PER_GROUP
    i = pl.program_id(2)
    start = i * tq
    t_row = start + lax.broadcasted_iota(jnp.int32, (1, tq), 1)

    def lane_cat(ref):
        return jnp.concatenate([ref[0, h * HEAD_DIM:(h + 1) * HEAD_DIM, :] for h in range(hg)], axis=1)

    q_pad = jnp.zeros((LANES - HEAD_DIM, hg * tq), BF16)
    qr_t = lane_cat(qrt_ref)
    qp4 = jnp.concatenate([lane_cat(qpt_ref), q_pad], axis=0)
    qr4 = jnp.concatenate([qr_t, q_pad], axis=0)

    any_c = t_row >= CMP_BLOCK - 1
    per_slc = SLC_BLOCK // CMP_STRIDE
    sub = CMP_CHUNK // per_slc
    sub_iota = lax.broadcasted_iota(jnp.int32, (sub, tq), 0)
    cur = t_row // SLC_BLOCK

    k_iota = lax.broadcasted_iota(jnp.int32, (tq, 1), 0)

    def key_rows(ref, kt):
        return ref[0, pl.ds(pl.multiple_of(kt * tq, tq), tq), :]

    win_kts = [jnp.maximum(i - back, 0) for back in range(WIN_TILES, -1, -1)]

    def win_bias(w):
        back = WIN_TILES - w
        kpos = win_kts[w] * tq + k_iota
        if back == WIN_TILES:
            ok = (kpos > t_row - WINDOW) & (i >= back)
        elif back == 0:
            ok = kpos <= t_row
        else:
            ok = jnp.broadcast_to(i >= back, (tq, tq))
        return jnp.where(ok, 0.0, NEG)

    def score_window():
        wmax = jnp.full((1, hg * tq), NEG, F32)
        for w in range(WIN_TILES + 1):
            st = _dot(key_rows(kwin_ref, win_kts[w]), qr4) + jnp.concatenate([win_bias(w)] * hg, axis=1)
            win_ref[w * tq:(w + 1) * tq, :] = st
            wmax = jnp.maximum(wmax, jnp.max(st, axis=0, keepdims=True))
        wmax_ref[0:1, :] = wmax

    def window_tile(w, wmax):
        rows = slice(w * tq, (w + 1) * tq)
        p = jnp.concatenate(
            [jnp.exp2(win_ref[rows, h * tq:(h + 1) * tq] - wmax[:, h * tq:(h + 1) * tq]).astype(BF16)
             for h in range(hg)], axis=1)
        return _dot(vwint_ref[0, win_kts[w]], p)

    def cmp_and_select(n_chunk):
        rows = n_chunk * CMP_CHUNK
        nblk = rows // per_slc
        n_last = (t_row - (CMP_BLOCK - 1)) // CMP_STRIDE
        m4 = jnp.full((1, hg * tq), NEG, F32)
        for c in range(n_chunk):
            rs = slice(c * CMP_CHUNK, (c + 1) * CMP_CHUNK)
            st = _dot(kc_ref[0, rs, :], qp4)
            if c >= n_chunk - 2:
                r_loc = lax.broadcasted_iota(jnp.int32, (CMP_CHUNK, tq), 0)
                n_idx = c * CMP_CHUNK + per_slc * (r_loc % sub) + r_loc // sub
                st = st + jnp.concatenate([jnp.where(n_idx <= n_last, 0.0, NEG)] * hg, axis=1)
            gb_ref[rs, :] = st
            m4 = jnp.maximum(m4, jnp.max(st, axis=0, keepdims=True))
        imp = jnp.zeros((nblk, tq), F32)
        for h in range(hg):
            hs = slice(h * tq, (h + 1) * tq)
            e = jnp.exp2(gb_ref[0:rows, hs] - m4[:, hs])
            oc = _dot(vct_ref[0, :, 0:rows], e.astype(BF16))
            rl = jnp.where(any_c, 1.0 / jnp.maximum(oc[HEAD_DIM:HEAD_DIM + 1, :], 1e-30), 0.0)
            yacc_ref[h * HEAD_DIM:(h + 1) * HEAD_DIM, :] = oc[0:HEAD_DIM, :] * (rl * gt_ref[0, 3 * h:3 * h + 1, :])
            carry = jnp.zeros((1, tq), F32)
            parts = []
            for c in range(n_chunk):
                run = [e[c * CMP_CHUNK + d * sub:c * CMP_CHUNK + (d + 1) * sub] for d in range(per_slc)]
                up = jnp.where(sub_iota == 0, carry, pltpu.roll(run[per_slc - 1], 1, 0))
                carry = run[per_slc - 1][sub - 1:sub]
                parts.append(sum(run[1:per_slc - 1], run[0]) + 0.5 * (run[per_slc - 1] + up))
            imp = imp + jnp.concatenate(parts, axis=0) * rl

        score_window()

        blk = lax.broadcasted_iota(jnp.int32, (nblk, tq), 0)
        blk_f = blk.astype(F32)
        forced = (blk == 0) | (blk == cur) | (blk == cur - 1)
        v = jnp.where(blk > cur, NEG, jnp.where(forced, REMOVED, imp))
        for _ in range(N_SELECT - N_FORCED):
            mx = jnp.max(v, axis=0, keepdims=True)
            first = jnp.min(jnp.where(v == mx, blk_f, float(nblk)), axis=0, keepdims=True)
            v = jnp.where(blk_f == first, REMOVED, v)
        selb_ref[0:nblk, :] = jnp.where((v == REMOVED) & (blk <= cur), 0.0, NEG)

    n_chunks_total = kc_ref.shape[1] // CMP_CHUNK
    visible = (start + tq - CMP_BLOCK) // (CMP_STRIDE * CMP_CHUNK) + 1
    for n_chunk in range(1, n_chunks_total + 1):
        pl.when(visible == n_chunk)(functools.partial(cmp_and_select, n_chunk))

    def reset():
        acc_ref[...] = jnp.zeros_like(acc_ref)
        m_ref[...] = jnp.full_like(m_ref, NEG)

    def finish(branch, acc):
        for h in range(hg):
            hs = slice(h * tq, (h + 1) * tq)
            o = acc[0:HEAD_DIM, hs] * (1.0 / acc[HEAD_DIM:HEAD_DIM + 1, hs])
            yacc_ref[h * HEAD_DIM:(h + 1) * HEAD_DIM, :] += o * gt_ref[0, 3 * h + branch:3 * h + branch + 1, :]

    def q_biased(grp):
        band = selb_ref[pl.ds(pl.multiple_of(grp * BIAS_ROWS, BIAS_ROWS), BIAS_ROWS), :].astype(BF16)
        return jnp.concatenate([qr_t, jnp.concatenate([band] * hg, axis=1),
                                q_pad[BIAS_ROWS:]], axis=0)

    def score_tile(grp, j, q4, dst_ref):
        st = _dot(key_rows(kslc_ref, grp * KT_GROUP + j), q4)
        dst_ref[j * tq:(j + 1) * tq, :] = st
        return jnp.max(st, axis=0, keepdims=True)

    def probs(scores, m):
        return jnp.concatenate(
            [jnp.exp2(scores(slice(h * tq, (h + 1) * tq)) - m[:, h * tq:(h + 1) * tq]).astype(BF16)
             for h in range(hg)], axis=1)

    def slc_group(grp, src_ref, dst_ref):
        m_old = m_ref[0:1, :]
        m_new = jnp.maximum(m_old, jnp.max(gmax_ref[0:KT_GROUP, :], axis=0, keepdims=True))
        q_next = q_biased(grp + 1)
        tmax = []
        part = jnp.zeros(acc_ref.shape, F32)
        for j in range(KT_GROUP):
            tmax.append(score_tile(grp + 1, j, q_next, dst_ref))
            p = probs(lambda hs, j=j: src_ref[j * tq:(j + 1) * tq, hs], m_new)
            part = part + _dot(vslct_ref[0, grp * KT_GROUP + j], p)
        acc_ref[...] = acc_ref[...] * jnp.exp2(m_old - m_new) + part
        m_ref[0:1, :] = m_new
        gmax_ref[0:KT_GROUP, :] = jnp.concatenate(tmax, axis=0)

    def score_first(dst_ref):
        q4 = q_biased(0)
        wmax = wmax_ref[0:1, :]
        part = jnp.zeros(acc_ref.shape, F32)
        tmax = []
        for j in range(KT_GROUP):
            tmax.append(score_tile(0, j, q4, dst_ref))
            if j <= WIN_TILES:
                part = part + window_tile(j, wmax)
        gmax_ref[0:KT_GROUP, :] = jnp.concatenate(tmax, axis=0)
        finish(2, part)

    reset()
    n_grp = i // KT_GROUP
    odd = n_grp % 2

    @pl.when(odd == 0)
    def _():
        score_first(ga_ref)

    @pl.when(odd == 1)
    def _():
        score_first(gb_ref)
        slc_group(0, gb_ref, ga_ref)

    def slc_pair(pair, carry):
        slc_group(2 * pair + odd, ga_ref, gb_ref)
        slc_group(2 * pair + odd + 1, gb_ref, ga_ref)
        return carry

    lax.fori_loop(0, n_grp // 2, slc_pair, 0)

    n_rest = i - n_grp * KT_GROUP
    causal = jnp.where((start + k_iota) <= t_row, 0.0, NEG)
    diag_rows = pl.ds(pl.multiple_of(n_rest * tq, tq), tq)
    diag = lambda hs: ga_ref[diag_rows, hs] + causal
    m_old = m_ref[0:1, :]
    m_new = jnp.maximum(m_old, jnp.concatenate(
        [jnp.max(diag(slice(h * tq, (h + 1) * tq)), axis=0, keepdims=True) for h in range(hg)], axis=1))
    for j in range(KT_GROUP - 1):
        m_new = jnp.maximum(m_new, jnp.where(j < n_rest, gmax_ref[j:j + 1, :], NEG))
    acc_ref[...] = acc_ref[...] * jnp.exp2(m_old - m_new) + _dot(vslct_ref[0, i], probs(diag, m_new))

    def rest_tile(j, carry):
        rows = pl.ds(pl.multiple_of(j * tq, tq), tq)
        acc_ref[...] += _dot(vslct_ref[0, n_grp * KT_GROUP + j], probs(lambda hs: ga_ref[rows, hs], m_new))
        return carry

    lax.fori_loop(0, n_rest, rest_tile, 0)
    finish(1, acc_ref)

    y_ref[0] = yacc_ref[...].T.astype(BF16)


def _nsa(qpt, qrt, gt, kc, vct, kslc, vslct, kwin, vwint):
    b, _, seq = qpt.shape
    tq = ATT_T
    hg = HEADS_PER_GROUP
    n_cmp = kc.shape[1]
    nkt = seq // tq
    qspec = pl.BlockSpec((1, hg * HEAD_DIM, tq), lambda bi, g, i: (bi, g, i))
    in_specs = [
        qspec, qspec,
        pl.BlockSpec((1, LANES // NSA_GROUPS, tq), lambda bi, g, i: (bi, g, i)),
        pl.BlockSpec((1, n_cmp, LANES), lambda bi, g, i: (bi, 0, g)),
        pl.BlockSpec((1, LANES, n_cmp), lambda bi, g, i: (bi, g, 0)),
        _resident((1, seq, LANES), lambda bi, g, i: (bi, 0, g)),
        _resident((1, nkt, None, VT_ROWS, tq), lambda bi, g, i: (bi, 0, g, 0, 0)),
        _resident((1, seq, LANES), lambda bi, g, i: (bi, 0, g)),
        _resident((1, nkt, None, VT_ROWS, tq), lambda bi, g, i: (bi, 0, g, 0, 0)),
    ]
    return pl.pallas_call(
        _nsa_body,
        grid=(b, NSA_GROUPS, seq // tq),
        in_specs=in_specs,
        out_specs=pl.BlockSpec((1, tq, hg * HEAD_DIM), lambda bi, g, i: (bi, i, g)),
        out_shape=jax.ShapeDtypeStruct((b, seq, NSA_HEADS * HEAD_DIM), BF16),
        scratch_shapes=[pltpu.VMEM((VT_ROWS, hg * tq), F32),
                        pltpu.VMEM((8, hg * tq), F32),
                        pltpu.VMEM((seq // SLC_BLOCK, tq), F32),
                        pltpu.VMEM(((WIN_TILES + 1) * tq, hg * tq), F32),
                        pltpu.VMEM((8, hg * tq), F32),
                        pltpu.VMEM((KT_GROUP * tq, hg * tq), F32),
                        pltpu.VMEM((KT_GROUP * tq, hg * tq), F32),
                        pltpu.VMEM((8, hg * tq), F32),
                        pltpu.VMEM((hg * HEAD_DIM, tq), F32)],
        compiler_params=_params(3),
        name="nsa",
    )(qpt, qrt, gt, kc, vct, kslc, vslct, kwin, vwint)


def _merge_ffn_body(x_ref, part_ref, gnsa_ref, ynsa_ref, wbn_ref, wo_ref, gmix_ref,
                    gpre_ref, gpost_ref, wa_ref, wb_ref, wout_ref, o_ref):
    merged = part_ref[...] + gnsa_ref[...].astype(F32) * _dot(ynsa_ref[...], wbn_ref[...])
    x = x_ref[...] + _rms(_dot(merged.astype(BF16), wo_ref[...]), gmix_ref[...])
    h = _rms(x, gpre_ref[...]).astype(BF16)
    y = jnp.zeros(x.shape, F32)
    for c0, c1 in zip(MERGE_FF_SPLITS[:-1], MERGE_FF_SPLITS[1:]):
        cs = slice(c0, c1)
        a = _dot(h, wa_ref[:, cs])
        g = (a * jax.nn.sigmoid(a) * _dot(h, wb_ref[:, cs])).astype(BF16)
        y = y + _dot(g, wout_ref[cs, :])
    o_ref[...] = x + 0.5 * _rms(y, gpost_ref[...])


def _merge_ffn(x, part, gnsa, ynsa, l, *params):
    t = x.shape[0]
    tm = MERGE_TM
    row = lambda i: (i, 0)
    return pl.pallas_call(
        _merge_ffn_body,
        grid=(t // tm,),
        in_specs=[pl.BlockSpec((tm, D_MODEL), row), pl.BlockSpec((tm, D_MODEL), row),
                  pl.BlockSpec((tm, D_MODEL), row), pl.BlockSpec((tm, NSA_HEADS * HEAD_DIM), row)]
        + [_layer(p, l) for p in params],
        out_specs=pl.BlockSpec((tm, D_MODEL), row),
        out_shape=jax.ShapeDtypeStruct((t, D_MODEL), F32),
        compiler_params=_params(1),
        name="merge_ffn",
    )(x, part, gnsa, ynsa, *params)


def _pad_groups(w, groups, width, stride):
    d = w.shape[0]
    w = w.reshape(d, groups, width)
    return jnp.pad(w, ((0, 0), (0, 0), (0, stride - width))).reshape(d, groups * stride)


def _mix_weight(w):
    o = 0
    u = w[:, o:o + CONV_DIM]; o += CONV_DIM
    bg = w[:, o:o + CONV_DIM]; o += CONV_DIM
    cg = w[:, o:o + CONV_DIM]; o += CONV_DIM
    q = w[:, o:o + NSA_HEADS * HEAD_DIM]; o += NSA_HEADS * HEAD_DIM
    kv = w[:, o:o + 6 * NSA_GROUPS * HEAD_DIM].reshape(D_MODEL, 6, NSA_GROUPS * HEAD_DIM)
    o += 6 * NSA_GROUPS * HEAD_DIM
    gate = w[:, o:o + 3 * NSA_HEADS]; o += 3 * NSA_HEADS
    qmem = w[:, o:o + MEM_HEADS * MEM_HEAD_DIM]; o += MEM_HEADS * MEM_HEAD_DIM
    merge = w[:, o:]
    cols = [u, bg, cg, q, kv[:, 2], kv[:, 4], kv[:, 3], kv[:, 5], kv[:, 0], kv[:, 1],
            _pad_groups(gate, NSA_GROUPS, 3 * HEADS_PER_GROUP, LANES // NSA_GROUPS), qmem, merge]
    out = jnp.concatenate(cols, axis=1).astype(BF16)
    assert out.shape == (D_MODEL, C_TOTAL)
    return out


def _cmp_weights(pos_emb, w1, b1, w2):
    eye = jnp.eye(NSA_GROUPS, dtype=F32)
    w1r = w1.reshape(CMP_BLOCK, HEAD_DIM, CMP_HIDDEN)
    width = CMP_STRIDE * NSA_GROUPS * HEAD_DIM

    def spread(a):
        return jnp.einsum('pdj,gh->pgdhj', a, eye).reshape(width, NSA_GROUPS * CMP_HIDDEN).astype(BF16)

    def pe_row(p):
        return jnp.broadcast_to(p[:, None, :], (CMP_STRIDE, NSA_GROUPS, HEAD_DIM)).reshape(1, width)

    pe = jnp.concatenate([pe_row(pos_emb[:CMP_STRIDE]), pe_row(pos_emb[CMP_STRIDE:]),
                          jnp.zeros((6, width), F32)], axis=0)
    w2p = jnp.pad(w2, ((0, 0), (0, LANES - HEAD_DIM)))
    w2b = jnp.einsum('jc,gh->gjhc', w2p, eye).reshape(NSA_GROUPS * CMP_HIDDEN, NSA_GROUPS * LANES)
    return (pe, spread(w1r[:CMP_STRIDE]), spread(w1r[CMP_STRIDE:]),
            jnp.tile(b1, NSA_GROUPS)[None, :], w2b.astype(BF16))


def _rope_table():
    half = ROT_DIM // 2
    lane = np.arange(LANES)
    tab = jnp.zeros((8, LANES), F32)
    d = lane % HEAD_DIM
    tab = tab.at[1].set(jnp.asarray(np.where(d < half, -1.0, 0.0), F32))
    tab = tab.at[2].set(jnp.asarray(np.where((d >= half) & (d < ROT_DIM), 1.0, 0.0), F32))
    tab = tab.at[3].set(jnp.asarray(np.where(lane == HEAD_DIM, 1.0, 0.0), F32))
    return tab


def kernel(x, mem, positions, ffn1_norm_pre, ffn1_norm_post, ffn1_w_in, ffn1_w_out, mix_norm_pre, mix_norm_post, mem_norm, w_mix_in, conv_w, cmp_pos_k, cmp_pos_v, cmp_k_w1, cmp_k_b1, cmp_k_w2, cmp_v_w1, cmp_v_b1, cmp_v_w2, w_mem_kv, w_branch_conv, w_branch_nsa, w_branch_mem, w_mix_out, ffn2_norm_pre, ffn2_norm_post, ffn2_w_in, ffn2_w_out):
    batch, seq, _ = x.shape
    t = batch * seq
    assert x.shape[-1] == D_MODEL and seq % MIX_TM == 0 and t % FFN_TM == 0 and t % MERGE_TM == 0
    assert MIX_TM % ATT_T == 0 and (seq // CMP_STRIDE) % CMP_CHUNK == 0 and (seq // ATT_T) % KT_GROUP == 0
    assert KT_GROUP > WIN_TILES and KT_GROUP * ATT_T >= seq // CMP_STRIDE
    xf = x.reshape(t, D_MODEL)
    cos_t, sin_t = _rope_trig(positions)
    tab = _rope_table()
    one = jnp.tile(tab[3:4, :], (1, NSA_GROUPS))

    vec = lambda g: g[:, None, :]
    cast = lambda w: w.astype(BF16)
    ffn1 = (vec(ffn1_norm_pre), vec(ffn1_norm_post), cast(ffn1_w_in[:, :, :D_FF]), cast(ffn1_w_in[:, :, D_FF:]),
            cast(ffn1_w_out))
    merge_ffn2 = (cast(w_branch_nsa), cast(w_mix_out), vec(mix_norm_post), vec(ffn2_norm_pre), vec(ffn2_norm_post),
                  cast(ffn2_w_in[:, :, :D_FF]), cast(ffn2_w_in[:, :, D_FF:]), cast(ffn2_w_out))
    w_main = jax.vmap(_mix_weight)(w_mix_in)
    convw = jnp.pad(conv_w, ((0, 0), (0, 8 - CONV_WIDTH), (0, 0)))
    cmp_k = jax.vmap(_cmp_weights)(cmp_pos_k, cmp_k_w1, cmp_k_b1, cmp_k_w2)
    cmp_v = jax.vmap(_cmp_weights)(cmp_pos_v, cmp_v_w1, cmp_v_b1, cmp_v_w2)
    mem_g, mem_w = vec(mem_norm), cast(w_mem_kv)
    mix_g, wbc, wbm = vec(mix_norm_pre), cast(w_branch_conv), cast(w_branch_mem)

    for l in range(DEPTH):
        xf = _ffn(xf, l, *ffn1)
        mk, mv = _memkv(mem, mem_g, mem_w, l)
        (part, gnsa, qpt, qrt, kslc, kwin, vslct, vwint, kcr, vcr, gt) = _mixin(
            xf, cos_t, sin_t, mix_g, tab, w_main, convw, mk, mv, wbc, wbm, batch, seq, l)
        kc = _compress(kcr.reshape(batch, seq, LANES), cmp_k, one, False, l)
        vct = _compress(vcr.reshape(batch, seq, LANES), cmp_v, one, True, l)
        ynsa = _nsa(qpt, qrt, gt, kc, vct,
                    kslc.reshape(batch, seq, NSA_GROUPS * LANES), vslct,
                    kwin.reshape(batch, seq, NSA_GROUPS * LANES), vwint)
        xf = _merge_ffn(xf, part, gnsa, ynsa.reshape(t, NSA_HEADS * HEAD_DIM), l, *merge_ffn2)
    return xf.reshape(batch, seq, D_MODEL)
```
